```python
import jax, jax.numpy as jnp
from jax import lax
import numpy as np

D_MODEL = 1024
BATCH = 4
SEQ = 8192
DEPTH = 2
DEC_BATCH = 8
DEC_SEQ = 32
PAST_LEN = 2048

CHUNK = 64
N_LRU_LAYERS = (DEPTH + 1) // 2
N_ATT_LAYERS = DEPTH // 2
LRU_WIDTH = D_MODEL
LRU_BLOCKS = 4
LRU_BLOCK_W = LRU_WIDTH // LRU_BLOCKS
CONV_W = 4
RG_C = 8.0
N_HEADS = 8
HEAD_DIM = D_MODEL // N_HEADS
LEFT_CHUNKS = 8
LEFT_ROWS = LEFT_CHUNKS * CHUNK
BAND = (LEFT_CHUNKS + 1) * CHUNK
REL_CLIP = 128
N_REL = 2 * REL_CLIP + 1
PEER_HEADS = 8
N_KEYS = 128
N_EXPERTS = N_KEYS * N_KEYS
PEER_TOPK = 16
PEER_QDIM = 256
PEER_HALF = PEER_QDIM // 2
PEER_BLOCK = 256
EPS = 1e-6
NEG_INF = -1e30

kernel_name = "hybrid_streaming_rglru_chunkattn_peer_step"


def _rms_norm(x, g):
    xf = x.astype(jnp.float32)
    y = xf * lax.rsqrt(jnp.mean(xf * xf, axis=-1, keepdims=True) + EPS)
    return (y * g.astype(jnp.float32)).astype(x.dtype)


def _ada(c, w, b):
    m = jnp.einsum('bd,de->be', jax.nn.silu(c), w) + b
    return jnp.split(m[:, None, :], 6, axis=-1)


def _causal_conv(u, conv_state, w, b):
    T = u.shape[1]
    full = jnp.concatenate([conv_state.astype(u.dtype), u], axis=1)
    out = b + sum(full[:, k:k + T] * w[k] for k in range(CONV_W))
    return out, full[:, -(CONV_W - 1):]


def _rglru(u, h0, gate_w, gate_b, lam):
    B, T, W = u.shape
    ub = u.reshape(B, T, LRU_BLOCKS, LRU_BLOCK_W)
    gr = jnp.einsum('btnk,nkj->btnj', ub, gate_w[0]).reshape(B, T, W) + gate_b[0]
    gi = jnp.einsum('btnk,nkj->btnj', ub, gate_w[1]).reshape(B, T, W) + gate_b[1]
    r = jax.nn.sigmoid(gr.astype(jnp.float32))
    i = jax.nn.sigmoid(gi.astype(jnp.float32))
    log_a = -RG_C * r * jax.nn.softplus(-lam.astype(jnp.float32))
    a = jnp.exp(log_a)
    mult = jnp.sqrt(-jnp.expm1(2.0 * log_a))
    bterm = mult * i * u.astype(jnp.float32)
    bterm = bterm.at[:, 0].add(a[:, 0] * h0.astype(jnp.float32))

    def comb(left, right):
        al, bl = left
        ar, br = right
        return al * ar, ar * bl + br

    _, h = lax.associative_scan(comb, (a, bterm), axis=1)
    return h.astype(u.dtype), h[:, -1]


def _recurrent_mixer(h, conv_state, h0, w_in, conv_w, conv_b, gate_w, gate_b, lam, w_out):
    proj = jnp.einsum('btd,de->bte', h, w_in)
    gate_branch, rec_branch = jnp.split(proj, 2, axis=-1)
    gate_branch = jax.nn.gelu(gate_branch)
    u, new_conv = _causal_conv(rec_branch, conv_state, conv_w, conv_b)
    y, h_last = _rglru(u, h0, gate_w, gate_b, lam)
    out = jnp.einsum('btw,wd->btd', y * gate_branch, w_out)
    return out, new_conv, h_last


def _attend(q, k, v, qpos, kpos, kvalid, rel_bias):
    s = jnp.einsum('bqhd,bkhd->bhqk', q, k).astype(jnp.float32) * (HEAD_DIM ** -0.5)
    rel = jnp.clip(qpos[:, None] - kpos[None, :], -REL_CLIP, REL_CLIP) + REL_CLIP
    s = s + rel_bias[:, rel].astype(jnp.float32)[None]
    s = jnp.where(kvalid[None, None, None, :], s, NEG_INF)
    p = jax.nn.softmax(s, axis=-1).astype(v.dtype)
    return jnp.einsum('bhqk,bkhd->bqhd', p, v)


def _chunk_band_attention(q, k, v, rel_bias):
    B, T, H, Dh = q.shape
    nc = T // CHUNK
    kp = jnp.pad(k, ((0, 0), (LEFT_ROWS, 0), (0, 0), (0, 0)))
    vp = jnp.pad(v, ((0, 0), (LEFT_ROWS, 0), (0, 0), (0, 0)))

    def one_chunk(n):
        start = n * CHUNK
        q_n = lax.dynamic_slice_in_dim(q, start, CHUNK, axis=1)
        k_n = lax.dynamic_slice_in_dim(kp, start, BAND, axis=1)
        v_n = lax.dynamic_slice_in_dim(vp, start, BAND, axis=1)
        qpos = start + jnp.arange(CHUNK)
        kpos = start - LEFT_ROWS + jnp.arange(BAND)
        return _attend(q_n, k_n, v_n, qpos, kpos, kpos >= 0, rel_bias)

    outs = lax.map(one_chunk, jnp.arange(nc))
    return jnp.moveaxis(outs, 0, 1).reshape(B, T, H, Dh)


def _attention_mixer(h, k_cache, v_cache, w_qkv, q_gain, k_gain, rel_bias, w_o, prompt):
    B, T, _ = h.shape
    qkv = jnp.einsum('btd,de->bte', h, w_qkv).reshape(B, T, 3, N_HEADS, HEAD_DIM)
    q = _rms_norm(qkv[:, :, 0], q_gain)
    k = _rms_norm(qkv[:, :, 1], k_gain)
    v = qkv[:, :, 2]
    if prompt:
        o = _chunk_band_attention(q, k, v, rel_bias)
        keep = min(LEFT_ROWS, T)
        new_k, new_v = k[:, T - keep:], v[:, T - keep:]
    else:
        lc = k_cache.shape[1]
        qpos = PAST_LEN + jnp.arange(T)
        kpos = jnp.concatenate([PAST_LEN - lc + jnp.arange(lc), qpos])
        k_all = jnp.concatenate([k_cache.astype(k.dtype), k], axis=1)
        v_all = jnp.concatenate([v_cache.astype(v.dtype), v], axis=1)
        o = _attend(q, k_all, v_all, qpos, kpos, jnp.ones((lc + T,), bool), rel_bias)
        new_k, new_v = k, v
    out = jnp.einsum('bte,ed->btd', o.reshape(B, T, D_MODEL), w_o)
    return out, new_k, new_v


def _peer(h, w_query, sub_keys, u_tab, v_tab):
    B, T, D = h.shape
    x = h.reshape(B * T, D)
    n = B * T
    nblk = -(-n // PEER_BLOCK)
    xp = jnp.pad(x, ((0, nblk * PEER_BLOCK - n), (0, 0))).reshape(nblk, PEER_BLOCK, D)

    def block(xb):
        q = jnp.einsum('nd,de->ne', xb, w_query).reshape(PEER_BLOCK, PEER_HEADS, 2, PEER_HALF)
        s = jnp.einsum('nhpk,hpek->nhpe', q, sub_keys).astype(jnp.float32)
        ts, ti = lax.top_k(s, PEER_TOPK)
        cs = ts[:, :, 0, :, None] + ts[:, :, 1, None, :]
        ci = ti[:, :, 0, :, None] * N_KEYS + ti[:, :, 1, None, :]
        fs, fi = lax.top_k(cs.reshape(PEER_BLOCK, PEER_HEADS, PEER_TOPK * PEER_TOPK), PEER_TOPK)
        eidx = jnp.take_along_axis(ci.reshape(PEER_BLOCK, PEER_HEADS, -1), fi, axis=-1)
        g = jax.nn.softmax(fs, axis=-1)
        u = u_tab[eidx]
        vv = v_tab[eidx]
        act = jax.nn.gelu(jnp.einsum('nd,nhkd->nhk', xb, u).astype(jnp.float32))
        return jnp.einsum('nhk,nhkd->nd', (g * act).astype(xb.dtype), vv)

    out = lax.map(block, xp).reshape(nblk * PEER_BLOCK, D)[:n]
    return out.reshape(B, T, D)


def _trunk(x, c, conv_st, lru_st, k_cache, v_cache, w, prompt):
    B = x.shape[0]
    new_conv, new_h, new_k, new_v = [], [], [], []
    for i in range(DEPTH):
        sh1, sc1, g1, sh2, sc2, g2 = _ada(c, w['ada_w'][i], w['ada_b'][i])
        hn = _rms_norm(x, w['norm_g'][i, 0]) * (1 + sc1) + sh1
        j = i // 2
        if i % 2 == 0:
            if prompt:
                cs0 = jnp.zeros((B, CONV_W - 1, LRU_WIDTH), x.dtype)
                h0 = jnp.zeros((B, LRU_WIDTH), jnp.float32)
            else:
                cs0, h0 = conv_st[j], lru_st[j]
            out, cnew, hnew = _recurrent_mixer(hn, cs0, h0, w['lru_w_in'][j], w['lru_conv_w'][j],
                                               w['lru_conv_b'][j], w['lru_gate_w'][j], w['lru_gate_b'][j],
                                               w['lru_lambda'][j], w['lru_w_out'][j])
            new_conv.append(cnew)
            new_h.append(hnew)
        else:
            kc = None if prompt else k_cache[j]
            vc = None if prompt else v_cache[j]
            out, kn, vn = _attention_mixer(hn, kc, vc, w['att_w_qkv'][j], w['att_q_gain'][j],
                                           w['att_k_gain'][j], w['att_rel_bias'][j], w['att_w_o'][j], prompt)
            new_k.append(kn)
            new_v.append(vn)
        x = x + g1 * out
        hn = _rms_norm(x, w['norm_g'][i, 1]) * (1 + sc2) + sh2
        x = x + g2 * _peer(hn, w['peer_w_query'][i], w['peer_sub_keys'][i], w['peer_u'][i], w['peer_v'][i])
    return x, jnp.stack(new_conv), jnp.stack(new_h), jnp.stack(new_k), jnp.stack(new_v)


def setup_inputs(seed: int = 0) -> dict:
    key = jax.random.key(seed)
    ks = jax.random.split(key, 32)
    f32 = jnp.float32

    def nrm(k, shape, s):
        return jax.random.normal(k, shape, f32) * s

    lc = min(LEFT_ROWS, PAST_LEN)
    a_c = jax.random.uniform(ks[20], (N_LRU_LAYERS, LRU_WIDTH), f32, 0.9, 0.999)
    a_base = a_c ** (1.0 / RG_C)
    lru_lambda = jnp.log(a_base) - jnp.log1p(-a_base)
    return {
        "x_prompt": nrm(ks[0], (BATCH, SEQ, D_MODEL), 1.0),
        "x_sample": nrm(ks[1], (DEC_BATCH, DEC_SEQ, D_MODEL), 1.0),
        "c_prompt": nrm(ks[2], (BATCH, D_MODEL), 1.0),
        "c_sample": nrm(ks[3], (DEC_BATCH, D_MODEL), 1.0),
        "state_conv": nrm(ks[4], (N_LRU_LAYERS, DEC_BATCH, CONV_W - 1, LRU_WIDTH), 0.5),
        "state_lru_h": nrm(ks[5], (N_LRU_LAYERS, DEC_BATCH, LRU_WIDTH), 0.5),
        "cache_k": nrm(ks[6], (N_ATT_LAYERS, DEC_BATCH, lc, N_HEADS, HEAD_DIM), 1.0),
        "cache_v": nrm(ks[7], (N_ATT_LAYERS, DEC_BATCH, lc, N_HEADS, HEAD_DIM), 1.0),
        "norm_g": 1.0 + nrm(ks[8], (DEPTH, 2, D_MODEL), 0.02),
        "ada_w": nrm(ks[9], (DEPTH, D_MODEL, 6 * D_MODEL), D_MODEL ** -0.5),
        "ada_b": nrm(ks[10], (DEPTH, 6 * D_MODEL), 0.02),
        "lru_w_in": nrm(ks[11], (N_LRU_LAYERS, D_MODEL, 2 * LRU_WIDTH), D_MODEL ** -0.5),
        "lru_conv_w": nrm(ks[12], (N_LRU_LAYERS, CONV_W, LRU_WIDTH), CONV_W ** -0.5),
        "lru_conv_b": nrm(ks[13], (N_LRU_LAYERS, LRU_WIDTH), 0.02),
        "lru_gate_w": nrm(ks[14], (N_LRU_LAYERS, 2, LRU_BLOCKS, LRU_BLOCK_W, LRU_BLOCK_W), LRU_BLOCK_W ** -0.5),
        "lru_gate_b": nrm(ks[15], (N_LRU_LAYERS, 2, LRU_WIDTH), 0.02),
        "lru_lambda": lru_lambda,
        "lru_w_out": nrm(ks[16], (N_LRU_LAYERS, LRU_WIDTH, D_MODEL), LRU_WIDTH ** -0.5),
        "att_w_qkv": nrm(ks[17], (N_ATT_LAYERS, D_MODEL, 3 * D_MODEL), D_MODEL ** -0.5),
        "att_q_gain": 1.0 + nrm(ks[18], (N_ATT_LAYERS, HEAD_DIM), 0.02),
        "att_k_gain": 1.0 + nrm(ks[19], (N_ATT_LAYERS, HEAD_DIM), 0.02),
        "att_rel_bias": nrm(ks[21], (N_ATT_LAYERS, N_HEADS, N_REL), 0.1),
        "att_w_o": nrm(ks[22], (N_ATT_LAYERS, D_MODEL, D_MODEL), D_MODEL ** -0.5),
        "peer_w_query": nrm(ks[23], (DEPTH, D_MODEL, PEER_HEADS * PEER_QDIM), D_MODEL ** -0.5),
        "peer_sub_keys": nrm(ks[24], (DEPTH, PEER_HEADS, 2, N_KEYS, PEER_HALF), PEER_HALF ** -0.5),
        "peer_u": nrm(ks[25], (DEPTH, N_EXPERTS, D_MODEL), D_MODEL ** -0.5),
        "peer_v": nrm(ks[26], (DEPTH, N_EXPERTS, D_MODEL), PEER_HEADS ** -0.5),
    }


def reference(x_prompt, x_sample, c_prompt, c_sample, state_conv, state_lru_h, cache_k, cache_v,
              norm_g, ada_w, ada_b, lru_w_in, lru_conv_w, lru_conv_b, lru_gate_w, lru_gate_b, lru_lambda,
              lru_w_out, att_w_qkv, att_q_gain, att_k_gain, att_rel_bias, att_w_o,
              peer_w_query, peer_sub_keys, peer_u, peer_v):
    w = {
        'norm_g': norm_g, 'ada_w': ada_w, 'ada_b': ada_b,
        'lru_w_in': lru_w_in, 'lru_conv_w': lru_conv_w, 'lru_conv_b': lru_conv_b,
        'lru_gate_w': lru_gate_w, 'lru_gate_b': lru_gate_b, 'lru_lambda': lru_lambda, 'lru_w_out': lru_w_out,
        'att_w_qkv': att_w_qkv, 'att_q_gain': att_q_gain, 'att_k_gain': att_k_gain,
        'att_rel_bias': att_rel_bias, 'att_w_o': att_w_o,
        'peer_w_query': peer_w_query, 'peer_sub_keys': peer_sub_keys, 'peer_u': peer_u, 'peer_v': peer_v,
    }
    y_prompt, p_conv, p_h, p_k, p_v = _trunk(x_prompt, c_prompt, None, None, None, None, w, True)
    y_sample, s_conv, s_h, s_k, s_v = _trunk(x_sample, c_sample, state_conv, state_lru_h,
                                              cache_k, cache_v, w, False)
    return (y_prompt, y_sample, p_conv, p_h, p_k, p_v, s_conv, s_h, s_k, s_v)
```

```python
import functools

import jax
import jax.numpy as jnp
from jax import lax
from jax.experimental import pallas as pl
from jax.experimental.pallas import tpu as pltpu

D_MODEL = 1024
CHUNK = 64
LRU_WIDTH = D_MODEL
LRU_BLOCKS = 4
LRU_BLOCK_W = LRU_WIDTH // LRU_BLOCKS
CONV_W = 4
RG_C = 8.0
N_HEADS = 8
HEAD_DIM = D_MODEL // N_HEADS
LEFT_CHUNKS = 8
LEFT_ROWS = LEFT_CHUNKS * CHUNK
BAND = (LEFT_CHUNKS + 1) * CHUNK
REL_CLIP = 128
N_REL = 2 * REL_CLIP + 1
PEER_HEADS = 8
N_KEYS = 128
N_EXPERTS = N_KEYS * N_KEYS
PEER_TOPK = 16
PEER_QDIM = 256
PEER_HALF = PEER_QDIM // 2
PEER_SEL = PEER_HEADS * PEER_TOPK
EPS = 1e-6
NEG_INF = -1e30
PAST_LEN = 2048

LANES = 128
BF16 = jnp.bfloat16
F32 = jnp.float32


def _gelu_tanh(x):
    return 0.5 * x * (1.0 + jnp.tanh(0.7978845608028654 * (x + 0.044715 * (x * x * x))))


def _split3(x):
    hi = x.astype(BF16)
    r1 = x - hi.astype(F32)
    mid = r1.astype(BF16)
    lo = (r1 - mid.astype(F32)).astype(BF16)
    return hi, mid, lo


def _exact_select_dot(x, sel):
    hi, mid, lo = _split3(x)
    d = lambda a: jnp.dot(a, sel, preferred_element_type=F32)
    return d(hi) + d(mid) + d(lo)


def _peer_route_kernel(x_ref, sh_ref, sc_ref, ng_ref, wq_ref, sk_ref,
                       hn_ref, idx_ref, g_ref,
                       q_s, ts_s, ti_s, fs_s, ei_s, mx_s):
    tt = x_ref.shape[1]
    x = x_ref[0]
    ms = jnp.mean(x * x, axis=-1, keepdims=True)
    hn = x * lax.rsqrt(ms + EPS) * ng_ref[...]
    hn = hn * (1.0 + sc_ref[0]) + sh_ref[0]
    hn_ref[0] = hn
    q = jnp.dot(hn.astype(BF16), wq_ref[...], preferred_element_type=F32)
    for gidx in range(2 * PEER_HEADS):
        q_s[gidx] = q[:, gidx * PEER_HALF:(gidx + 1) * PEER_HALF]

    lane = lax.broadcasted_iota(jnp.int32, (tt, LANES), 1)

    def list_body(gidx, _):
        s0 = lax.dot_general(q_s[gidx].astype(BF16), sk_ref[gidx], (((1,), (1,)), ((), ())),
                             preferred_element_type=F32)

        def pick(j, carry):
            s, ts, ti = carry
            m = jnp.max(s, axis=-1, keepdims=True)
            am = jnp.min(jnp.where(s == m, lane, N_KEYS), axis=-1, keepdims=True)
            ts = jnp.where(lane == j, m, ts)
            ti = jnp.where(lane == j, am.astype(F32), ti)
            s = jnp.where(lane == am, -jnp.inf, s)
            return s, ts, ti

        _, ts, ti = lax.fori_loop(0, PEER_TOPK, pick,
                                  (s0, jnp.zeros((tt, LANES), F32), jnp.zeros((tt, LANES), F32)))
        ts_s[gidx] = ts
        ti_s[gidx] = ti
        return 0

    lax.fori_loop(0, 2 * PEER_HEADS, list_body, 0)

    ncand = PEER_TOPK * PEER_TOPK
    row = lax.broadcasted_iota(jnp.int32, (LANES, ncand), 0)
    col = lax.broadcasted_iota(jnp.int32, (LANES, ncand), 1)
    sel_a = (row == col // PEER_TOPK).astype(BF16)
    sel_b = (row == col % PEER_TOPK).astype(BF16)
    cand = lax.broadcasted_iota(jnp.int32, (tt, ncand), 1)

    fs_s[...] = jnp.zeros((tt, LANES), F32)
    ei_s[...] = jnp.zeros((tt, LANES), F32)
    mx_s[...] = jnp.zeros((tt, LANES), F32)

    def head_body(h, _):
        ts1, ts2 = ts_s[2 * h], ts_s[2 * h + 1]
        ti1, ti2 = ti_s[2 * h], ti_s[2 * h + 1]
        cs0 = _exact_select_dot(ts1, sel_a) + _exact_select_dot(ts2, sel_b)
        ci = (jnp.dot(ti1.astype(BF16), sel_a, preferred_element_type=F32) * float(N_KEYS)
              + jnp.dot(ti2.astype(BF16), sel_b, preferred_element_type=F32))

        def pick(k, cs):
            m = jnp.max(cs, axis=-1, keepdims=True)
            cmin = jnp.min(jnp.where(cs == m, cand, ncand), axis=-1, keepdims=True)
            sel = cand == cmin
            e = jnp.sum(jnp.where(sel, ci, 0.0), axis=-1, keepdims=True)
            slot = lane == h * PEER_TOPK + k
            fs_s[...] = jnp.where(slot, m, fs_s[...])
            ei_s[...] = jnp.where(slot, e, ei_s[...])

            @pl.when(k == 0)
            def _():
                mx_s[...] = jnp.where(lane // PEER_TOPK == h, m, mx_s[...])

            return jnp.where(sel, -jnp.inf, cs)

        lax.fori_loop(0, PEER_TOPK, pick, cs0)
        return 0

    lax.fori_loop(0, PEER_HEADS, head_body, 0)

    ex = jnp.exp(fs_s[...] - mx_s[...])
    r2 = lax.broadcasted_iota(jnp.int32, (LANES, LANES), 0)
    c2 = lax.broadcasted_iota(jnp.int32, (LANES, LANES), 1)
    grp = (r2 // PEER_TOPK == c2 // PEER_TOPK).astype(BF16)
    den = _exact_select_dot(ex, grp)
    g_ref[0] = ex / den
    idx_ref[0] = ei_s[...].astype(jnp.int32)


def _peer_route(x, sh, sc, ng, wq, sk):
    B, T, D = x.shape
    tt = min(T, 128)
    grid = (B, T // tt)
    tok = lambda b, t: (b, t, 0)
    per_b = lambda b, t: (b, 0, 0)
    return pl.pallas_call(
        _peer_route_kernel,
        grid=grid,
        in_specs=[
            pl.BlockSpec((1, tt, D), tok),
            pl.BlockSpec((1, 1, D), per_b),
            pl.BlockSpec((1, 1, D), per_b),
            pl.BlockSpec((1, D), lambda b, t: (0, 0)),
            pl.BlockSpec(wq.shape, lambda b, t: (0, 0)),
            pl.BlockSpec(sk.shape, lambda b, t: (0, 0, 0)),
        ],
        out_specs=[
            pl.BlockSpec((1, tt, D), tok),
            pl.BlockSpec((1, tt, PEER_SEL), tok),
            pl.BlockSpec((1, tt, PEER_SEL), tok),
        ],
        out_shape=[
            jax.ShapeDtypeStruct((B, T, D), F32),
            jax.ShapeDtypeStruct((B, T, PEER_SEL), jnp.int32),
            jax.ShapeDtypeStruct((B, T, PEER_SEL), F32),
        ],
        scratch_shapes=[
            pltpu.VMEM((2 * PEER_HEADS, tt, PEER_HALF), F32),
            pltpu.VMEM((2 * PEER_HEADS, tt, LANES), F32),
            pltpu.VMEM((2 * PEER_HEADS, tt, LANES), F32),
            pltpu.VMEM((tt, LANES), F32),
            pltpu.VMEM((tt, LANES), F32),
            pltpu.VMEM((tt, LANES), F32),
        ],
        compiler_params=pltpu.CompilerParams(
            dimension_semantics=("parallel", "parallel"), vmem_limit_bytes=48 * 1024 * 1024),
        name="peer_route",
    )(x, sh, sc, ng, wq, sk)


def _peer_expert_kernel(idx_hbm, x_ref, hn_ref, g_ref, g2_ref, u_hbm, v_hbm, out_ref,
                        idx_s, ubuf, vbuf, sem, isem):
    b = pl.program_id(0)
    tb_i = pl.program_id(1)
    tb = x_ref.shape[1]
    cp = pltpu.make_async_copy(idx_hbm.at[b, pl.ds(tb_i * tb, tb)], idx_s, isem)
    cp.start()
    cp.wait()

    def issue(t, slot):
        def body(j, _):
            e = idx_s[t, j]
            pltpu.make_async_copy(u_hbm.at[pl.ds(e, 1)], ubuf.at[slot, pl.ds(j, 1)], sem.at[0, slot]).start()
            pltpu.make_async_copy(v_hbm.at[pl.ds(e, 1)], vbuf.at[slot, pl.ds(j, 1)], sem.at[1, slot]).start()
            return 0

        lax.fori_loop(0, PEER_SEL, body, 0)

    def wait(slot):
        pltpu.make_async_copy(u_hbm.at[pl.ds(0, PEER_SEL)], ubuf.at[slot], sem.at[0, slot]).wait()
        pltpu.make_async_copy(v_hbm.at[pl.ds(0, PEER_SEL)], vbuf.at[slot], sem.at[1, slot]).wait()

    issue(0, 0)
    r2 = lax.broadcasted_iota(jnp.int32, (PEER_SEL, LANES), 0)
    c2 = lax.broadcasted_iota(jnp.int32, (PEER_SEL, LANES), 1)
    diag = r2 == c2

    def tok(t, _):
        slot = t % 2

        @pl.when(t + 1 < tb)
        def _():
            issue(t + 1, 1 - slot)

        wait(slot)
        x = hn_ref[0, pl.ds(t, 1), :]
        prod = ubuf[slot] * x
        p = prod[:, 0:LANES]
        for c in range(1, D_MODEL // LANES):
            p = p + prod[:, c * LANES:(c + 1) * LANES]
        act = jnp.sum(p, axis=-1, keepdims=True)
        grow = g_ref[0, pl.ds(t, 1), :]
        gcol = jnp.sum(jnp.where(diag, jnp.broadcast_to(grow, (PEER_SEL, LANES)), 0.0),
                       axis=-1, keepdims=True)
        w = gcol * _gelu_tanh(act)
        o = jnp.sum(vbuf[slot] * w, axis=0, keepdims=True)
        out_ref[0, pl.ds(t, 1), :] = x_ref[0, pl.ds(t, 1), :] + g2_ref[0] * o
        return 0

    lax.fori_loop(0, tb, tok, 0)


def _peer_expert(x, hn, eidx, g, g2, u_tab, v_tab):
    B, T, D = x.shape
    tb = min(T, 64)
    grid = (B, T // tb)
    tok = lambda b, t: (b, t, 0)
    per_b = lambda b, t: (b, 0, 0)
    return pl.pallas_call(
        _peer_expert_kernel,
        grid=grid,
        in_specs=[
            pl.BlockSpec(memory_space=pl.ANY),
            pl.BlockSpec((1, tb, D), tok),
            pl.BlockSpec((1, tb, D), tok),
            pl.BlockSpec((1, tb, PEER_SEL), tok),
            pl.BlockSpec((1, 1, D), per_b),
            pl.BlockSpec(memory_space=pl.ANY),
            pl.BlockSpec(memory_space=pl.ANY),
        ],
        out_specs=pl.BlockSpec((1, tb, D), tok),
        out_shape=jax.ShapeDtypeStruct((B, T, D), F32),
        scratch_shapes=[
            pltpu.SMEM((tb, PEER_SEL), jnp.int32),
            pltpu.VMEM((2, PEER_SEL, D), F32),
            pltpu.VMEM((2, PEER_SEL, D), F32),
            pltpu.SemaphoreType.DMA((2, 2)),
            pltpu.SemaphoreType.DMA,
        ],
        compiler_params=pltpu.CompilerParams(
            dimension_semantics=("parallel", "parallel"), vmem_limit_bytes=32 * 1024 * 1024),
        name="peer_expert",
    )(eidx, x, hn, g, g2, u_tab, v_tab)


def _peer_layer(x, sh2, sc2, g2, ng, wq, sk, u_tab, v_tab):
    hn, eidx, g = _peer_route(x, sh2, sc2, ng, wq, sk)
    return _peer_expert(x, hn, eidx, g, g2, u_tab, v_tab)


def _rms_norm(x, g):
    xf = x.astype(jnp.float32)
    y = xf * lax.rsqrt(jnp.mean(xf * xf, axis=-1, keepdims=True) + EPS)
    return (y * g.astype(jnp.float32)).astype(x.dtype)


def _ada(c, w, b):
    m = jnp.einsum('bd,de->be', jax.nn.silu(c), w) + b
    return jnp.split(m[:, None, :], 6, axis=-1)


def _causal_conv(u, conv_state, w, b):
    T = u.shape[1]
    full = jnp.concatenate([conv_state.astype(u.dtype), u], axis=1)
    out = b + sum(full[:, k:k + T] * w[k] for k in range(CONV_W))
    return out, full[:, -(CONV_W - 1):]


def _rglru(u, h0, gate_w, gate_b, lam):
    B, T, W = u.shape
    ub = u.reshape(B, T, LRU_BLOCKS, LRU_BLOCK_W)
    gr = jnp.einsum('btnk,nkj->btnj', ub, gate_w[0]).reshape(B, T, W) + gate_b[0]
    gi = jnp.einsum('btnk,nkj->btnj', ub, gate_w[1]).reshape(B, T, W) + gate_b[1]
    r = jax.nn.sigmoid(gr.astype(jnp.float32))
    i = jax.nn.sigmoid(gi.astype(jnp.float32))
    log_a = -RG_C * r * jax.nn.softplus(-lam.astype(jnp.float32))
    a = jnp.exp(log_a)
    mult = jnp.sqrt(-jnp.expm1(2.0 * log_a))
    bterm = mult * i * u.astype(jnp.float32)
    bterm = bterm.at[:, 0].add(a[:, 0] * h0.astype(jnp.float32))

    def comb(left, right):
        al, bl = left
        ar, br = right
        return al * ar, ar * bl + br

    _, h = lax.associative_scan(comb, (a, bterm), axis=1)
    return h.astype(u.dtype), h[:, -1]


def _recurrent_mixer(h, conv_state, h0, w_in, conv_w, conv_b, gate_w, gate_b, lam, w_out):
    proj = jnp.einsum('btd,de->bte', h, w_in)
    gate_branch, rec_branch = jnp.split(proj, 2, axis=-1)
    gate_branch = jax.nn.gelu(gate_branch)
    u, new_conv = _causal_conv(rec_branch, conv_state, conv_w, conv_b)
    y, h_last = _rglru(u, h0, gate_w, gate_b, lam)
    out = jnp.einsum('btw,wd->btd', y * gate_branch, w_out)
    return out, new_conv, h_last


def _attend(q, k, v, qpos, kpos, kvalid, rel_bias):
    s = jnp.einsum('bqhd,bkhd->bhqk', q, k).astype(jnp.float32) * (HEAD_DIM ** -0.5)
    rel = jnp.clip(qpos[:, None] - kpos[None, :], -REL_CLIP, REL_CLIP) + REL_CLIP
    s = s + rel_bias[:, rel].astype(jnp.float32)[None]
    s = jnp.where(kvalid[None, None, None, :], s, NEG_INF)
    p = jax.nn.softmax(s, axis=-1).astype(v.dtype)
    return jnp.einsum('bhqk,bkhd->bqhd', p, v)


def _chunk_band_attention(q, k, v, rel_bias):
    B, T, H, Dh = q.shape
    nc = T // CHUNK
    kp = jnp.pad(k, ((0, 0), (LEFT_ROWS, 0), (0, 0), (0, 0)))
    vp = jnp.pad(v, ((0, 0), (LEFT_ROWS, 0), (0, 0), (0, 0)))

    def one_chunk(n):
        start = n * CHUNK
        q_n = lax.dynamic_slice_in_dim(q, start, CHUNK, axis=1)
        k_n = lax.dynamic_slice_in_dim(kp, start, BAND, axis=1)
        v_n = lax.dynamic_slice_in_dim(vp, start, BAND, axis=1)
        qpos = start + jnp.arange(CHUNK)
        kpos = start - LEFT_ROWS + jnp.arange(BAND)
        return _attend(q_n, k_n, v_n, qpos, kpos, kpos >= 0, rel_bias)

    outs = lax.map(one_chunk, jnp.arange(nc))
    return jnp.moveaxis(outs, 0, 1).reshape(B, T, H, Dh)


def _attention_mixer(h, k_cache, v_cache, w_qkv, q_gain, k_gain, rel_bias, w_o, prompt):
    B, T, _ = h.shape
    qkv = jnp.einsum('btd,de->bte', h, w_qkv).reshape(B, T, 3, N_HEADS, HEAD_DIM)
    q = _rms_norm(qkv[:, :, 0], q_gain)
    k = _rms_norm(qkv[:, :, 1], k_gain)
    v = qkv[:, :, 2]
    if prompt:
        o = _chunk_band_attention(q, k, v, rel_bias)
        keep = min(LEFT_ROWS, T)
        new_k, new_v = k[:, T - keep:], v[:, T - keep:]
    else:
        lc = k_cache.shape[1]
        qpos = PAST_LEN + jnp.arange(T)
        kpos = jnp.concatenate([PAST_LEN - lc + jnp.arange(lc), qpos])
        k_all = jnp.concatenate([k_cache.astype(k.dtype), k], axis=1)
        v_all = jnp.concatenate([v_cache.astype(v.dtype), v], axis=1)
        o = _attend(q, k_all, v_all, qpos, kpos, jnp.ones((lc + T,), bool), rel_bias)
        new_k, new_v = k, v
    out = jnp.einsum('bte,ed->btd', o.reshape(B, T, D_MODEL), w_o)
    return out, new_k, new_v


def _trunk(x, c, conv_st, lru_st, k_cache, v_cache, w, prompt):
    B = x.shape[0]
    new_conv, new_h, new_k, new_v = [], [], [], []
    depth = w['norm_g'].shape[0]
    for i in range(depth):
        sh1, sc1, g1, sh2, sc2, g2 = _ada(c, w['ada_w'][i], w['ada_b'][i])
        hn = _rms_norm(x, w['norm_g'][i, 0]) * (1 + sc1) + sh1
        j = i // 2
        if i % 2 == 0:
            if prompt:
                cs0 = jnp.zeros((B, CONV_W - 1, LRU_WIDTH), x.dtype)
                h0 = jnp.zeros((B, LRU_WIDTH), jnp.float32)
            else:
                cs0, h0 = conv_st[j], lru_st[j]
            out, cnew, hnew = _recurrent_mixer(hn, cs0, h0, w['lru_w_in'][j], w['lru_conv_w'][j],
                                               w['lru_conv_b'][j], w['lru_gate_w'][j], w['lru_gate_b'][j],
                                               w['lru_lambda'][j], w['lru_w_out'][j])
            new_conv.append(cnew)
            new_h.append(hnew)
        else:
            kc = None if prompt else k_cache[j]
            vc = None if prompt else v_cache[j]
            out, kn, vn = _attention_mixer(hn, kc, vc, w['att_w_qkv'][j], w['att_q_gain'][j],
                                           w['att_k_gain'][j], w['att_rel_bias'][j], w['att_w_o'][j], prompt)
            new_k.append(kn)
            new_v.append(vn)
        x = x + g1 * out
        x = _peer_layer(x, sh2, sc2, g2, w['norm_g'][i, 1][None, :], w['peer_wq'][i], w['peer_sk'][i],
                        w['peer_u'][i], w['peer_v'][i])
    return x, jnp.stack(new_conv), jnp.stack(new_h), jnp.stack(new_k), jnp.stack(new_v)


def kernel(x_prompt, x_sample, c_prompt, c_sample, state_conv, state_lru_h, cache_k, cache_v, norm_g, ada_w, ada_b, lru_w_in, lru_conv_w, lru_conv_b, lru_gate_w, lru_gate_b, lru_lambda, lru_w_out, att_w_qkv, att_q_gain, att_k_gain, att_rel_bias, att_w_o, peer_w_query, peer_sub_keys, peer_u, peer_v):
    depth = norm_g.shape[0]
    w = {
        'norm_g': norm_g, 'ada_w': ada_w, 'ada_b': ada_b,
        'lru_w_in': lru_w_in, 'lru_conv_w': lru_conv_w, 'lru_conv_b': lru_conv_b,
        'lru_gate_w': lru_gate_w, 'lru_gate_b': lru_gate_b, 'lru_lambda': lru_lambda, 'lru_w_out': lru_w_out,
        'att_w_qkv': att_w_qkv, 'att_q_gain': att_q_gain, 'att_k_gain': att_k_gain,
        'att_rel_bias': att_rel_bias, 'att_w_o': att_w_o,
        'peer_wq': peer_w_query.astype(BF16),
        'peer_sk': peer_sub_keys.astype(BF16).reshape(depth, 2 * PEER_HEADS, N_KEYS, PEER_HALF),
        'peer_u': peer_u, 'peer_v': peer_v,
    }
    y_prompt, p_conv, p_h, p_k, p_v = _trunk(x_prompt, c_prompt, None, None, None, None, w, True)
    y_sample, s_conv, s_h, s_k, s_v = _trunk(x_sample, c_sample, state_conv, state_lru_h,
                                              cache_k, cache_v, w, False)
    return (y_prompt, y_sample, p_conv, p_h, p_k, p_v, s_conv, s_h, s_k, s_v)
```

```python
import functools

import jax
import jax.numpy as jnp
from jax import lax
from jax.experimental import pallas as pl
from jax.experimental.pallas import tpu as pltpu
from jax.experimental.pallas import tpu_sc as plsc

D_MODEL = 1024
CHUNK = 64
LRU_WIDTH = D_MODEL
LRU_BLOCKS = 4
LRU_BLOCK_W = LRU_WIDTH // LRU_BLOCKS
CONV_W = 4
RG_C = 8.0
N_HEADS = 8
HEAD_DIM = D_MODEL // N_HEADS
LEFT_CHUNKS = 8
LEFT_ROWS = LEFT_CHUNKS * CHUNK
BAND = (LEFT_CHUNKS + 1) * CHUNK
REL_CLIP = 128
N_REL = 2 * REL_CLIP + 1
PEER_HEADS = 8
N_KEYS = 128
N_EXPERTS = N_KEYS * N_KEYS
PEER_TOPK = 16
PEER_QDIM = 256
PEER_HALF = PEER_QDIM // 2
PEER_SEL = PEER_HEADS * PEER_TOPK
EPS = 1e-6
NEG_INF = -1e30
PAST_LEN = 2048

LANES = 128
BF16 = jnp.bfloat16
F32 = jnp.float32


def _peer_route_kernel(x_ref, sh_ref, sc_ref, ng_ref, wqt_ref, sk_ref,
                       hn_ref, idx_ref, g_ref,
                       q_s, ts_s, ti_s, cs_s, ci_s):
    tt = x_ref.shape[1]
    ncand = PEER_TOPK * PEER_TOPK
    x = x_ref[0]
    ms = jnp.mean(x * x, axis=-1, keepdims=True)
    hn = x * lax.rsqrt(ms + EPS) * ng_ref[...]
    hn = hn * (1.0 + sc_ref[0]) + sh_ref[0]
    hn_ref[0] = hn
    qt = lax.dot_general(wqt_ref[...], hn.astype(BF16), (((1,), (1,)), ((), ())),
                         preferred_element_type=F32)
    for gidx in range(2 * PEER_HEADS):
        q_s[gidx] = qt[gidx * PEER_HALF:(gidx + 1) * PEER_HALF, :]

    key_id = lax.broadcasted_iota(jnp.int32, (N_KEYS, tt), 0)
    slot_id = lax.broadcasted_iota(jnp.int32, (PEER_TOPK, tt), 0)

    def list_body(gidx, _):
        s = jnp.dot(sk_ref[gidx], q_s[gidx].astype(BF16), preferred_element_type=F32)
        ts = jnp.zeros((PEER_TOPK, tt), F32)
        ti = jnp.zeros((PEER_TOPK, tt), jnp.int32)
        for j in range(PEER_TOPK):
            m = jnp.max(s, axis=0, keepdims=True)
            am = jnp.min(jnp.where(s == m, key_id, N_KEYS), axis=0, keepdims=True)
            ts = jnp.where(slot_id == j, m, ts)
            ti = jnp.where(slot_id == j, am, ti)
            s = jnp.where(key_id == am, -jnp.inf, s)
        ts_s[gidx] = ts
        ti_s[gidx] = ti
        return 0

    lax.fori_loop(0, 2 * PEER_HEADS, list_body, 0)

    cand = lax.broadcasted_iota(jnp.int32, (ncand, tt), 0)

    def head_body(h, _):
        ts1, ts2 = ts_s[2 * h], ts_s[2 * h + 1]
        ti1, ti2 = ti_s[2 * h], ti_s[2 * h + 1]
        for a in range(PEER_TOPK):
            cs_s[a * PEER_TOPK:(a + 1) * PEER_TOPK, :] = ts1[a:a + 1, :] + ts2
            ci_s[a * PEER_TOPK:(a + 1) * PEER_TOPK, :] = ti1[a:a + 1, :] * N_KEYS + ti2
        cs = cs_s[...]
        ci = ci_s[...]
        fs = jnp.zeros((PEER_TOPK, tt), F32)
        ei = jnp.zeros((PEER_TOPK, tt), jnp.int32)
        for k in range(PEER_TOPK):
            m = jnp.max(cs, axis=0, keepdims=True)
            cmin = jnp.min(jnp.where(cs == m, cand, ncand), axis=0, keepdims=True)
            sel = cand == cmin
            e = jnp.sum(jnp.where(sel, ci, 0), axis=0, keepdims=True)
            fs = jnp.where(slot_id == k, m, fs)
            ei = jnp.where(slot_id == k, e, ei)
            cs = jnp.where(sel, -jnp.inf, cs)
        ex = jnp.exp(fs - fs[0:1, :])
        g = ex / jnp.sum(ex, axis=0, keepdims=True)
        row0 = pl.multiple_of(h * PEER_TOPK, PEER_TOPK)
        g_ref[0, pl.ds(row0, PEER_TOPK), :] = g
        idx_ref[0, pl.ds(row0, PEER_TOPK), :] = ei
        return 0

    lax.fori_loop(0, PEER_HEADS, head_body, 0)


def _peer_route(x, sh, sc, ng, wqt, sk):
    B, T, D = x.shape
    tt = min(T, 128)
    grid = (B, T // tt)
    tok = lambda b, t: (b, t, 0)
    tok_t = lambda b, t: (b, 0, t)
    per_b = lambda b, t: (b, 0, 0)
    ncand = PEER_TOPK * PEER_TOPK
    hn, idx_t, g_t = pl.pallas_call(
        _peer_route_kernel,
        grid=grid,
        in_specs=[
            pl.BlockSpec((1, tt, D), tok),
            pl.BlockSpec((1, 1, D), per_b),
            pl.BlockSpec((1, 1, D), per_b),
            pl.BlockSpec((1, D), lambda b, t: (0, 0)),
            pl.BlockSpec(wqt.shape, lambda b, t: (0, 0)),
            pl.BlockSpec(sk.shape, lambda b, t: (0, 0, 0)),
        ],
        out_specs=[
            pl.BlockSpec((1, tt, D), tok),
            pl.BlockSpec((1, PEER_SEL, tt), tok_t),
            pl.BlockSpec((1, PEER_SEL, tt), tok_t),
        ],
        out_shape=[
            jax.ShapeDtypeStruct((B, T, D), F32),
            jax.ShapeDtypeStruct((B, PEER_SEL, T), jnp.int32),
            jax.ShapeDtypeStruct((B, PEER_SEL, T), F32),
        ],
        scratch_shapes=[
            pltpu.VMEM((2 * PEER_HEADS, PEER_HALF, tt), F32),
            pltpu.VMEM((2 * PEER_HEADS, PEER_TOPK, tt), F32),
            pltpu.VMEM((2 * PEER_HEADS, PEER_TOPK, tt), jnp.int32),
            pltpu.VMEM((ncand, tt), F32),
            pltpu.VMEM((ncand, tt), jnp.int32),
        ],
        compiler_params=pltpu.CompilerParams(
            dimension_semantics=("parallel", "parallel"), vmem_limit_bytes=48 * 1024 * 1024),
        name="peer_route",
    )(x, sh, sc, ng, wqt, sk)
    return hn, jnp.swapaxes(idx_t, 1, 2), jnp.swapaxes(g_t, 1, 2)


SC_CORES = 2
SC_SUBCORES = 16
SC_LANES = 16
SC_WORKERS = SC_CORES * SC_SUBCORES
NCHUNK = D_MODEL // SC_LANES
TOK_BLOCK = 8


def _peer_expert_sc(x, hn, eidx, g, g2, u_tab, v_tab, tokens_per_seq):
    N, D = x.shape
    tpw = N // SC_WORKERS
    assert tpw * SC_WORKERS == N and tpw % TOK_BLOCK == 0
    assert tokens_per_seq % tpw == 0
    workers_per_seq = tokens_per_seq // tpw
    nblocks = tpw // TOK_BLOCK
    steps = TOK_BLOCK * PEER_HEADS
    mesh = plsc.VectorSubcoreMesh(core_axis_name="c", subcore_axis_name="s",
                                  num_cores=SC_CORES, num_subcores=SC_SUBCORES)

    @functools.partial(
        pl.kernel, out_type=jax.ShapeDtypeStruct((N, D), F32), mesh=mesh,
        scratch_types=[
            pltpu.VMEM((TOK_BLOCK, D), F32),
            pltpu.VMEM((TOK_BLOCK, D), F32),
            pltpu.VMEM((TOK_BLOCK, PEER_SEL), jnp.int32),
            pltpu.VMEM((TOK_BLOCK, PEER_SEL), F32),
            pltpu.VMEM((D,), F32),
            pltpu.VMEM((2, PEER_TOPK, D), F32),
            pltpu.VMEM((2, PEER_TOPK, D), F32),
            pltpu.VMEM((D,), F32),
            pltpu.SemaphoreType.DMA((2,)),
            pltpu.SemaphoreType.DMA((2,)),
        ],
        compiler_params=pltpu.CompilerParams(needs_layout_passes=False),
        name="peer_expert_sc",
    )
    def k(x_hbm, hn_hbm, idx_hbm, g_hbm, g2_hbm, u_hbm, v_hbm, out_hbm,
          hn_b, x_b, idx_b, g_b, g2_v, ubuf, vbuf, o_v, usem, vsem):
        wid = lax.axis_index("s") * SC_CORES + lax.axis_index("c")
        tok0 = wid * tpw
        pltpu.sync_copy(g2_hbm.at[wid // workers_per_seq], g2_v)
        lane = lax.iota(jnp.int32, SC_LANES)

        def gather_descs(s, slot):
            t = s // PEER_HEADS
            h0 = pl.multiple_of((s % PEER_HEADS) * PEER_TOPK, PEER_TOPK)
            ids = idx_b.at[t, pl.ds(h0, PEER_TOPK)]
            return (pltpu.make_async_copy(u_hbm.at[ids], ubuf.at[slot], usem.at[slot]),
                    pltpu.make_async_copy(v_hbm.at[ids], vbuf.at[slot], vsem.at[slot]))

        def issue(s, slot):
            du, dv = gather_descs(s, slot)
            du.start()
            dv.start()

        def wait(s, slot):
            du, dv = gather_descs(s, slot)
            du.wait()
            dv.wait()

        def compute(s, slot):
            t = s // PEER_HEADS
            h = s % PEER_HEADS
            h0 = pl.multiple_of(h * PEER_TOPK, PEER_TOPK)

            @pl.when(h == 0)
            def _():
                @pl.loop(0, NCHUNK)
                def _(c):
                    o_v[pl.ds(pl.multiple_of(c * SC_LANES, SC_LANES), SC_LANES)] = jnp.zeros((SC_LANES,), F32)

            def ubody(c, accs):
                c0 = pl.multiple_of(c * SC_LANES, SC_LANES)
                xv = hn_b[t, pl.ds(c0, SC_LANES)]
                return tuple(accs[kk] + xv * ubuf[slot, kk, pl.ds(c0, SC_LANES)] for kk in range(PEER_TOPK))

            accs = lax.fori_loop(0, NCHUNK, ubody,
                                 tuple(jnp.zeros((SC_LANES,), F32) for _ in range(PEER_TOPK)))
            last = jnp.full((SC_LANES,), SC_LANES - 1, jnp.int32)
            r = jnp.zeros((SC_LANES,), F32)
            for kk in range(PEER_TOPK):
                tot = jnp.take_along_axis(plsc.cumsum(accs[kk]), last, axis=0)
                r = jnp.where(lane == kk, tot, r)
            gv = g_b[t, pl.ds(h0, PEER_TOPK)]
            z = 0.7978845608028654 * (r + 0.044715 * (r * r * r))
            w = gv * r / (1.0 + jnp.exp(-2.0 * z))
            wk = [jnp.take_along_axis(w, jnp.full((SC_LANES,), kk, jnp.int32), axis=0) for kk in range(PEER_TOPK)]

            @pl.loop(0, NCHUNK)
            def _(c):
                c0 = pl.multiple_of(c * SC_LANES, SC_LANES)
                acc = wk[0] * vbuf[slot, 0, pl.ds(c0, SC_LANES)]
                for kk in range(1, PEER_TOPK):
                    acc = acc + wk[kk] * vbuf[slot, kk, pl.ds(c0, SC_LANES)]
                plsc.addupdate(o_v.at[pl.ds(c0, SC_LANES)], acc)

            @pl.when(h == PEER_HEADS - 1)
            def _():
                @pl.loop(0, NCHUNK)
                def _(c):
                    c0 = pl.multiple_of(c * SC_LANES, SC_LANES)
                    x_b[t, pl.ds(c0, SC_LANES)] = (x_b[t, pl.ds(c0, SC_LANES)]
                                                   + g2_v[pl.ds(c0, SC_LANES)] * o_v[pl.ds(c0, SC_LANES)])

        @pl.loop(0, nblocks)
        def _(blk):
            tok = pl.multiple_of(tok0 + blk * TOK_BLOCK, TOK_BLOCK)
            pltpu.sync_copy(idx_hbm.at[pl.ds(tok, TOK_BLOCK)], idx_b)
            pltpu.sync_copy(g_hbm.at[pl.ds(tok, TOK_BLOCK)], g_b)
            pltpu.sync_copy(hn_hbm.at[pl.ds(tok, TOK_BLOCK)], hn_b)
            pltpu.sync_copy(x_hbm.at[pl.ds(tok, TOK_BLOCK)], x_b)
            issue(0, 0)

            @pl.loop(0, steps, step=2)
            def _(s):
                issue(s + 1, 1)
                wait(s, 0)
                compute(s, 0)

                @pl.when(s + 2 < steps)
                def _():
                    issue(s + 2, 0)

                wait(s + 1, 1)
                compute(s + 1, 1)

            pltpu.sync_copy(x_b, out_hbm.at[pl.ds(tok, TOK_BLOCK)])

    return k(x, hn, eidx, g, g2, u_tab, v_tab)


def _peer_layer(x, sh2, sc2, g2, ng, wqt, sk, u_tab, v_tab):
    B, T, D = x.shape
    hn, eidx, g = _peer_route(x, sh2, sc2, ng, wqt, sk)
    y = _peer_expert_sc(x.reshape(B * T, D), hn.reshape(B * T, D), eidx.reshape(B * T, PEER_SEL),
                        g.reshape(B * T, PEER_SEL), g2.reshape(B, D), u_tab, v_tab, tokens_per_seq=T)
    return y.reshape(B, T, D)


def _rms_norm(x, g):
    xf = x.astype(jnp.float32)
    y = xf * lax.rsqrt(jnp.mean(xf * xf, axis=-1, keepdims=True) + EPS)
    return (y * g.astype(jnp.float32)).astype(x.dtype)


def _ada(c, w, b):
    m = jnp.einsum('bd,de->be', jax.nn.silu(c), w) + b
    return jnp.split(m[:, None, :], 6, axis=-1)


def _causal_conv(u, conv_state, w, b):
    T = u.shape[1]
    full = jnp.concatenate([conv_state.astype(u.dtype), u], axis=1)
    out = b + sum(full[:, k:k + T] * w[k] for k in range(CONV_W))
    return out, full[:, -(CONV_W - 1):]


def _rglru(u, h0, gate_w, gate_b, lam):
    B, T, W = u.shape
    ub = u.reshape(B, T, LRU_BLOCKS, LRU_BLOCK_W)
    gr = jnp.einsum('btnk,nkj->btnj', ub, gate_w[0]).reshape(B, T, W) + gate_b[0]
    gi = jnp.einsum('btnk,nkj->btnj', ub, gate_w[1]).reshape(B, T, W) + gate_b[1]
    r = jax.nn.sigmoid(gr.astype(jnp.float32))
    i = jax.nn.sigmoid(gi.astype(jnp.float32))
    log_a = -RG_C * r * jax.nn.softplus(-lam.astype(jnp.float32))
    a = jnp.exp(log_a)
    mult = jnp.sqrt(-jnp.expm1(2.0 * log_a))
    bterm = mult * i * u.astype(jnp.float32)
    bterm = bterm.at[:, 0].add(a[:, 0] * h0.astype(jnp.float32))

    def comb(left, right):
        al, bl = left
        ar, br = right
        return al * ar, ar * bl + br

    _, h = lax.associative_scan(comb, (a, bterm), axis=1)
    return h.astype(u.dtype), h[:, -1]


def _recurrent_mixer(h, conv_state, h0, w_in, conv_w, conv_b, gate_w, gate_b, lam, w_out):
    proj = jnp.einsum('btd,de->bte', h, w_in)
    gate_branch, rec_branch = jnp.split(proj, 2, axis=-1)
    gate_branch = jax.nn.gelu(gate_branch)
    u, new_conv = _causal_conv(rec_branch, conv_state, conv_w, conv_b)
    y, h_last = _rglru(u, h0, gate_w, gate_b, lam)
    out = jnp.einsum('btw,wd->btd', y * gate_branch, w_out)
    return out, new_conv, h_last


def _attend(q, k, v, qpos, kpos, kvalid, rel_bias):
    s = jnp.einsum('bqhd,bkhd->bhqk', q, k).astype(jnp.float32) * (HEAD_DIM ** -0.5)
    rel = jnp.clip(qpos[:, None] - kpos[None, :], -REL_CLIP, REL_CLIP) + REL_CLIP
    s = s + rel_bias[:, rel].astype(jnp.float32)[None]
    s = jnp.where(kvalid[None, None, None, :], s, NEG_INF)
    p = jax.nn.softmax(s, axis=-1).astype(v.dtype)
    return jnp.einsum('bhqk,bkhd->bqhd', p, v)


def _chunk_band_attention(q, k, v, rel_bias):
    B, T, H, Dh = q.shape
    nc = T // CHUNK
    kp = jnp.pad(k, ((0, 0), (LEFT_ROWS, 0), (0, 0), (0, 0)))
    vp = jnp.pad(v, ((0, 0), (LEFT_ROWS, 0), (0, 0), (0, 0)))

    def one_chunk(n):
        start = n * CHUNK
        q_n = lax.dynamic_slice_in_dim(q, start, CHUNK, axis=1)
        k_n = lax.dynamic_slice_in_dim(kp, start, BAND, axis=1)
        v_n = lax.dynamic_slice_in_dim(vp, start, BAND, axis=1)
        qpos = start + jnp.arange(CHUNK)
        kpos = start - LEFT_ROWS + jnp.arange(BAND)
        return _attend(q_n, k_n, v_n, qpos, kpos, kpos >= 0, rel_bias)

    outs = lax.map(one_chunk, jnp.arange(nc))
    return jnp.moveaxis(outs, 0, 1).reshape(B, T, H, Dh)


def _attention_mixer(h, k_cache, v_cache, w_qkv, q_gain, k_gain, rel_bias, w_o, prompt):
    B, T, _ = h.shape
    qkv = jnp.einsum('btd,de->bte', h, w_qkv).reshape(B, T, 3, N_HEADS, HEAD_DIM)
    q = _rms_norm(qkv[:, :, 0], q_gain)
    k = _rms_norm(qkv[:, :, 1], k_gain)
    v = qkv[:, :, 2]
    if prompt:
        o = _chunk_band_attention(q, k, v, rel_bias)
        keep = min(LEFT_ROWS, T)
        new_k, new_v = k[:, T - keep:], v[:, T - keep:]
    else:
        lc = k_cache.shape[1]
        qpos = PAST_LEN + jnp.arange(T)
        kpos = jnp.concatenate([PAST_LEN - lc + jnp.arange(lc), qpos])
        k_all = jnp.concatenate([k_cache.astype(k.dtype), k], axis=1)
        v_all = jnp.concatenate([v_cache.astype(v.dtype), v], axis=1)
        o = _attend(q, k_all, v_all, qpos, kpos, jnp.ones((lc + T,), bool), rel_bias)
        new_k, new_v = k, v
    out = jnp.einsum('bte,ed->btd', o.reshape(B, T, D_MODEL), w_o)
    return out, new_k, new_v


def _trunk(x, c, conv_st, lru_st, k_cache, v_cache, w, prompt):
    B = x.shape[0]
    new_conv, new_h, new_k, new_v = [], [], [], []
    depth = w['norm_g'].shape[0]
    for i in range(depth):
        sh1, sc1, g1, sh2, sc2, g2 = _ada(c, w['ada_w'][i], w['ada_b'][i])
        hn = _rms_norm(x, w['norm_g'][i, 0]) * (1 + sc1) + sh1
        j = i // 2
        if i % 2 == 0:
            if prompt:
                cs0 = jnp.zeros((B, CONV_W - 1, LRU_WIDTH), x.dtype)
                h0 = jnp.zeros((B, LRU_WIDTH), jnp.float32)
            else:
                cs0, h0 = conv_st[j], lru_st[j]
            out, cnew, hnew = _recurrent_mixer(hn, cs0, h0, w['lru_w_in'][j], w['lru_conv_w'][j],
                                               w['lru_conv_b'][j], w['lru_gate_w'][j], w['lru_gate_b'][j],
                                               w['lru_lambda'][j], w['lru_w_out'][j])
            new_conv.append(cnew)
            new_h.append(hnew)
        else:
            kc = None if prompt else k_cache[j]
            vc = None if prompt else v_cache[j]
            out, kn, vn = _attention_mixer(hn, kc, vc, w['att_w_qkv'][j], w['att_q_gain'][j],
                                           w['att_k_gain'][j], w['att_rel_bias'][j], w['att_w_o'][j], prompt)
            new_k.append(kn)
            new_v.append(vn)
        x = x + g1 * out
        x = _peer_layer(x, sh2, sc2, g2, w['norm_g'][i, 1][None, :], w['peer_wqt'][i], w['peer_sk'][i],
                        w['peer_u'][i], w['peer_v'][i])
    return x, jnp.stack(new_conv), jnp.stack(new_h), jnp.stack(new_k), jnp.stack(new_v)


def kernel(x_prompt, x_sample, c_prompt, c_sample, state_conv, state_lru_h, cache_k, cache_v, norm_g, ada_w, ada_b, lru_w_in, lru_conv_w, lru_conv_b, lru_gate_w, lru_gate_b, lru_lambda, lru_w_out, att_w_qkv, att_q_gain, att_k_gain, att_rel_bias, att_w_o, peer_w_query, peer_sub_keys, peer_u, peer_v):
    depth = norm_g.shape[0]
    w = {
        'norm_g': norm_g, 'ada_w': ada_w, 'ada_b': ada_b,
        'lru_w_in': lru_w_in, 'lru_conv_w': lru_conv_w, 'lru_conv_b': lru_conv_b,
        'lru_gate_w': lru_gate_w, 'lru_gate_b': lru_gate_b, 'lru_lambda': lru_lambda, 'lru_w_out': lru_w_out,
        'att_w_qkv': att_w_qkv, 'att_q_gain': att_q_gain, 'att_k_gain': att_k_gain,
        'att_rel_bias': att_rel_bias, 'att_w_o': att_w_o,
        'peer_wqt': jnp.swapaxes(peer_w_query, 1, 2).astype(BF16),
        'peer_sk': peer_sub_keys.astype(BF16).reshape(depth, 2 * PEER_HEADS, N_KEYS, PEER_HALF),
        'peer_u': peer_u, 'peer_v': peer_v,
    }
    y_prompt, p_conv, p_h, p_k, p_v = _trunk(x_prompt, c_prompt, None, None, None, None, w, True)
    y_sample, s_conv, s_h, s_k, s_v = _trunk(x_sample, c_sample, state_conv, state_lru_h,
                                              cache_k, cache_v, w, False)
    return (y_prompt, y_sample, p_conv, p_h, p_k, p_v, s_conv, s_h, s_k, s_v)
```

```python
import functools

import jax
import jax.numpy as jnp
from jax import lax
from jax.experimental import pallas as pl
from jax.experimental.pallas import tpu as pltpu
from jax.experimental.pallas import tpu_sc as plsc

D_MODEL = 1024
CHUNK = 64
LRU_WIDTH = D_MODEL
LRU_BLOCKS = 4
LRU_BLOCK_W = LRU_WIDTH // LRU_BLOCKS
CONV_W = 4
RG_C = 8.0
N_HEADS = 8
HEAD_DIM = D_MODEL // N_HEADS
LEFT_CHUNKS = 8
LEFT_ROWS = LEFT_CHUNKS * CHUNK
BAND = (LEFT_CHUNKS + 1) * CHUNK
REL_CLIP = 128
N_REL = 2 * REL_CLIP + 1
PEER_HEADS = 8
N_KEYS = 128
N_EXPERTS = N_KEYS * N_KEYS
PEER_TOPK = 16
PEER_QDIM = 256
PEER_HALF = PEER_QDIM // 2
PEER_SEL = PEER_HEADS * PEER_TOPK
EPS = 1e-6
NEG_INF = -1e30
PAST_LEN = 2048

LANES = 128
BF16 = jnp.bfloat16
F32 = jnp.float32


def _peer_route_kernel(x_ref, sh_ref, sc_ref, ng_ref, wqt_ref, sk_ref,
                       hn_ref, idx_ref, g_ref,
                       q_s, ts_s, ti_s, cs_s, ci_s):
    tt = x_ref.shape[1]
    ncand = PEER_TOPK * PEER_TOPK
    x = x_ref[0]
    ms = jnp.mean(x * x, axis=-1, keepdims=True)
    hn = x * lax.rsqrt(ms + EPS) * ng_ref[...]
    hn = hn * (1.0 + sc_ref[0]) + sh_ref[0]
    hn_ref[0] = hn
    qt = lax.dot_general(wqt_ref[...], hn.astype(BF16), (((1,), (1,)), ((), ())),
                         preferred_element_type=F32)
    for gidx in range(2 * PEER_HEADS):
        q_s[gidx] = qt[gidx * PEER_HALF:(gidx + 1) * PEER_HALF, :]

    key_id = lax.broadcasted_iota(jnp.int32, (N_KEYS, tt), 0)
    slot_id = lax.broadcasted_iota(jnp.int32, (PEER_TOPK, tt), 0)

    def list_body(gidx, _):
        s = jnp.dot(sk_ref[gidx], q_s[gidx].astype(BF16), preferred_element_type=F32)
        ts = jnp.zeros((PEER_TOPK, tt), F32)
        ti = jnp.zeros((PEER_TOPK, tt), jnp.int32)
        for j in range(PEER_TOPK):
            m = jnp.max(s, axis=0, keepdims=True)
            am = jnp.min(jnp.where(s == m, key_id, N_KEYS), axis=0, keepdims=True)
            ts = jnp.where(slot_id == j, m, ts)
            ti = jnp.where(slot_id == j, am, ti)
            s = jnp.where(key_id == am, -jnp.inf, s)
        ts_s[gidx] = ts
        ti_s[gidx] = ti
        return 0

    lax.fori_loop(0, 2 * PEER_HEADS, list_body, 0)

    cand = lax.broadcasted_iota(jnp.int32, (ncand, tt), 0)

    def head_body(h, _):
        ts1, ts2 = ts_s[2 * h], ts_s[2 * h + 1]
        ti1, ti2 = ti_s[2 * h], ti_s[2 * h + 1]
        for a in range(PEER_TOPK):
            cs_s[a * PEER_TOPK:(a + 1) * PEER_TOPK, :] = ts1[a:a + 1, :] + ts2
            ci_s[a * PEER_TOPK:(a + 1) * PEER_TOPK, :] = ti1[a:a + 1, :] * N_KEYS + ti2
        cs = cs_s[...]
        ci = ci_s[...]
        fs = jnp.zeros((PEER_TOPK, tt), F32)
        ei = jnp.zeros((PEER_TOPK, tt), jnp.int32)
        for k in range(PEER_TOPK):
            m = jnp.max(cs, axis=0, keepdims=True)
            cmin = jnp.min(jnp.where(cs == m, cand, ncand), axis=0, keepdims=True)
            sel = cand == cmin
            e = jnp.sum(jnp.where(sel, ci, 0), axis=0, keepdims=True)
            fs = jnp.where(slot_id == k, m, fs)
            ei = jnp.where(slot_id == k, e, ei)
            cs = jnp.where(sel, -jnp.inf, cs)
        ex = jnp.exp(fs - fs[0:1, :])
        g = ex / jnp.sum(ex, axis=0, keepdims=True)
        row0 = pl.multiple_of(h * PEER_TOPK, PEER_TOPK)
        g_ref[0, pl.ds(row0, PEER_TOPK), :] = g
        idx_ref[0, pl.ds(row0, PEER_TOPK), :] = ei
        return 0

    lax.fori_loop(0, PEER_HEADS, head_body, 0)


def _peer_route(x, sh, sc, ng, wqt, sk):
    B, T, D = x.shape
    tt = min(T, 128)
    grid = (B, T // tt)
    tok = lambda b, t: (b, t, 0)
    tok_t = lambda b, t: (b, 0, t)
    per_b = lambda b, t: (b, 0, 0)
    ncand = PEER_TOPK * PEER_TOPK
    hn, idx_t, g_t = pl.pallas_call(
        _peer_route_kernel,
        grid=grid,
        in_specs=[
            pl.BlockSpec((1, tt, D), tok),
            pl.BlockSpec((1, 1, D), per_b),
            pl.BlockSpec((1, 1, D), per_b),
            pl.BlockSpec((1, D), lambda b, t: (0, 0)),
            pl.BlockSpec(wqt.shape, lambda b, t: (0, 0)),
            pl.BlockSpec(sk.shape, lambda b, t: (0, 0, 0)),
        ],
        out_specs=[
            pl.BlockSpec((1, tt, D), tok),
            pl.BlockSpec((1, PEER_SEL, tt), tok_t),
            pl.BlockSpec((1, PEER_SEL, tt), tok_t),
        ],
        out_shape=[
            jax.ShapeDtypeStruct((B, T, D), F32),
            jax.ShapeDtypeStruct((B, PEER_SEL, T), jnp.int32),
            jax.ShapeDtypeStruct((B, PEER_SEL, T), F32),
        ],
        scratch_shapes=[
            pltpu.VMEM((2 * PEER_HEADS, PEER_HALF, tt), F32),
            pltpu.VMEM((2 * PEER_HEADS, PEER_TOPK, tt), F32),
            pltpu.VMEM((2 * PEER_HEADS, PEER_TOPK, tt), jnp.int32),
            pltpu.VMEM((ncand, tt), F32),
            pltpu.VMEM((ncand, tt), jnp.int32),
        ],
        compiler_params=pltpu.CompilerParams(
            dimension_semantics=("parallel", "parallel"), vmem_limit_bytes=48 * 1024 * 1024),
        name="peer_route",
    )(x, sh, sc, ng, wqt, sk)
    return hn, jnp.swapaxes(idx_t, 1, 2), jnp.swapaxes(g_t, 1, 2)


SC_CORES = 2
SC_SUBCORES = 16
SC_LANES = 16
SC_WORKERS = SC_CORES * SC_SUBCORES
NCHUNK = D_MODEL // SC_LANES
TOK_BLOCK = 8


def _peer_expert_sc(x, hn, eidx, g, g2, u_tab, v_tab, tokens_per_seq):
    N, D = x.shape
    tpw = N // SC_WORKERS
    assert tpw * SC_WORKERS == N and tpw % TOK_BLOCK == 0
    assert tokens_per_seq % tpw == 0
    workers_per_seq = tokens_per_seq // tpw
    nblocks = tpw // TOK_BLOCK
    steps = TOK_BLOCK * PEER_HEADS
    mesh = plsc.VectorSubcoreMesh(core_axis_name="c", subcore_axis_name="s",
                                  num_cores=SC_CORES, num_subcores=SC_SUBCORES)

    @functools.partial(
        pl.kernel, out_type=jax.ShapeDtypeStruct((N, D), F32), mesh=mesh,
        scratch_types=[
            pltpu.VMEM((TOK_BLOCK, D), F32),
            pltpu.VMEM((TOK_BLOCK, D), F32),
            pltpu.VMEM((TOK_BLOCK, PEER_SEL), jnp.int32),
            pltpu.VMEM((TOK_BLOCK, PEER_SEL), F32),
            pltpu.VMEM((D,), F32),
            pltpu.VMEM((2, PEER_TOPK, D), F32),
            pltpu.VMEM((2, PEER_TOPK, D), F32),
            pltpu.VMEM((D,), F32),
            pltpu.SemaphoreType.DMA((2,)),
            pltpu.SemaphoreType.DMA((2,)),
        ],
        compiler_params=pltpu.CompilerParams(needs_layout_passes=False),
        name="peer_expert_sc",
    )
    def k(x_hbm, hn_hbm, idx_hbm, g_hbm, g2_hbm, u_hbm, v_hbm, out_hbm,
          hn_b, x_b, idx_b, g_b, g2_v, ubuf, vbuf, o_v, usem, vsem):
        wid = lax.axis_index("s") * SC_CORES + lax.axis_index("c")
        tok0 = wid * tpw
        pltpu.sync_copy(g2_hbm.at[wid // workers_per_seq], g2_v)
        lane = lax.iota(jnp.int32, SC_LANES)

        def gather_descs(s, slot):
            t = s // PEER_HEADS
            h0 = pl.multiple_of((s % PEER_HEADS) * PEER_TOPK, PEER_TOPK)
            ids = idx_b.at[t, pl.ds(h0, PEER_TOPK)]
            return (pltpu.make_async_copy(u_hbm.at[ids], ubuf.at[slot], usem.at[slot]),
                    pltpu.make_async_copy(v_hbm.at[ids], vbuf.at[slot], vsem.at[slot]))

        def issue(s, slot):
            du, dv = gather_descs(s, slot)
            du.start()
            dv.start()

        def wait(s, slot):
            du, dv = gather_descs(s, slot)
            du.wait()
            dv.wait()

        def compute(s, slot):
            t = s // PEER_HEADS
            h = s % PEER_HEADS
            h0 = pl.multiple_of(h * PEER_TOPK, PEER_TOPK)

            @pl.when(h == 0)
            def _():
                @pl.loop(0, NCHUNK)
                def _(c):
                    o_v[pl.ds(pl.multiple_of(c * SC_LANES, SC_LANES), SC_LANES)] = jnp.zeros((SC_LANES,), F32)

            def ubody(c, accs):
                c0 = pl.multiple_of(c * SC_LANES, SC_LANES)
                xv = hn_b[t, pl.ds(c0, SC_LANES)]
                return tuple(accs[kk] + xv * ubuf[slot, kk, pl.ds(c0, SC_LANES)] for kk in range(PEER_TOPK))

            accs = lax.fori_loop(0, NCHUNK, ubody,
                                 tuple(jnp.zeros((SC_LANES,), F32) for _ in range(PEER_TOPK)))
            last = jnp.full((SC_LANES,), SC_LANES - 1, jnp.int32)
            r = jnp.zeros((SC_LANES,), F32)
            for kk in range(PEER_TOPK):
                tot = jnp.take_along_axis(plsc.cumsum(accs[kk]), last, axis=0)
                r = jnp.where(lane == kk, tot, r)
            gv = g_b[t, pl.ds(h0, PEER_TOPK)]
            z = 0.7978845608028654 * (r + 0.044715 * (r * r * r))
            w = gv * r / (1.0 + jnp.exp(-2.0 * z))
            wk = [jnp.take_along_axis(w, jnp.full((SC_LANES,), kk, jnp.int32), axis=0) for kk in range(PEER_TOPK)]

            @pl.loop(0, NCHUNK)
            def _(c):
                c0 = pl.multiple_of(c * SC_LANES, SC_LANES)
                acc = wk[0] * vbuf[slot, 0, pl.ds(c0, SC_LANES)]
                for kk in range(1, PEER_TOPK):
                    acc = acc + wk[kk] * vbuf[slot, kk, pl.ds(c0, SC_LANES)]
                plsc.addupdate(o_v.at[pl.ds(c0, SC_LANES)], acc)

            @pl.when(h == PEER_HEADS - 1)
            def _():
                @pl.loop(0, NCHUNK)
                def _(c):
                    c0 = pl.multiple_of(c * SC_LANES, SC_LANES)
                    x_b[t, pl.ds(c0, SC_LANES)] = (x_b[t, pl.ds(c0, SC_LANES)]
                                                   + g2_v[pl.ds(c0, SC_LANES)] * o_v[pl.ds(c0, SC_LANES)])

        @pl.loop(0, nblocks)
        def _(blk):
            tok = pl.multiple_of(tok0 + blk * TOK_BLOCK, TOK_BLOCK)
            pltpu.sync_copy(idx_hbm.at[pl.ds(tok, TOK_BLOCK)], idx_b)
            pltpu.sync_copy(g_hbm.at[pl.ds(tok, TOK_BLOCK)], g_b)
            pltpu.sync_copy(hn_hbm.at[pl.ds(tok, TOK_BLOCK)], hn_b)
            pltpu.sync_copy(x_hbm.at[pl.ds(tok, TOK_BLOCK)], x_b)
            issue(0, 0)

            @pl.loop(0, steps, step=2)
            def _(s):
                issue(s + 1, 1)
                wait(s, 0)
                compute(s, 0)

                @pl.when(s + 2 < steps)
                def _():
                    issue(s + 2, 0)

                wait(s + 1, 1)
                compute(s + 1, 1)

            pltpu.sync_copy(x_b, out_hbm.at[pl.ds(tok, TOK_BLOCK)])

    return k(x, hn, eidx, g, g2, u_tab, v_tab)


def _peer_layer(x, sh2, sc2, g2, ng, wqt, sk, u_tab, v_tab):
    B, T, D = x.shape
    hn, eidx, g = _peer_route(x, sh2, sc2, ng, wqt, sk)
    y = _peer_expert_sc(x.reshape(B * T, D), hn.reshape(B * T, D), eidx.reshape(B * T, PEER_SEL),
                        g.reshape(B * T, PEER_SEL), g2.reshape(B, D), u_tab, v_tab, tokens_per_seq=T)
    return y.reshape(B, T, D)


def _attn_layer_kernel(*refs, tq, ch, has_cache):
    if has_cache:
        (x_ref, sh_ref, sc_ref, g1_ref, ng_ref, wqkv_ref, qg_ref, kg_ref, bias_ref, wo_ref, kc_ref, vc_ref,
         y_ref, ko_ref, vo_ref, kcat, vcat, qn_s, o_s) = refs
    else:
        (x_ref, sh_ref, sc_ref, g1_ref, ng_ref, wqkv_ref, qg_ref, kg_ref, bias_ref, wo_ref,
         y_ref, ko_ref, vo_ref, kcat, vcat, qn_s, o_s) = refs
    t = pl.program_id(1)
    band = LEFT_ROWS + ch

    if has_cache:
        kcat[0:LEFT_ROWS, :] = kc_ref[0].astype(BF16)
        vcat[0:LEFT_ROWS, :] = vc_ref[0].astype(BF16)
    else:
        @pl.when(t == 0)
        def _():
            kcat[0:LEFT_ROWS, :] = jnp.zeros((LEFT_ROWS, D_MODEL), BF16)
            vcat[0:LEFT_ROWS, :] = jnp.zeros((LEFT_ROWS, D_MODEL), BF16)

    x = x_ref[0]
    ms = jnp.mean(x * x, axis=-1, keepdims=True)
    hn = x * lax.rsqrt(ms + EPS) * ng_ref[...]
    hn = hn * (1.0 + sc_ref[0]) + sh_ref[0]
    qkv = jnp.dot(hn.astype(BF16), wqkv_ref[...], preferred_element_type=F32)
    for h in range(N_HEADS):
        lo = h * HEAD_DIM
        qh = qkv[:, lo:lo + HEAD_DIM]
        kh = qkv[:, D_MODEL + lo:D_MODEL + lo + HEAD_DIM]
        qh = qh * lax.rsqrt(jnp.mean(qh * qh, axis=-1, keepdims=True) + EPS) * qg_ref[...]
        kh = kh * lax.rsqrt(jnp.mean(kh * kh, axis=-1, keepdims=True) + EPS) * kg_ref[...]
        qn_s[:, lo:lo + HEAD_DIM] = qh.astype(BF16)
        ko_ref[0, :, lo:lo + HEAD_DIM] = kh
        kcat[LEFT_ROWS:LEFT_ROWS + tq, lo:lo + HEAD_DIM] = kh.astype(BF16)
    v = qkv[:, 2 * D_MODEL:]
    vo_ref[0] = v
    vcat[LEFT_ROWS:LEFT_ROWS + tq, :] = v.astype(BF16)

    scale = HEAD_DIM ** -0.5
    kpos_rel = lax.broadcasted_iota(jnp.int32, (ch, band), 1)

    def chunk_body(c, _):
        r0 = pl.multiple_of(c * ch, ch)
        if not has_cache:
            valid = kpos_rel + (t * tq + r0) >= LEFT_ROWS
        for h in range(N_HEADS):
            lo = h * HEAD_DIM
            q = qn_s[pl.ds(r0, ch), lo:lo + HEAD_DIM]
            kb = kcat[pl.ds(r0, band), lo:lo + HEAD_DIM]
            vb = vcat[pl.ds(r0, band), lo:lo + HEAD_DIM]
            s = lax.dot_general(q, kb, (((1,), (1,)), ((), ())), preferred_element_type=F32) * scale
            s = s + bias_ref[h, 0:ch, 0:band]
            if not has_cache:
                s = jnp.where(valid, s, NEG_INF)
            m = jnp.max(s, axis=-1, keepdims=True)
            e = jnp.exp(s - m)
            p = e / jnp.sum(e, axis=-1, keepdims=True)
            o_s[pl.ds(r0, ch), lo:lo + HEAD_DIM] = jnp.dot(p.astype(BF16), vb, preferred_element_type=F32).astype(BF16)
        return 0

    lax.fori_loop(0, tq // ch, chunk_body, 0)

    out = jnp.dot(o_s[...], wo_ref[...], preferred_element_type=F32)
    y_ref[0] = x + g1_ref[0] * out

    if not has_cache:
        kcat[0:LEFT_ROWS, :] = kcat[tq:tq + LEFT_ROWS, :]
        vcat[0:LEFT_ROWS, :] = vcat[tq:tq + LEFT_ROWS, :]


def _rel_bias_tile(rel_bias):
    i = jnp.arange(CHUNK)[:, None]
    r = jnp.arange(LEFT_ROWS + CHUNK)[None, :]
    rel = jnp.clip(i + LEFT_ROWS - r, -REL_CLIP, REL_CLIP) + REL_CLIP
    return rel_bias[:, rel]


def _attn_layer(x, sh, sc, g1, ng, wqkv, qg, kg, bias, wo, k_cache=None, v_cache=None):
    B, T, D = x.shape
    has_cache = k_cache is not None
    if has_cache:
        tq, ch = T, T
        keep_blocks = 1
    else:
        tq, ch = 256, CHUNK
        keep_blocks = LEFT_ROWS // tq
    nt = T // tq
    assert nt >= keep_blocks
    tok = lambda b, t: (b, t, 0)
    per_b = lambda b, t: (b, 0, 0)
    c2 = lambda b, t: (0, 0)
    c3 = lambda b, t: (0, 0, 0)
    keep = lambda b, t: (b, jnp.maximum(t - (nt - keep_blocks), 0), 0)
    in_specs = [
        pl.BlockSpec((1, tq, D), tok),
        pl.BlockSpec((1, 1, D), per_b), pl.BlockSpec((1, 1, D), per_b), pl.BlockSpec((1, 1, D), per_b),
        pl.BlockSpec((1, D), c2),
        pl.BlockSpec(wqkv.shape, c2),
        pl.BlockSpec((1, HEAD_DIM), c2), pl.BlockSpec((1, HEAD_DIM), c2),
        pl.BlockSpec(bias.shape, c3),
        pl.BlockSpec(wo.shape, c2),
    ]
    args = [x, sh, sc, g1, ng, wqkv, qg, kg, bias, wo]
    if has_cache:
        in_specs += [pl.BlockSpec((1, LEFT_ROWS, D), per_b), pl.BlockSpec((1, LEFT_ROWS, D), per_b)]
        args += [k_cache, v_cache]
    return pl.pallas_call(
        functools.partial(_attn_layer_kernel, tq=tq, ch=ch, has_cache=has_cache),
        grid=(B, nt),
        in_specs=in_specs,
        out_specs=[pl.BlockSpec((1, tq, D), tok), pl.BlockSpec((1, tq, D), keep), pl.BlockSpec((1, tq, D), keep)],
        out_shape=[jax.ShapeDtypeStruct((B, T, D), F32),
                   jax.ShapeDtypeStruct((B, keep_blocks * tq, D), F32),
                   jax.ShapeDtypeStruct((B, keep_blocks * tq, D), F32)],
        scratch_shapes=[
            pltpu.VMEM((LEFT_ROWS + tq, D), BF16),
            pltpu.VMEM((LEFT_ROWS + tq, D), BF16),
            pltpu.VMEM((tq, D), BF16),
            pltpu.VMEM((tq, D), BF16),
        ],
        compiler_params=pltpu.CompilerParams(
            dimension_semantics=("parallel", "arbitrary"), vmem_limit_bytes=48 * 1024 * 1024),
        name="attn_layer",
    )(*args)


SUBLANES = 8


def _gelu_tanh(x):
    return 0.5 * x * (1.0 + jnp.tanh(0.7978845608028654 * (x + 0.044715 * (x * x * x))))


def _lru_layer_kernel(x_ref, sh_ref, sc_ref, g1_ref, ng_ref, win_ref, cw_ref, cb_ref, gw_ref, gb_ref, lam_ref,
                      wout_ref, cs0_ref, h0_ref,
                      y_ref, cs_ref, hl_ref,
                      ext_s, a_s, b_s, hcar_s):
    tt = x_ref.shape[1]
    W = LRU_WIDTH
    t = pl.program_id(1)

    @pl.when(t == 0)
    def _():
        ext_s[0:SUBLANES, :] = cs0_ref[0]
        hcar_s[...] = h0_ref[0]

    x = x_ref[0]
    ms = jnp.mean(x * x, axis=-1, keepdims=True)
    hn = x * lax.rsqrt(ms + EPS) * ng_ref[...]
    hn = hn * (1.0 + sc_ref[0]) + sh_ref[0]
    proj = jnp.dot(hn.astype(BF16), win_ref[...], preferred_element_type=F32)
    gate = _gelu_tanh(proj[:, :W])
    rec = proj[:, W:]
    ext_s[SUBLANES:SUBLANES + tt, :] = rec
    u = cb_ref[...] + cw_ref[3:4, :] * rec
    for k in range(CONV_W - 1):
        u = u + cw_ref[k:k + 1, :] * ext_s[SUBLANES - 3 + k:SUBLANES - 3 + k + tt, :]
    tail = ext_s[tt:tt + SUBLANES, :]
    cs_ref[0] = tail
    ext_s[0:SUBLANES, :] = tail

    sp_in = -lam_ref[...]
    softplus = jnp.maximum(sp_in, 0.0) + jnp.log1p(jnp.exp(-jnp.abs(sp_in)))
    for n in range(LRU_BLOCKS):
        lo = n * LRU_BLOCK_W
        ub = u[:, lo:lo + LRU_BLOCK_W]
        ubb = ub.astype(BF16)
        gr = jnp.dot(ubb, gw_ref[0, n], preferred_element_type=F32) + gb_ref[0:1, lo:lo + LRU_BLOCK_W]
        gi = jnp.dot(ubb, gw_ref[1, n], preferred_element_type=F32) + gb_ref[1:2, lo:lo + LRU_BLOCK_W]
        r = jax.nn.sigmoid(gr)
        i = jax.nn.sigmoid(gi)
        log_a = -RG_C * r * softplus[:, lo:lo + LRU_BLOCK_W]
        a = jnp.exp(log_a)
        a_s[:, lo:lo + LRU_BLOCK_W] = a
        b_s[:, lo:lo + LRU_BLOCK_W] = jnp.sqrt(1.0 - a * a) * i * ub

    def group(gi_, h):
        r0 = pl.multiple_of(gi_ * SUBLANES, SUBLANES)
        a8 = a_s[pl.ds(r0, SUBLANES), :]
        b8 = b_s[pl.ds(r0, SUBLANES), :]
        rows = []
        for j in range(SUBLANES):
            h = a8[j:j + 1, :] * h + b8[j:j + 1, :]
            rows.append(h)
        b_s[pl.ds(r0, SUBLANES), :] = jnp.concatenate(rows, axis=0)
        return h

    h_last = lax.fori_loop(0, tt // SUBLANES, group, hcar_s[...])
    hcar_s[...] = h_last
    hl_ref[0] = h_last
    yv = b_s[...] * gate
    out = jnp.dot(yv.astype(BF16), wout_ref[...], preferred_element_type=F32)
    y_ref[0] = x + g1_ref[0] * out


def _lru_layer(x, sh, sc, g1, ng, w_in, conv_w, conv_b, gate_w, gate_b, lam, w_out, conv_state, h0):
    B, T, D = x.shape
    W = LRU_WIDTH
    tt = min(T, 256)
    cs0 = jnp.pad(conv_state, ((0, 0), (SUBLANES - (CONV_W - 1), 0), (0, 0)))
    tok = lambda b, t: (b, t, 0)
    per_b = lambda b, t: (b, 0, 0)
    c2 = lambda b, t: (0, 0)
    y, cs, hl = pl.pallas_call(
        _lru_layer_kernel,
        grid=(B, T // tt),
        in_specs=[
            pl.BlockSpec((1, tt, D), tok),
            pl.BlockSpec((1, 1, D), per_b), pl.BlockSpec((1, 1, D), per_b), pl.BlockSpec((1, 1, D), per_b),
            pl.BlockSpec((1, D), c2),
            pl.BlockSpec(w_in.shape, c2),
            pl.BlockSpec((CONV_W, W), c2), pl.BlockSpec((1, W), c2),
            pl.BlockSpec(gate_w.shape, lambda b, t: (0, 0, 0, 0)),
            pl.BlockSpec((2, W), c2), pl.BlockSpec((1, W), c2),
            pl.BlockSpec(w_out.shape, c2),
            pl.BlockSpec((1, SUBLANES, W), per_b), pl.BlockSpec((1, 1, W), per_b),
        ],
        out_specs=[pl.BlockSpec((1, tt, D), tok), pl.BlockSpec((1, SUBLANES, W), per_b),
                   pl.BlockSpec((1, 1, W), per_b)],
        out_shape=[jax.ShapeDtypeStruct((B, T, D), F32), jax.ShapeDtypeStruct((B, SUBLANES, W), F32),
                   jax.ShapeDtypeStruct((B, 1, W), F32)],
        scratch_shapes=[
            pltpu.VMEM((tt + SUBLANES, W), F32),
            pltpu.VMEM((tt, W), F32),
            pltpu.VMEM((tt, W), F32),
            pltpu.VMEM((1, W), F32),
        ],
        compiler_params=pltpu.CompilerParams(
            dimension_semantics=("parallel", "arbitrary"), vmem_limit_bytes=48 * 1024 * 1024),
        name="lru_layer",
    )(x, sh, sc, g1, ng, w_in, conv_w, conv_b, gate_w, gate_b, lam, w_out, cs0, h0[:, None, :])
    return y, cs[:, SUBLANES - (CONV_W - 1):], hl[:, 0]


def _ada_kernel(c_ref, w_ref, b_ref, m_ref):
    c = c_ref[...]
    s = c * jax.nn.sigmoid(c)
    m_ref[0] = jnp.dot(s, w_ref[0], preferred_element_type=F32, precision=lax.Precision.HIGHEST) + b_ref[0]


def _ada_all(c_all, ada_w, ada_b):
    L, D, E = ada_w.shape
    R = c_all.shape[0]
    tn = 1536
    return pl.pallas_call(
        _ada_kernel,
        grid=(L, E // tn),
        in_specs=[pl.BlockSpec((R, D), lambda l, j: (0, 0)),
                  pl.BlockSpec((1, D, tn), lambda l, j: (l, 0, j)),
                  pl.BlockSpec((1, 1, tn), lambda l, j: (l, 0, j))],
        out_specs=pl.BlockSpec((1, R, tn), lambda l, j: (l, 0, j)),
        out_shape=jax.ShapeDtypeStruct((L, R, E), F32),
        compiler_params=pltpu.CompilerParams(
            dimension_semantics=("parallel", "parallel"), vmem_limit_bytes=32 * 1024 * 1024),
        name="ada_mod",
    )(c_all, ada_w, ada_b[:, None, :])


def _trunk(x, mods, conv_st, lru_st, k_cache, v_cache, w, prompt):
    B, T, D = x.shape
    new_conv, new_h, new_k, new_v = [], [], [], []
    for i in range(len(mods)):
        sh1, sc1, g1, sh2, sc2, g2 = mods[i]
        j = i // 2
        if i % 2 == 0:
            if prompt:
                cs0 = jnp.zeros((B, CONV_W - 1, LRU_WIDTH), x.dtype)
                h0 = jnp.zeros((B, LRU_WIDTH), F32)
            else:
                cs0, h0 = conv_st[j], lru_st[j]
            x, cnew, hnew = _lru_layer(x, sh1, sc1, g1, w['norm_g'][i, 0][None, :], w['lru_w_in'][j],
                                       w['lru_conv_w'][j], w['lru_conv_b'][j][None, :], w['lru_gate_w'][j],
                                       w['lru_gate_b'][j], w['lru_lambda'][j][None, :], w['lru_w_out'][j], cs0, h0)
            new_conv.append(cnew)
            new_h.append(hnew)
        else:
            if prompt:
                kc = vc = None
            else:
                kc = k_cache[j].reshape(B, LEFT_ROWS, D)
                vc = v_cache[j].reshape(B, LEFT_ROWS, D)
            x, kn, vn = _attn_layer(x, sh1, sc1, g1, w['norm_g'][i, 0][None, :], w['att_w_qkv'][j],
                                    w['att_q_gain'][j][None, :], w['att_k_gain'][j][None, :], w['att_bias'][j],
                                    w['att_w_o'][j], kc, vc)
            new_k.append(kn.reshape(B, kn.shape[1], N_HEADS, HEAD_DIM))
            new_v.append(vn.reshape(B, vn.shape[1], N_HEADS, HEAD_DIM))
        x = _peer_layer(x, sh2, sc2, g2, w['norm_g'][i, 1][None, :], w['peer_wqt'][i], w['peer_sk'][i],
                        w['peer_u'][i], w['peer_v'][i])
    return x, jnp.stack(new_conv), jnp.stack(new_h), jnp.stack(new_k), jnp.stack(new_v)


def kernel(x_prompt, x_sample, c_prompt, c_sample, state_conv, state_lru_h, cache_k, cache_v, norm_g, ada_w, ada_b, lru_w_in, lru_conv_w, lru_conv_b, lru_gate_w, lru_gate_b, lru_lambda, lru_w_out, att_w_qkv, att_q_gain, att_k_gain, att_rel_bias, att_w_o, peer_w_query, peer_sub_keys, peer_u, peer_v):
    depth = norm_g.shape[0]
    bp, bs = c_prompt.shape[0], c_sample.shape[0]
    rows = -(-(bp + bs) // SUBLANES) * SUBLANES
    c_all = jnp.concatenate([c_prompt, c_sample, jnp.zeros((rows - bp - bs, D_MODEL), F32)], axis=0)
    m_all = _ada_all(c_all, ada_w, ada_b)

    def mods(lo, n):
        return [[m_all[i, lo:lo + n, None, k * D_MODEL:(k + 1) * D_MODEL] for k in range(6)] for i in range(depth)]

    w = {
        'norm_g': norm_g,
        'lru_w_in': lru_w_in.astype(BF16), 'lru_conv_w': lru_conv_w, 'lru_conv_b': lru_conv_b,
        'lru_gate_w': lru_gate_w.astype(BF16), 'lru_gate_b': lru_gate_b, 'lru_lambda': lru_lambda,
        'lru_w_out': lru_w_out.astype(BF16),
        'att_w_qkv': att_w_qkv.astype(BF16), 'att_q_gain': att_q_gain, 'att_k_gain': att_k_gain,
        'att_bias': jax.vmap(_rel_bias_tile)(att_rel_bias), 'att_w_o': att_w_o.astype(BF16),
        'peer_wqt': jnp.swapaxes(peer_w_query, 1, 2).astype(BF16),
        'peer_sk': peer_sub_keys.astype(BF16).reshape(depth, 2 * PEER_HEADS, N_KEYS, PEER_HALF),
        'peer_u': peer_u, 'peer_v': peer_v,
    }
    y_prompt, p_conv, p_h, p_k, p_v = _trunk(x_prompt, mods(0, bp), None, None, None, None, w, True)
    y_sample, s_conv, s_h, s_k, s_v = _trunk(x_sample, mods(bp, bs), state_conv, state_lru_h,
                                              cache_k, cache_v, w, False)
    return (y_prompt, y_sample, p_conv, p_h, p_k, p_v, s_conv, s_h, s_k, s_v)
```

```python
import functools

import jax
import jax.numpy as jnp
from jax import lax
from jax.experimental import pallas as pl
from jax.experimental.pallas import tpu as pltpu
from jax.experimental.pallas import tpu_sc as plsc

D_MODEL = 1024
CHUNK = 64
LRU_WIDTH = D_MODEL
LRU_BLOCKS = 4
LRU_BLOCK_W = LRU_WIDTH // LRU_BLOCKS
CONV_W = 4
RG_C = 8.0
N_HEADS = 8
HEAD_DIM = D_MODEL // N_HEADS
LEFT_CHUNKS = 8
LEFT_ROWS = LEFT_CHUNKS * CHUNK
BAND = (LEFT_CHUNKS + 1) * CHUNK
REL_CLIP = 128
N_REL = 2 * REL_CLIP + 1
PEER_HEADS = 8
N_KEYS = 128
N_EXPERTS = N_KEYS * N_KEYS
PEER_TOPK = 16
PEER_QDIM = 256
PEER_HALF = PEER_QDIM // 2
PEER_SEL = PEER_HEADS * PEER_TOPK
EPS = 1e-6
NEG_INF = -1e30
PAST_LEN = 2048

LANES = 128
BF16 = jnp.bfloat16
F32 = jnp.float32


def _peer_route_kernel(x_ref, sh_ref, sc_ref, ng_ref, wqt_ref, sk_ref,
                       hn_ref, idx_ref, g_ref,
                       q_s, ts_s, ti_s, cs_s, ci_s):
    tt = x_ref.shape[1]
    ncand = PEER_TOPK * PEER_TOPK
    x = x_ref[0]
    ms = jnp.mean(x * x, axis=-1, keepdims=True)
    hn = x * lax.rsqrt(ms + EPS) * ng_ref[...]
    hn = hn * (1.0 + sc_ref[0]) + sh_ref[0]
    hn_ref[0] = hn
    qt = lax.dot_general(wqt_ref[...], hn.astype(BF16), (((1,), (1,)), ((), ())),
                         preferred_element_type=F32)
    for gidx in range(2 * PEER_HEADS):
        q_s[gidx] = qt[gidx * PEER_HALF:(gidx + 1) * PEER_HALF, :]

    key_id = lax.broadcasted_iota(jnp.int32, (N_KEYS, tt), 0)
    slot_id = lax.broadcasted_iota(jnp.int32, (PEER_TOPK, tt), 0)

    def list_body(gidx, _):
        s = jnp.dot(sk_ref[gidx], q_s[gidx].astype(BF16), preferred_element_type=F32)
        ts = jnp.zeros((PEER_TOPK, tt), F32)
        ti = jnp.zeros((PEER_TOPK, tt), jnp.int32)
        for j in range(PEER_TOPK):
            m = jnp.max(s, axis=0, keepdims=True)
            am = jnp.min(jnp.where(s == m, key_id, N_KEYS), axis=0, keepdims=True)
            ts = jnp.where(slot_id == j, m, ts)
            ti = jnp.where(slot_id == j, am, ti)
            s = jnp.where(key_id == am, -jnp.inf, s)
        ts_s[gidx] = ts
        ti_s[gidx] = ti
        return 0

    lax.fori_loop(0, 2 * PEER_HEADS, list_body, 0)

    cand = lax.broadcasted_iota(jnp.int32, (ncand, tt), 0)

    def head_body(h, _):
        ts1, ts2 = ts_s[2 * h], ts_s[2 * h + 1]
        ti1, ti2 = ti_s[2 * h], ti_s[2 * h + 1]
        for a in range(PEER_TOPK):
            cs_s[a * PEER_TOPK:(a + 1) * PEER_TOPK, :] = ts1[a:a + 1, :] + ts2
            ci_s[a * PEER_TOPK:(a + 1) * PEER_TOPK, :] = ti1[a:a + 1, :] * N_KEYS + ti2
        cs = cs_s[...]
        ci = ci_s[...]
        fs = jnp.zeros((PEER_TOPK, tt), F32)
        ei = jnp.zeros((PEER_TOPK, tt), jnp.int32)
        for k in range(PEER_TOPK):
            m = jnp.max(cs, axis=0, keepdims=True)
            cmin = jnp.min(jnp.where(cs == m, cand, ncand), axis=0, keepdims=True)
            sel = cand == cmin
            e = jnp.sum(jnp.where(sel, ci, 0), axis=0, keepdims=True)
            fs = jnp.where(slot_id == k, m, fs)
            ei = jnp.where(slot_id == k, e, ei)
            cs = jnp.where(sel, -jnp.inf, cs)
        ex = jnp.exp(fs - fs[0:1, :])
        g = ex / jnp.sum(ex, axis=0, keepdims=True)
        row0 = pl.multiple_of(h * PEER_TOPK, PEER_TOPK)
        g_ref[0, pl.ds(row0, PEER_TOPK), :] = g
        idx_ref[0, pl.ds(row0, PEER_TOPK), :] = ei
        return 0

    lax.fori_loop(0, PEER_HEADS, head_body, 0)


def _peer_route(x, sh, sc, ng, wqt, sk):
    B, T, D = x.shape
    tt = min(T, 128)
    grid = (B, T // tt)
    tok = lambda b, t: (b, t, 0)
    tok_t = lambda b, t: (b, 0, t)
    per_b = lambda b, t: (b, 0, 0)
    ncand = PEER_TOPK * PEER_TOPK
    hn, idx_t, g_t = pl.pallas_call(
        _peer_route_kernel,
        grid=grid,
        in_specs=[
            pl.BlockSpec((1, tt, D), tok),
            pl.BlockSpec((1, 1, D), per_b),
            pl.BlockSpec((1, 1, D), per_b),
            pl.BlockSpec((1, D), lambda b, t: (0, 0)),
            pl.BlockSpec(wqt.shape, lambda b, t: (0, 0)),
            pl.BlockSpec(sk.shape, lambda b, t: (0, 0, 0)),
        ],
        out_specs=[
            pl.BlockSpec((1, tt, D), tok),
            pl.BlockSpec((1, PEER_SEL, tt), tok_t),
            pl.BlockSpec((1, PEER_SEL, tt), tok_t),
        ],
        out_shape=[
            jax.ShapeDtypeStruct((B, T, D), F32),
            jax.ShapeDtypeStruct((B, PEER_SEL, T), jnp.int32),
            jax.ShapeDtypeStruct((B, PEER_SEL, T), F32),
        ],
        scratch_shapes=[
            pltpu.VMEM((2 * PEER_HEADS, PEER_HALF, tt), F32),
            pltpu.VMEM((2 * PEER_HEADS, PEER_TOPK, tt), F32),
            pltpu.VMEM((2 * PEER_HEADS, PEER_TOPK, tt), jnp.int32),
            pltpu.VMEM((ncand, tt), F32),
            pltpu.VMEM((ncand, tt), jnp.int32),
        ],
        compiler_params=pltpu.CompilerParams(
            dimension_semantics=("parallel", "parallel"), vmem_limit_bytes=48 * 1024 * 1024),
        name="peer_route",
    )(x, sh, sc, ng, wqt, sk)
    return hn, jnp.swapaxes(idx_t, 1, 2), jnp.swapaxes(g_t, 1, 2)


SC_CORES = 2
SC_SUBCORES = 16
SC_LANES = 16
SC_WORKERS = SC_CORES * SC_SUBCORES
NCHUNK = D_MODEL // SC_LANES
TOK_BLOCK = 8
NSLOT = 3


def _peer_expert_sc(x, hn, eidx, g, g2, u_tab, v_tab, tokens_per_seq):
    N, D = x.shape
    tpw = N // SC_WORKERS
    assert tpw * SC_WORKERS == N and tpw % TOK_BLOCK == 0
    assert tokens_per_seq % tpw == 0
    workers_per_seq = tokens_per_seq // tpw
    nblocks = tpw // TOK_BLOCK
    steps = TOK_BLOCK * PEER_HEADS
    mesh = plsc.VectorSubcoreMesh(core_axis_name="c", subcore_axis_name="s",
                                  num_cores=SC_CORES, num_subcores=SC_SUBCORES)

    @functools.partial(
        pl.kernel, out_type=jax.ShapeDtypeStruct((N, D), F32), mesh=mesh,
        scratch_types=[
            pltpu.VMEM((TOK_BLOCK, D), F32),
            pltpu.VMEM((TOK_BLOCK, D), F32),
            pltpu.VMEM((TOK_BLOCK, PEER_SEL), jnp.int32),
            pltpu.VMEM((TOK_BLOCK, PEER_SEL), F32),
            pltpu.VMEM((D,), F32),
            pltpu.VMEM((NSLOT, PEER_TOPK, D), F32),
            pltpu.VMEM((NSLOT, PEER_TOPK, D), F32),
            pltpu.VMEM((D,), F32),
            pltpu.SemaphoreType.DMA((NSLOT,)),
            pltpu.SemaphoreType.DMA((NSLOT,)),
        ],
        compiler_params=pltpu.CompilerParams(needs_layout_passes=False),
        name="peer_expert_sc",
    )
    def k(x_hbm, hn_hbm, idx_hbm, g_hbm, g2_hbm, u_hbm, v_hbm, out_hbm,
          hn_b, x_b, idx_b, g_b, g2_v, ubuf, vbuf, o_v, usem, vsem):
        wid = lax.axis_index("s") * SC_CORES + lax.axis_index("c")
        tok0 = wid * tpw
        pltpu.sync_copy(g2_hbm.at[wid // workers_per_seq], g2_v)
        lane = lax.iota(jnp.int32, SC_LANES)

        def gather_descs(s, slot):
            t = s // PEER_HEADS
            h0 = pl.multiple_of((s % PEER_HEADS) * PEER_TOPK, PEER_TOPK)
            ids = idx_b.at[t, pl.ds(h0, PEER_TOPK)]
            return (pltpu.make_async_copy(u_hbm.at[ids], ubuf.at[slot], usem.at[slot]),
                    pltpu.make_async_copy(v_hbm.at[ids], vbuf.at[slot], vsem.at[slot]))

        def issue(s, slot):
            du, dv = gather_descs(s, slot)
            du.start()
            dv.start()

        def wait(s, slot):
            du, dv = gather_descs(s, slot)
            du.wait()
            dv.wait()

        def compute(s, slot):
            t = s // PEER_HEADS
            h = s % PEER_HEADS
            h0 = pl.multiple_of(h * PEER_TOPK, PEER_TOPK)

            @pl.when(h == 0)
            def _():
                @pl.loop(0, NCHUNK)
                def _(c):
                    o_v[pl.ds(pl.multiple_of(c * SC_LANES, SC_LANES), SC_LANES)] = jnp.zeros((SC_LANES,), F32)

            def ubody(c, accs):
                c0 = pl.multiple_of(c * SC_LANES, SC_LANES)
                xv = hn_b[t, pl.ds(c0, SC_LANES)]
                return tuple(accs[kk] + xv * ubuf[slot, kk, pl.ds(c0, SC_LANES)] for kk in range(PEER_TOPK))

            accs = lax.fori_loop(0, NCHUNK, ubody,
                                 tuple(jnp.zeros((SC_LANES,), F32) for _ in range(PEER_TOPK)))
            last = jnp.full((SC_LANES,), SC_LANES - 1, jnp.int32)
            r = jnp.zeros((SC_LANES,), F32)
            for kk in range(PEER_TOPK):
                tot = jnp.take_along_axis(plsc.cumsum(accs[kk]), last, axis=0)
                r = jnp.where(lane == kk, tot, r)
            gv = g_b[t, pl.ds(h0, PEER_TOPK)]
            z = 0.7978845608028654 * (r + 0.044715 * (r * r * r))
            w = gv * r / (1.0 + jnp.exp(-2.0 * z))
            wk = [jnp.take_along_axis(w, jnp.full((SC_LANES,), kk, jnp.int32), axis=0) for kk in range(PEER_TOPK)]

            @plsc.parallel_loop(0, NCHUNK, unroll=2)
            def _(c):
                c0 = pl.multiple_of(c * SC_LANES, SC_LANES)
                parts = [wk[kk] * vbuf[slot, kk, pl.ds(c0, SC_LANES)] for kk in range(PEER_TOPK)]
                while len(parts) > 1:
                    parts = [parts[i] + parts[i + 1] for i in range(0, len(parts), 2)]
                plsc.addupdate(o_v.at[pl.ds(c0, SC_LANES)], parts[0])

            @pl.when(h == PEER_HEADS - 1)
            def _():
                @pl.loop(0, NCHUNK)
                def _(c):
                    c0 = pl.multiple_of(c * SC_LANES, SC_LANES)
                    x_b[t, pl.ds(c0, SC_LANES)] = (x_b[t, pl.ds(c0, SC_LANES)]
                                                   + g2_v[pl.ds(c0, SC_LANES)] * o_v[pl.ds(c0, SC_LANES)])

        @pl.loop(0, nblocks)
        def _(blk):
            tok = pl.multiple_of(tok0 + blk * TOK_BLOCK, TOK_BLOCK)
            pltpu.sync_copy(idx_hbm.at[pl.ds(tok, TOK_BLOCK)], idx_b)
            pltpu.sync_copy(g_hbm.at[pl.ds(tok, TOK_BLOCK)], g_b)
            pltpu.sync_copy(hn_hbm.at[pl.ds(tok, TOK_BLOCK)], hn_b)
            pltpu.sync_copy(x_hbm.at[pl.ds(tok, TOK_BLOCK)], x_b)
            for s0 in range(NSLOT - 1):
                issue(s0, s0)

            @pl.loop(0, steps)
            def _(s):
                @pl.when(s + (NSLOT - 1) < steps)
                def _():
                    issue(s + (NSLOT - 1), (s + (NSLOT - 1)) % NSLOT)

                slot = s % NSLOT
                wait(s, slot)
                compute(s, slot)

            pltpu.sync_copy(x_b, out_hbm.at[pl.ds(tok, TOK_BLOCK)])

    return k(x, hn, eidx, g, g2, u_tab, v_tab)


def _peer_layer(x, sh2, sc2, g2, ng, wqt, sk, u_tab, v_tab):
    B, T, D = x.shape
    hn, eidx, g = _peer_route(x, sh2, sc2, ng, wqt, sk)
    y = _peer_expert_sc(x.reshape(B * T, D), hn.reshape(B * T, D), eidx.reshape(B * T, PEER_SEL),
                        g.reshape(B * T, PEER_SEL), g2.reshape(B, D), u_tab, v_tab, tokens_per_seq=T)
    return y.reshape(B, T, D)


def _attn_layer_kernel(*refs, tq, ch, has_cache):
    if has_cache:
        (x_ref, sh_ref, sc_ref, g1_ref, ng_ref, wqkv_ref, qg_ref, kg_ref, bias_ref, wo_ref, kc_ref, vc_ref,
         y_ref, ko_ref, vo_ref, kcat, vcat, qn_s, o_s) = refs
    else:
        (x_ref, sh_ref, sc_ref, g1_ref, ng_ref, wqkv_ref, qg_ref, kg_ref, bias_ref, wo_ref,
         y_ref, ko_ref, vo_ref, kcat, vcat, qn_s, o_s) = refs
    t = pl.program_id(1)
    band = LEFT_ROWS + ch

    if has_cache:
        kcat[0:LEFT_ROWS, :] = kc_ref[0].astype(BF16)
        vcat[0:LEFT_ROWS, :] = vc_ref[0].astype(BF16)
    else:
        @pl.when(t == 0)
        def _():
            kcat[0:LEFT_ROWS, :] = jnp.zeros((LEFT_ROWS, D_MODEL), BF16)
            vcat[0:LEFT_ROWS, :] = jnp.zeros((LEFT_ROWS, D_MODEL), BF16)

    x = x_ref[0]
    ms = jnp.mean(x * x, axis=-1, keepdims=True)
    hn = x * lax.rsqrt(ms + EPS) * ng_ref[...]
    hn = hn * (1.0 + sc_ref[0]) + sh_ref[0]
    qkv = jnp.dot(hn.astype(BF16), wqkv_ref[...], preferred_element_type=F32)
    for h in range(N_HEADS):
        lo = h * HEAD_DIM
        qh = qkv[:, lo:lo + HEAD_DIM]
        kh = qkv[:, D_MODEL + lo:D_MODEL + lo + HEAD_DIM]
        qh = qh * lax.rsqrt(jnp.mean(qh * qh, axis=-1, keepdims=True) + EPS) * qg_ref[...]
        kh = kh * lax.rsqrt(jnp.mean(kh * kh, axis=-1, keepdims=True) + EPS) * kg_ref[...]
        qn_s[:, lo:lo + HEAD_DIM] = qh.astype(BF16)
        ko_ref[0, :, lo:lo + HEAD_DIM] = kh
        kcat[LEFT_ROWS:LEFT_ROWS + tq, lo:lo + HEAD_DIM] = kh.astype(BF16)
    v = qkv[:, 2 * D_MODEL:]
    vo_ref[0] = v
    vcat[LEFT_ROWS:LEFT_ROWS + tq, :] = v.astype(BF16)

    scale = HEAD_DIM ** -0.5
    kpos_rel = lax.broadcasted_iota(jnp.int32, (ch, band), 1)

    def chunk_body(c, _):
        r0 = pl.multiple_of(c * ch, ch)
        if not has_cache:
            valid = kpos_rel + (t * tq + r0) >= LEFT_ROWS
        for h in range(N_HEADS):
            lo = h * HEAD_DIM
            q = qn_s[pl.ds(r0, ch), lo:lo + HEAD_DIM]
            kb = kcat[pl.ds(r0, band), lo:lo + HEAD_DIM]
            vb = vcat[pl.ds(r0, band), lo:lo + HEAD_DIM]
            s = lax.dot_general(q, kb, (((1,), (1,)), ((), ())), preferred_element_type=F32) * scale
            s = s + bias_ref[h, 0:ch, 0:band]
            if not has_cache:
                s = jnp.where(valid, s, NEG_INF)
            m = jnp.max(s, axis=-1, keepdims=True)
            e = jnp.exp(s - m)
            p = e / jnp.sum(e, axis=-1, keepdims=True)
            o_s[pl.ds(r0, ch), lo:lo + HEAD_DIM] = jnp.dot(p.astype(BF16), vb, preferred_element_type=F32).astype(BF16)
        return 0

    lax.fori_loop(0, tq // ch, chunk_body, 0)

    out = jnp.dot(o_s[...], wo_ref[...], preferred_element_type=F32)
    y_ref[0] = x + g1_ref[0] * out

    if not has_cache:
        kcat[0:LEFT_ROWS, :] = kcat[tq:tq + LEFT_ROWS, :]
        vcat[0:LEFT_ROWS, :] = vcat[tq:tq + LEFT_ROWS, :]


def _rel_bias_tile(rel_bias):
    i = jnp.arange(CHUNK)[:, None]
    r = jnp.arange(LEFT_ROWS + CHUNK)[None, :]
    rel = jnp.clip(i + LEFT_ROWS - r, -REL_CLIP, REL_CLIP) + REL_CLIP
    return rel_bias[:, rel]


def _attn_layer(x, sh, sc, g1, ng, wqkv, qg, kg, bias, wo, k_cache=None, v_cache=None):
    B, T, D = x.shape
    has_cache = k_cache is not None
    if has_cache:
        tq, ch = T, T
        keep_blocks = 1
    else:
        tq, ch = 256, CHUNK
        keep_blocks = LEFT_ROWS // tq
    nt = T // tq
    assert nt >= keep_blocks
    tok = lambda b, t: (b, t, 0)
    per_b = lambda b, t: (b, 0, 0)
    c2 = lambda b, t: (0, 0)
    c3 = lambda b, t: (0, 0, 0)
    keep = lambda b, t: (b, jnp.maximum(t - (nt - keep_blocks), 0), 0)
    in_specs = [
        pl.BlockSpec((1, tq, D), tok),
        pl.BlockSpec((1, 1, D), per_b), pl.BlockSpec((1, 1, D), per_b), pl.BlockSpec((1, 1, D), per_b),
        pl.BlockSpec((1, D), c2),
        pl.BlockSpec(wqkv.shape, c2),
        pl.BlockSpec((1, HEAD_DIM), c2), pl.BlockSpec((1, HEAD_DIM), c2),
        pl.BlockSpec(bias.shape, c3),
        pl.BlockSpec(wo.shape, c2),
    ]
    args = [x, sh, sc, g1, ng, wqkv, qg, kg, bias, wo]
    if has_cache:
        in_specs += [pl.BlockSpec((1, LEFT_ROWS, D), per_b), pl.BlockSpec((1, LEFT_ROWS, D), per_b)]
        args += [k_cache, v_cache]
    return pl.pallas_call(
        functools.partial(_attn_layer_kernel, tq=tq, ch=ch, has_cache=has_cache),
        grid=(B, nt),
        in_specs=in_specs,
        out_specs=[pl.BlockSpec((1, tq, D), tok), pl.BlockSpec((1, tq, D), keep), pl.BlockSpec((1, tq, D), keep)],
        out_shape=[jax.ShapeDtypeStruct((B, T, D), F32),
                   jax.ShapeDtypeStruct((B, keep_blocks * tq, D), F32),
                   jax.ShapeDtypeStruct((B, keep_blocks * tq, D), F32)],
        scratch_shapes=[
            pltpu.VMEM((LEFT_ROWS + tq, D), BF16),
            pltpu.VMEM((LEFT_ROWS + tq, D), BF16),
            pltpu.VMEM((tq, D), BF16),
            pltpu.VMEM((tq, D), BF16),
        ],
        compiler_params=pltpu.CompilerParams(
            dimension_semantics=("parallel", "arbitrary"), vmem_limit_bytes=48 * 1024 * 1024),
        name="attn_layer",
    )(*args)


SUBLANES = 8


def _gelu_tanh(x):
    return 0.5 * x * (1.0 + jnp.tanh(0.7978845608028654 * (x + 0.044715 * (x * x * x))))


def _lru_layer_kernel(x_ref, sh_ref, sc_ref, g1_ref, ng_ref, win_ref, cw_ref, cb_ref, gw_ref, gb_ref, lam_ref,
                      wout_ref, cs0_ref, h0_ref,
                      y_ref, cs_ref, hl_ref,
                      ext_s, a_s, b_s, hcar_s):
    tt = x_ref.shape[1]
    W = LRU_WIDTH
    t = pl.program_id(1)

    @pl.when(t == 0)
    def _():
        ext_s[0:SUBLANES, :] = cs0_ref[0]
        hcar_s[...] = h0_ref[0]

    x = x_ref[0]
    ms = jnp.mean(x * x, axis=-1, keepdims=True)
    hn = x * lax.rsqrt(ms + EPS) * ng_ref[...]
    hn = hn * (1.0 + sc_ref[0]) + sh_ref[0]
    proj = jnp.dot(hn.astype(BF16), win_ref[...], preferred_element_type=F32)
    gate = _gelu_tanh(proj[:, :W])
    rec = proj[:, W:]
    ext_s[SUBLANES:SUBLANES + tt, :] = rec
    u = cb_ref[...] + cw_ref[3:4, :] * rec
    for k in range(CONV_W - 1):
        u = u + cw_ref[k:k + 1, :] * ext_s[SUBLANES - 3 + k:SUBLANES - 3 + k + tt, :]
    tail = ext_s[tt:tt + SUBLANES, :]
    cs_ref[0] = tail
    ext_s[0:SUBLANES, :] = tail

    sp_in = -lam_ref[...]
    softplus = jnp.maximum(sp_in, 0.0) + jnp.log1p(jnp.exp(-jnp.abs(sp_in)))
    for n in range(LRU_BLOCKS):
        lo = n * LRU_BLOCK_W
        ub = u[:, lo:lo + LRU_BLOCK_W]
        ubb = ub.astype(BF16)
        gr = jnp.dot(ubb, gw_ref[0, n], preferred_element_type=F32) + gb_ref[0:1, lo:lo + LRU_BLOCK_W]
        gi = jnp.dot(ubb, gw_ref[1, n], preferred_element_type=F32) + gb_ref[1:2, lo:lo + LRU_BLOCK_W]
        r = jax.nn.sigmoid(gr)
        i = jax.nn.sigmoid(gi)
        log_a = -RG_C * r * softplus[:, lo:lo + LRU_BLOCK_W]
        a = jnp.exp(log_a)
        a_s[:, lo:lo + LRU_BLOCK_W] = a
        b_s[:, lo:lo + LRU_BLOCK_W] = jnp.sqrt(1.0 - a * a) * i * ub

    def group(gi_, h):
        r0 = pl.multiple_of(gi_ * SUBLANES, SUBLANES)
        a8 = a_s[pl.ds(r0, SUBLANES), :]
        b8 = b_s[pl.ds(r0, SUBLANES), :]
        rows = []
        for j in range(SUBLANES):
            h = a8[j:j + 1, :] * h + b8[j:j + 1, :]
            rows.append(h)
        b_s[pl.ds(r0, SUBLANES), :] = jnp.concatenate(rows, axis=0)
        return h

    h_last = lax.fori_loop(0, tt // SUBLANES, group, hcar_s[...])
    hcar_s[...] = h_last
    hl_ref[0] = h_last
    yv = b_s[...] * gate
    out = jnp.dot(yv.astype(BF16), wout_ref[...], preferred_element_type=F32)
    y_ref[0] = x + g1_ref[0] * out


def _lru_layer(x, sh, sc, g1, ng, w_in, conv_w, conv_b, gate_w, gate_b, lam, w_out, conv_state, h0):
    B, T, D = x.shape
    W = LRU_WIDTH
    tt = min(T, 256)
    cs0 = jnp.pad(conv_state, ((0, 0), (SUBLANES - (CONV_W - 1), 0), (0, 0)))
    tok = lambda b, t: (b, t, 0)
    per_b = lambda b, t: (b, 0, 0)
    c2 = lambda b, t: (0, 0)
    y, cs, hl = pl.pallas_call(
        _lru_layer_kernel,
        grid=(B, T // tt),
        in_specs=[
            pl.BlockSpec((1, tt, D), tok),
            pl.BlockSpec((1, 1, D), per_b), pl.BlockSpec((1, 1, D), per_b), pl.BlockSpec((1, 1, D), per_b),
            pl.BlockSpec((1, D), c2),
            pl.BlockSpec(w_in.shape, c2),
            pl.BlockSpec((CONV_W, W), c2), pl.BlockSpec((1, W), c2),
            pl.BlockSpec(gate_w.shape, lambda b, t: (0, 0, 0, 0)),
            pl.BlockSpec((2, W), c2), pl.BlockSpec((1, W), c2),
            pl.BlockSpec(w_out.shape, c2),
            pl.BlockSpec((1, SUBLANES, W), per_b), pl.BlockSpec((1, 1, W), per_b),
        ],
        out_specs=[pl.BlockSpec((1, tt, D), tok), pl.BlockSpec((1, SUBLANES, W), per_b),
                   pl.BlockSpec((1, 1, W), per_b)],
        out_shape=[jax.ShapeDtypeStruct((B, T, D), F32), jax.ShapeDtypeStruct((B, SUBLANES, W), F32),
                   jax.ShapeDtypeStruct((B, 1, W), F32)],
        scratch_shapes=[
            pltpu.VMEM((tt + SUBLANES, W), F32),
            pltpu.VMEM((tt, W), F32),
            pltpu.VMEM((tt, W), F32),
            pltpu.VMEM((1, W), F32),
        ],
        compiler_params=pltpu.CompilerParams(
            dimension_semantics=("parallel", "arbitrary"), vmem_limit_bytes=48 * 1024 * 1024),
        name="lru_layer",
    )(x, sh, sc, g1, ng, w_in, conv_w, conv_b, gate_w, gate_b, lam, w_out, cs0, h0[:, None, :])
    return y, cs[:, SUBLANES - (CONV_W - 1):], hl[:, 0]


def _ada_kernel(c_ref, w_ref, b_ref, m_ref):
    c = c_ref[...]
    s = c * jax.nn.sigmoid(c)
    m_ref[0] = jnp.dot(s, w_ref[0], preferred_element_type=F32, precision=lax.Precision.HIGHEST) + b_ref[0]


def _ada_all(c_all, ada_w, ada_b):
    L, D, E = ada_w.shape
    R = c_all.shape[0]
    tn = 1536
    return pl.pallas_call(
        _ada_kernel,
        grid=(L, E // tn),
        in_specs=[pl.BlockSpec((R, D), lambda l, j: (0, 0)),
                  pl.BlockSpec((1, D, tn), lambda l, j: (l, 0, j)),
                  pl.BlockSpec((1, 1, tn), lambda l, j: (l, 0, j))],
        out_specs=pl.BlockSpec((1, R, tn), lambda l, j: (l, 0, j)),
        out_shape=jax.ShapeDtypeStruct((L, R, E), F32),
        compiler_params=pltpu.CompilerParams(
            dimension_semantics=("parallel", "parallel"), vmem_limit_bytes=32 * 1024 * 1024),
        name="ada_mod",
    )(c_all, ada_w, ada_b[:, None, :])


def _trunk(x, mods, conv_st, lru_st, k_cache, v_cache, w, prompt):
    B, T, D = x.shape
    new_conv, new_h, new_k, new_v = [], [], [], []
    for i in range(len(mods)):
        sh1, sc1, g1, sh2, sc2, g2 = mods[i]
        j = i // 2
        if i % 2 == 0:
            if prompt:
                cs0 = jnp.zeros((B, CONV_W - 1, LRU_WIDTH), x.dtype)
                h0 = jnp.zeros((B, LRU_WIDTH), F32)
            else:
                cs0, h0 = conv_st[j], lru_st[j]
            x, cnew, hnew = _lru_layer(x, sh1, sc1, g1, w['norm_g'][i, 0][None, :], w['lru_w_in'][j],
                                       w['lru_conv_w'][j], w['lru_conv_b'][j][None, :], w['lru_gate_w'][j],
                                       w['lru_gate_b'][j], w['lru_lambda'][j][None, :], w['lru_w_out'][j], cs0, h0)
            new_conv.append(cnew)
            new_h.append(hnew)
        else:
            if prompt:
                kc = vc = None
            else:
                kc = k_cache[j].reshape(B, LEFT_ROWS, D)
                vc = v_cache[j].reshape(B, LEFT_ROWS, D)
            x, kn, vn = _attn_layer(x, sh1, sc1, g1, w['norm_g'][i, 0][None, :], w['att_w_qkv'][j],
                                    w['att_q_gain'][j][None, :], w['att_k_gain'][j][None, :], w['att_bias'][j],
                                    w['att_w_o'][j], kc, vc)
            new_k.append(kn.reshape(B, kn.shape[1], N_HEADS, HEAD_DIM))
            new_v.append(vn.reshape(B, vn.shape[1], N_HEADS, HEAD_DIM))
        x = _peer_layer(x, sh2, sc2, g2, w['norm_g'][i, 1][None, :], w['peer_wqt'][i], w['peer_sk'][i],
                        w['peer_u'][i], w['peer_v'][i])
    return x, jnp.stack(new_conv), jnp.stack(new_h), jnp.stack(new_k), jnp.stack(new_v)


def kernel(x_prompt, x_sample, c_prompt, c_sample, state_conv, state_lru_h, cache_k, cache_v, norm_g, ada_w, ada_b, lru_w_in, lru_conv_w, lru_conv_b, lru_gate_w, lru_gate_b, lru_lambda, lru_w_out, att_w_qkv, att_q_gain, att_k_gain, att_rel_bias, att_w_o, peer_w_query, peer_sub_keys, peer_u, peer_v):
    depth = norm_g.shape[0]
    bp, bs = c_prompt.shape[0], c_sample.shape[0]
    rows = -(-(bp + bs) // SUBLANES) * SUBLANES
    c_all = jnp.concatenate([c_prompt, c_sample, jnp.zeros((rows - bp - bs, D_MODEL), F32)], axis=0)
    m_all = _ada_all(c_all, ada_w, ada_b)

    def mods(lo, n):
        return [[m_all[i, lo:lo + n, None, k * D_MODEL:(k + 1) * D_MODEL] for k in range(6)] for i in range(depth)]

    w = {
        'norm_g': norm_g,
        'lru_w_in': lru_w_in.astype(BF16), 'lru_conv_w': lru_conv_w, 'lru_conv_b': lru_conv_b,
        'lru_gate_w': lru_gate_w.astype(BF16), 'lru_gate_b': lru_gate_b, 'lru_lambda': lru_lambda,
        'lru_w_out': lru_w_out.astype(BF16),
        'att_w_qkv': att_w_qkv.astype(BF16), 'att_q_gain': att_q_gain, 'att_k_gain': att_k_gain,
        'att_bias': jax.vmap(_rel_bias_tile)(att_rel_bias), 'att_w_o': att_w_o.astype(BF16),
        'peer_wqt': jnp.swapaxes(peer_w_query, 1, 2).astype(BF16),
        'peer_sk': peer_sub_keys.astype(BF16).reshape(depth, 2 * PEER_HEADS, N_KEYS, PEER_HALF),
        'peer_u': peer_u, 'peer_v': peer_v,
    }
    y_prompt, p_conv, p_h, p_k, p_v = _trunk(x_prompt, mods(0, bp), None, None, None, None, w, True)
    y_sample, s_conv, s_h, s_k, s_v = _trunk(x_sample, mods(bp, bs), state_conv, state_lru_h,
                                              cache_k, cache_v, w, False)
    return (y_prompt, y_sample, p_conv, p_h, p_k, p_v, s_conv, s_h, s_k, s_v)
```

```python
import functools

import jax
import jax.numpy as jnp
import numpy as np
from jax import lax
from jax.experimental import pallas as pl
from jax.experimental.pallas import tpu as pltpu
from jax.experimental.pallas import tpu_sc as plsc

D_MODEL = 1024
CHUNK = 64
LRU_WIDTH = D_MODEL
LRU_BLOCKS = 4
LRU_BLOCK_W = LRU_WIDTH // LRU_BLOCKS
CONV_W = 4
RG_C = 8.0
N_HEADS = 8
HEAD_DIM = D_MODEL // N_HEADS
LEFT_CHUNKS = 8
LEFT_ROWS = LEFT_CHUNKS * CHUNK
BAND = (LEFT_CHUNKS + 1) * CHUNK
REL_CLIP = 128
N_REL = 2 * REL_CLIP + 1
PEER_HEADS = 8
N_KEYS = 128
N_EXPERTS = N_KEYS * N_KEYS
PEER_TOPK = 16
PEER_QDIM = 256
PEER_HALF = PEER_QDIM // 2
PEER_SEL = PEER_HEADS * PEER_TOPK
EPS = 1e-6
NEG_INF = -1e30
PAST_LEN = 2048

LANES = 128
SUBLANES = 8
BF16 = jnp.bfloat16
F32 = jnp.float32


_CAND_RUNS = [(a, PEER_TOPK // (a + 1)) for a in range(PEER_TOPK)]
_NCAND = sum(n for _, n in _CAND_RUNS)
_NCAND_PAD = -(-_NCAND // SUBLANES) * SUBLANES
_CAND_FLAT = np.full((_NCAND_PAD,), PEER_TOPK * PEER_TOPK, np.int32)
_off = 0
for _a, _n in _CAND_RUNS:
    _CAND_FLAT[_off:_off + _n] = _a * PEER_TOPK + np.arange(_n)
    _off += _n


def _peer_route_kernel(x_ref, sh_ref, sc_ref, ng_ref, wqt_ref, sk_ref, cand_ref,
                       hn_ref, idx_ref, g_ref,
                       q_s, cs_s, ci_s):
    tt = x_ref.shape[1]
    x = x_ref[0]
    ms = jnp.mean(x * x, axis=-1, keepdims=True)
    hn = x * lax.rsqrt(ms + EPS) * ng_ref[...]
    hn = hn * (1.0 + sc_ref[0]) + sh_ref[0]
    hn_ref[0] = hn
    qt = lax.dot_general(wqt_ref[...], hn.astype(BF16), (((1,), (1,)), ((), ())),
                         preferred_element_type=F32)
    for gidx in range(2 * PEER_HEADS):
        q_s[gidx] = qt[gidx * PEER_HALF:(gidx + 1) * PEER_HALF, :]

    key_id = lax.broadcasted_iota(jnp.int32, (N_KEYS, tt), 0)
    slot_id = lax.broadcasted_iota(jnp.int32, (PEER_TOPK, tt), 0)
    cand = cand_ref[:, 0:tt]
    big = PEER_TOPK * PEER_TOPK
    cs_s[_NCAND:_NCAND_PAD, :] = jnp.full((_NCAND_PAD - _NCAND, tt), -jnp.inf, F32)
    ci_s[_NCAND:_NCAND_PAD, :] = jnp.zeros((_NCAND_PAD - _NCAND, tt), jnp.int32)

    def head_body(h, _):
        s = [jnp.dot(sk_ref[2 * h + p], q_s[2 * h + p].astype(BF16), preferred_element_type=F32)
             for p in range(2)]
        ts = [jnp.zeros((PEER_TOPK, tt), F32) for _ in range(2)]
        ti = [jnp.zeros((PEER_TOPK, tt), jnp.int32) for _ in range(2)]
        for j in range(PEER_TOPK):
            for p in range(2):
                m = jnp.max(s[p], axis=0, keepdims=True)
                am = jnp.min(jnp.where(s[p] == m, key_id, N_KEYS), axis=0, keepdims=True)
                ts[p] = jnp.where(slot_id == j, m, ts[p])
                ti[p] = jnp.where(slot_id == j, am, ti[p])
                s[p] = jnp.where(key_id == am, -jnp.inf, s[p])
        off = 0
        for a, n in _CAND_RUNS:
            cs_s[off:off + n, :] = ts[0][a:a + 1, :] + ts[1][0:n, :]
            ci_s[off:off + n, :] = ti[0][a:a + 1, :] * N_KEYS + ti[1][0:n, :]
            off += n
        cs = cs_s[...]
        ci = ci_s[...]
        fs = jnp.zeros((PEER_TOPK, tt), F32)
        ei = jnp.zeros((PEER_TOPK, tt), jnp.int32)
        for k in range(PEER_TOPK):
            m = jnp.max(cs, axis=0, keepdims=True)
            cmin = jnp.min(jnp.where(cs == m, cand, big), axis=0, keepdims=True)
            sel = cand == cmin
            e = jnp.sum(jnp.where(sel, ci, 0), axis=0, keepdims=True)
            fs = jnp.where(slot_id == k, m, fs)
            ei = jnp.where(slot_id == k, e, ei)
            cs = jnp.where(sel, -jnp.inf, cs)
        ex = jnp.exp(fs - fs[0:1, :])
        g = ex / jnp.sum(ex, axis=0, keepdims=True)
        row0 = pl.multiple_of(h * PEER_TOPK, PEER_TOPK)
        g_ref[0, pl.ds(row0, PEER_TOPK), :] = g
        idx_ref[0, pl.ds(row0, PEER_TOPK), :] = ei
        return 0

    lax.fori_loop(0, PEER_HEADS, head_body, 0)


def _peer_route(x, sh, sc, ng, wqt, sk):
    B, T, D = x.shape
    tt = min(T, 128)
    grid = (B, T // tt)
    tok = lambda b, t: (b, t, 0)
    tok_t = lambda b, t: (b, 0, t)
    per_b = lambda b, t: (b, 0, 0)
    cand_tab = jnp.asarray(np.broadcast_to(_CAND_FLAT[:, None], (_NCAND_PAD, LANES)))
    hn, idx_t, g_t = pl.pallas_call(
        _peer_route_kernel,
        grid=grid,
        in_specs=[
            pl.BlockSpec((1, tt, D), tok),
            pl.BlockSpec((1, 1, D), per_b),
            pl.BlockSpec((1, 1, D), per_b),
            pl.BlockSpec((1, D), lambda b, t: (0, 0)),
            pl.BlockSpec(wqt.shape, lambda b, t: (0, 0)),
            pl.BlockSpec(sk.shape, lambda b, t: (0, 0, 0)),
            pl.BlockSpec(cand_tab.shape, lambda b, t: (0, 0)),
        ],
        out_specs=[
            pl.BlockSpec((1, tt, D), tok),
            pl.BlockSpec((1, PEER_SEL, tt), tok_t),
            pl.BlockSpec((1, PEER_SEL, tt), tok_t),
        ],
        out_shape=[
            jax.ShapeDtypeStruct((B, T, D), F32),
            jax.ShapeDtypeStruct((B, PEER_SEL, T), jnp.int32),
            jax.ShapeDtypeStruct((B, PEER_SEL, T), F32),
        ],
        scratch_shapes=[
            pltpu.VMEM((2 * PEER_HEADS, PEER_HALF, tt), F32),
            pltpu.VMEM((_NCAND_PAD, tt), F32),
            pltpu.VMEM((_NCAND_PAD, tt), jnp.int32),
        ],
        compiler_params=pltpu.CompilerParams(
            dimension_semantics=("parallel", "parallel"), vmem_limit_bytes=48 * 1024 * 1024),
        name="peer_route",
    )(x, sh, sc, ng, wqt, sk, cand_tab)
    return hn, jnp.swapaxes(idx_t, 1, 2), jnp.swapaxes(g_t, 1, 2)


SC_CORES = 2
SC_SUBCORES = 16
SC_LANES = 16
SC_WORKERS = SC_CORES * SC_SUBCORES
NCHUNK = D_MODEL // SC_LANES
TOK_BLOCK = 8
NSLOT = 3


def _peer_expert_sc(x, hn, eidx, g, g2, u_tab, v_tab, tokens_per_seq):
    N, D = x.shape
    tpw = N // SC_WORKERS
    assert tpw * SC_WORKERS == N and tpw % TOK_BLOCK == 0
    assert tokens_per_seq % tpw == 0
    workers_per_seq = tokens_per_seq // tpw
    nblocks = tpw // TOK_BLOCK
    steps = TOK_BLOCK * PEER_HEADS
    mesh = plsc.VectorSubcoreMesh(core_axis_name="c", subcore_axis_name="s",
                                  num_cores=SC_CORES, num_subcores=SC_SUBCORES)

    @functools.partial(
        pl.kernel, out_type=jax.ShapeDtypeStruct((N, D), F32), mesh=mesh,
        scratch_types=[
            pltpu.VMEM((TOK_BLOCK, D), F32),
            pltpu.VMEM((TOK_BLOCK, D), F32),
            pltpu.VMEM((TOK_BLOCK, PEER_SEL), jnp.int32),
            pltpu.VMEM((TOK_BLOCK, PEER_SEL), F32),
            pltpu.VMEM((D,), F32),
            pltpu.VMEM((NSLOT, PEER_TOPK, D), F32),
            pltpu.VMEM((NSLOT, PEER_TOPK, D), F32),
            pltpu.VMEM((D,), F32),
            pltpu.SemaphoreType.DMA((NSLOT,)),
            pltpu.SemaphoreType.DMA((NSLOT,)),
        ],
        compiler_params=pltpu.CompilerParams(needs_layout_passes=False),
        name="peer_expert_sc",
    )
    def k(x_hbm, hn_hbm, idx_hbm, g_hbm, g2_hbm, u_hbm, v_hbm, out_hbm,
          hn_b, x_b, idx_b, g_b, g2_v, ubuf, vbuf, o_v, usem, vsem):
        wid = lax.axis_index("s") * SC_CORES + lax.axis_index("c")
        tok0 = wid * tpw
        pltpu.sync_copy(g2_hbm.at[wid // workers_per_seq], g2_v)
        lane = lax.iota(jnp.int32, SC_LANES)

        def gather_descs(s, slot):
            t = s // PEER_HEADS
            h0 = pl.multiple_of((s % PEER_HEADS) * PEER_TOPK, PEER_TOPK)
            ids = idx_b.at[t, pl.ds(h0, PEER_TOPK)]
            return (pltpu.make_async_copy(u_hbm.at[ids], ubuf.at[slot], usem.at[slot]),
                    pltpu.make_async_copy(v_hbm.at[ids], vbuf.at[slot], vsem.at[slot]))

        def issue(s, slot):
            du, dv = gather_descs(s, slot)
            du.start()
            dv.start()

        def wait(s, slot):
            du, dv = gather_descs(s, slot)
            du.wait()
            dv.wait()

        def compute(s, slot):
            t = s // PEER_HEADS
            h = s % PEER_HEADS
            h0 = pl.multiple_of(h * PEER_TOPK, PEER_TOPK)

            @pl.when(h == 0)
            def _():
                @pl.loop(0, NCHUNK)
                def _(c):
                    o_v[pl.ds(pl.multiple_of(c * SC_LANES, SC_LANES), SC_LANES)] = jnp.zeros((SC_LANES,), F32)

            def ubody(c, accs):
                c0 = pl.multiple_of(c * SC_LANES, SC_LANES)
                xv = hn_b[t, pl.ds(c0, SC_LANES)]
                return tuple(accs[kk] + xv * ubuf[slot, kk, pl.ds(c0, SC_LANES)] for kk in range(PEER_TOPK))

            accs = lax.fori_loop(0, NCHUNK, ubody,
                                 tuple(jnp.zeros((SC_LANES,), F32) for _ in range(PEER_TOPK)))
            last = jnp.full((SC_LANES,), SC_LANES - 1, jnp.int32)
            r = jnp.zeros((SC_LANES,), F32)
            for kk in range(PEER_TOPK):
                tot = jnp.take_along_axis(plsc.cumsum(accs[kk]), last, axis=0)
                r = jnp.where(lane == kk, tot, r)
            gv = g_b[t, pl.ds(h0, PEER_TOPK)]
            z = 0.7978845608028654 * (r + 0.044715 * (r * r * r))
            w = gv * r / (1.0 + jnp.exp(-2.0 * z))
            wk = [jnp.take_along_axis(w, jnp.full((SC_LANES,), kk, jnp.int32), axis=0) for kk in range(PEER_TOPK)]

            @plsc.parallel_loop(0, NCHUNK, unroll=2)
            def _(c):
                c0 = pl.multiple_of(c * SC_LANES, SC_LANES)
                parts = [wk[kk] * vbuf[slot, kk, pl.ds(c0, SC_LANES)] for kk in range(PEER_TOPK)]
                while len(parts) > 1:
                    parts = [parts[i] + parts[i + 1] for i in range(0, len(parts), 2)]
                plsc.addupdate(o_v.at[pl.ds(c0, SC_LANES)], parts[0])

            @pl.when(h == PEER_HEADS - 1)
            def _():
                @pl.loop(0, NCHUNK)
                def _(c):
                    c0 = pl.multiple_of(c * SC_LANES, SC_LANES)
                    x_b[t, pl.ds(c0, SC_LANES)] = (x_b[t, pl.ds(c0, SC_LANES)]
                                                   + g2_v[pl.ds(c0, SC_LANES)] * o_v[pl.ds(c0, SC_LANES)])

        @pl.loop(0, nblocks)
        def _(blk):
            tok = pl.multiple_of(tok0 + blk * TOK_BLOCK, TOK_BLOCK)
            pltpu.sync_copy(idx_hbm.at[pl.ds(tok, TOK_BLOCK)], idx_b)
            pltpu.sync_copy(g_hbm.at[pl.ds(tok, TOK_BLOCK)], g_b)
            pltpu.sync_copy(hn_hbm.at[pl.ds(tok, TOK_BLOCK)], hn_b)
            pltpu.sync_copy(x_hbm.at[pl.ds(tok, TOK_BLOCK)], x_b)
            for s0 in range(NSLOT - 1):
                issue(s0, s0)

            @pl.loop(0, steps)
            def _(s):
                @pl.when(s + (NSLOT - 1) < steps)
                def _():
                    issue(s + (NSLOT - 1), (s + (NSLOT - 1)) % NSLOT)

                slot = s % NSLOT
                wait(s, slot)
                compute(s, slot)

            pltpu.sync_copy(x_b, out_hbm.at[pl.ds(tok, TOK_BLOCK)])

    return k(x, hn, eidx, g, g2, u_tab, v_tab)


def _peer_layer(x, sh2, sc2, g2, ng, wqt, sk, u_tab, v_tab):
    B, T, D = x.shape
    hn, eidx, g = _peer_route(x, sh2, sc2, ng, wqt, sk)
    y = _peer_expert_sc(x.reshape(B * T, D), hn.reshape(B * T, D), eidx.reshape(B * T, PEER_SEL),
                        g.reshape(B * T, PEER_SEL), g2.reshape(B, D), u_tab, v_tab, tokens_per_seq=T)
    return y.reshape(B, T, D)


def _attn_layer_kernel(*refs, tq, ch, has_cache):
    if has_cache:
        (x_ref, sh_ref, sc_ref, g1_ref, ng_ref, wqkv_ref, qg_ref, kg_ref, bias_ref, wo_ref, kc_ref, vc_ref,
         y_ref, ko_ref, vo_ref, kcat, vcat, qn_s, o_s) = refs
    else:
        (x_ref, sh_ref, sc_ref, g1_ref, ng_ref, wqkv_ref, qg_ref, kg_ref, bias_ref, wo_ref,
         y_ref, ko_ref, vo_ref, kcat, vcat, qn_s, o_s) = refs
    t = pl.program_id(1)
    band = LEFT_ROWS + ch

    if has_cache:
        kcat[0:LEFT_ROWS, :] = kc_ref[0].astype(BF16)
        vcat[0:LEFT_ROWS, :] = vc_ref[0].astype(BF16)
    else:
        @pl.when(t == 0)
        def _():
            kcat[0:LEFT_ROWS, :] = jnp.zeros((LEFT_ROWS, D_MODEL), BF16)
            vcat[0:LEFT_ROWS, :] = jnp.zeros((LEFT_ROWS, D_MODEL), BF16)

    x = x_ref[0]
    ms = jnp.mean(x * x, axis=-1, keepdims=True)
    hn = x * lax.rsqrt(ms + EPS) * ng_ref[...]
    hn = hn * (1.0 + sc_ref[0]) + sh_ref[0]
    qkv = jnp.dot(hn.astype(BF16), wqkv_ref[...], preferred_element_type=F32)
    for h in range(N_HEADS):
        lo = h * HEAD_DIM
        qh = qkv[:, lo:lo + HEAD_DIM]
        kh = qkv[:, D_MODEL + lo:D_MODEL + lo + HEAD_DIM]
        qh = qh * lax.rsqrt(jnp.mean(qh * qh, axis=-1, keepdims=True) + EPS) * qg_ref[...]
        kh = kh * lax.rsqrt(jnp.mean(kh * kh, axis=-1, keepdims=True) + EPS) * kg_ref[...]
        qn_s[:, lo:lo + HEAD_DIM] = qh.astype(BF16)
        ko_ref[0, :, lo:lo + HEAD_DIM] = kh
        kcat[LEFT_ROWS:LEFT_ROWS + tq, lo:lo + HEAD_DIM] = kh.astype(BF16)
    v = qkv[:, 2 * D_MODEL:]
    vo_ref[0] = v
    vcat[LEFT_ROWS:LEFT_ROWS + tq, :] = v.astype(BF16)

    scale = HEAD_DIM ** -0.5
    kpos_rel = lax.broadcasted_iota(jnp.int32, (ch, band), 1)

    def chunk_body(c, _):
        r0 = pl.multiple_of(c * ch, ch)
        if not has_cache:
            valid = kpos_rel + (t * tq + r0) >= LEFT_ROWS
        for h in range(N_HEADS):
            lo = h * HEAD_DIM
            q = qn_s[pl.ds(r0, ch), lo:lo + HEAD_DIM]
            kb = kcat[pl.ds(r0, band), lo:lo + HEAD_DIM]
            vb = vcat[pl.ds(r0, band), lo:lo + HEAD_DIM]
            s = lax.dot_general(q, kb, (((1,), (1,)), ((), ())), preferred_element_type=F32) * scale
            s = s + bias_ref[h, 0:ch, 0:band]
            if not has_cache:
                s = jnp.where(valid, s, NEG_INF)
            m = jnp.max(s, axis=-1, keepdims=True)
            e = jnp.exp(s - m)
            p = e / jnp.sum(e, axis=-1, keepdims=True)
            o_s[pl.ds(r0, ch), lo:lo + HEAD_DIM] = jnp.dot(p.astype(BF16), vb, preferred_element_type=F32).astype(BF16)
        return 0

    lax.fori_loop(0, tq // ch, chunk_body, 0)

    out = jnp.dot(o_s[...], wo_ref[...], preferred_element_type=F32)
    y_ref[0] = x + g1_ref[0] * out

    if not has_cache:
        kcat[0:LEFT_ROWS, :] = kcat[tq:tq + LEFT_ROWS, :]
        vcat[0:LEFT_ROWS, :] = vcat[tq:tq + LEFT_ROWS, :]


def _rel_bias_tile(rel_bias):
    i = jnp.arange(CHUNK)[:, None]
    r = jnp.arange(LEFT_ROWS + CHUNK)[None, :]
    rel = jnp.clip(i + LEFT_ROWS - r, -REL_CLIP, REL_CLIP) + REL_CLIP
    return rel_bias[:, rel]


def _attn_layer(x, sh, sc, g1, ng, wqkv, qg, kg, bias, wo, k_cache=None, v_cache=None):
    B, T, D = x.shape
    has_cache = k_cache is not None
    if has_cache:
        tq, ch = T, T
        keep_blocks = 1
    else:
        tq, ch = 256, CHUNK
        keep_blocks = LEFT_ROWS // tq
    nt = T // tq
    assert nt >= keep_blocks
    tok = lambda b, t: (b, t, 0)
    per_b = lambda b, t: (b, 0, 0)
    c2 = lambda b, t: (0, 0)
    c3 = lambda b, t: (0, 0, 0)
    keep = lambda b, t: (b, jnp.maximum(t - (nt - keep_blocks), 0), 0)
    in_specs = [
        pl.BlockSpec((1, tq, D), tok),
        pl.BlockSpec((1, 1, D), per_b), pl.BlockSpec((1, 1, D), per_b), pl.BlockSpec((1, 1, D), per_b),
        pl.BlockSpec((1, D), c2),
        pl.BlockSpec(wqkv.shape, c2),
        pl.BlockSpec((1, HEAD_DIM), c2), pl.BlockSpec((1, HEAD_DIM), c2),
        pl.BlockSpec(bias.shape, c3),
        pl.BlockSpec(wo.shape, c2),
    ]
    args = [x, sh, sc, g1, ng, wqkv, qg, kg, bias, wo]
    if has_cache:
        in_specs += [pl.BlockSpec((1, LEFT_ROWS, D), per_b), pl.BlockSpec((1, LEFT_ROWS, D), per_b)]
        args += [k_cache, v_cache]
    return pl.pallas_call(
        functools.partial(_attn_layer_kernel, tq=tq, ch=ch, has_cache=has_cache),
        grid=(B, nt),
        in_specs=in_specs,
        out_specs=[pl.BlockSpec((1, tq, D), tok), pl.BlockSpec((1, tq, D), keep), pl.BlockSpec((1, tq, D), keep)],
        out_shape=[jax.ShapeDtypeStruct((B, T, D), F32),
                   jax.ShapeDtypeStruct((B, keep_blocks * tq, D), F32),
                   jax.ShapeDtypeStruct((B, keep_blocks * tq, D), F32)],
        scratch_shapes=[
            pltpu.VMEM((LEFT_ROWS + tq, D), BF16),
            pltpu.VMEM((LEFT_ROWS + tq, D), BF16),
            pltpu.VMEM((tq, D), BF16),
            pltpu.VMEM((tq, D), BF16),
        ],
        compiler_params=pltpu.CompilerParams(
            dimension_semantics=("parallel", "arbitrary"), vmem_limit_bytes=48 * 1024 * 1024),
        name="attn_layer",
    )(*args)


def _gelu_tanh(x):
    return 0.5 * x * (1.0 + jnp.tanh(0.7978845608028654 * (x + 0.044715 * (x * x * x))))


def _lru_layer_kernel(x_ref, sh_ref, sc_ref, g1_ref, ng_ref, win_ref, cw_ref, cb_ref, gw_ref, gb_ref, lam_ref,
                      wout_ref, cs0_ref, h0_ref,
                      y_ref, cs_ref, hl_ref,
                      ext_s, a_s, b_s, hcar_s):
    tt = x_ref.shape[1]
    W = LRU_WIDTH
    t = pl.program_id(1)

    @pl.when(t == 0)
    def _():
        ext_s[0:SUBLANES, :] = cs0_ref[0]
        hcar_s[...] = h0_ref[0]

    x = x_ref[0]
    ms = jnp.mean(x * x, axis=-1, keepdims=True)
    hn = x * lax.rsqrt(ms + EPS) * ng_ref[...]
    hn = hn * (1.0 + sc_ref[0]) + sh_ref[0]
    proj = jnp.dot(hn.astype(BF16), win_ref[...], preferred_element_type=F32)
    gate = _gelu_tanh(proj[:, :W])
    rec = proj[:, W:]
    ext_s[SUBLANES:SUBLANES + tt, :] = rec
    u = cb_ref[...] + cw_ref[3:4, :] * rec
    for k in range(CONV_W - 1):
        u = u + cw_ref[k:k + 1, :] * ext_s[SUBLANES - 3 + k:SUBLANES - 3 + k + tt, :]
    tail = ext_s[tt:tt + SUBLANES, :]
    cs_ref[0] = tail
    ext_s[0:SUBLANES, :] = tail

    sp_in = -lam_ref[...]
    softplus = jnp.maximum(sp_in, 0.0) + jnp.log1p(jnp.exp(-jnp.abs(sp_in)))
    for n in range(LRU_BLOCKS):
        lo = n * LRU_BLOCK_W
        ub = u[:, lo:lo + LRU_BLOCK_W]
        ubb = ub.astype(BF16)
        gr = jnp.dot(ubb, gw_ref[0, n], preferred_element_type=F32) + gb_ref[0:1, lo:lo + LRU_BLOCK_W]
        gi = jnp.dot(ubb, gw_ref[1, n], preferred_element_type=F32) + gb_ref[1:2, lo:lo + LRU_BLOCK_W]
        r = jax.nn.sigmoid(gr)
        i = jax.nn.sigmoid(gi)
        log_a = -RG_C * r * softplus[:, lo:lo + LRU_BLOCK_W]
        a = jnp.exp(log_a)
        a_s[:, lo:lo + LRU_BLOCK_W] = a
        b_s[:, lo:lo + LRU_BLOCK_W] = jnp.sqrt(1.0 - a * a) * i * ub

    def group(gi_, h):
        r0 = pl.multiple_of(gi_ * SUBLANES, SUBLANES)
        a8 = a_s[pl.ds(r0, SUBLANES), :]
        b8 = b_s[pl.ds(r0, SUBLANES), :]
        rows = []
        for j in range(SUBLANES):
            h = a8[j:j + 1, :] * h + b8[j:j + 1, :]
            rows.append(h)
        b_s[pl.ds(r0, SUBLANES), :] = jnp.concatenate(rows, axis=0)
        return h

    h_last = lax.fori_loop(0, tt // SUBLANES, group, hcar_s[...])
    hcar_s[...] = h_last
    hl_ref[0] = h_last
    yv = b_s[...] * gate
    out = jnp.dot(yv.astype(BF16), wout_ref[...], preferred_element_type=F32)
    y_ref[0] = x + g1_ref[0] * out


def _lru_layer(x, sh, sc, g1, ng, w_in, conv_w, conv_b, gate_w, gate_b, lam, w_out, conv_state, h0):
    B, T, D = x.shape
    W = LRU_WIDTH
    tt = min(T, 256)
    cs0 = jnp.pad(conv_state, ((0, 0), (SUBLANES - (CONV_W - 1), 0), (0, 0)))
    tok = lambda b, t: (b, t, 0)
    per_b = lambda b, t: (b, 0, 0)
    c2 = lambda b, t: (0, 0)
    y, cs, hl = pl.pallas_call(
        _lru_layer_kernel,
        grid=(B, T // tt),
        in_specs=[
            pl.BlockSpec((1, tt, D), tok),
            pl.BlockSpec((1, 1, D), per_b), pl.BlockSpec((1, 1, D), per_b), pl.BlockSpec((1, 1, D), per_b),
            pl.BlockSpec((1, D), c2),
            pl.BlockSpec(w_in.shape, c2),
            pl.BlockSpec((CONV_W, W), c2), pl.BlockSpec((1, W), c2),
            pl.BlockSpec(gate_w.shape, lambda b, t: (0, 0, 0, 0)),
            pl.BlockSpec((2, W), c2), pl.BlockSpec((1, W), c2),
            pl.BlockSpec(w_out.shape, c2),
            pl.BlockSpec((1, SUBLANES, W), per_b), pl.BlockSpec((1, 1, W), per_b),
        ],
        out_specs=[pl.BlockSpec((1, tt, D), tok), pl.BlockSpec((1, SUBLANES, W), per_b),
                   pl.BlockSpec((1, 1, W), per_b)],
        out_shape=[jax.ShapeDtypeStruct((B, T, D), F32), jax.ShapeDtypeStruct((B, SUBLANES, W), F32),
                   jax.ShapeDtypeStruct((B, 1, W), F32)],
        scratch_shapes=[
            pltpu.VMEM((tt + SUBLANES, W), F32),
            pltpu.VMEM((tt, W), F32),
            pltpu.VMEM((tt, W), F32),
            pltpu.VMEM((1, W), F32),
        ],
        compiler_params=pltpu.CompilerParams(
            dimension_semantics=("parallel", "arbitrary"), vmem_limit_bytes=48 * 1024 * 1024),
        name="lru_layer",
    )(x, sh, sc, g1, ng, w_in, conv_w, conv_b, gate_w, gate_b, lam, w_out, cs0, h0[:, None, :])
    return y, cs[:, SUBLANES - (CONV_W - 1):], hl[:, 0]


def _ada_kernel(c_ref, w_ref, b_ref, m_ref):
    c = c_ref[...]
    s = c * jax.nn.sigmoid(c)
    m_ref[0] = jnp.dot(s, w_ref[0], preferred_element_type=F32, precision=lax.Precision.HIGHEST) + b_ref[0]


def _ada_all(c_all, ada_w, ada_b):
    L, D, E = ada_w.shape
    R = c_all.shape[0]
    tn = 1536
    return pl.pallas_call(
        _ada_kernel,
        grid=(L, E // tn),
        in_specs=[pl.BlockSpec((R, D), lambda l, j: (0, 0)),
                  pl.BlockSpec((1, D, tn), lambda l, j: (l, 0, j)),
                  pl.BlockSpec((1, 1, tn), lambda l, j: (l, 0, j))],
        out_specs=pl.BlockSpec((1, R, tn), lambda l, j: (l, 0, j)),
        out_shape=jax.ShapeDtypeStruct((L, R, E), F32),
        compiler_params=pltpu.CompilerParams(
            dimension_semantics=("parallel", "parallel"), vmem_limit_bytes=32 * 1024 * 1024),
        name="ada_mod",
    )(c_all, ada_w, ada_b[:, None, :])


def _trunk(x, mods, conv_st, lru_st, k_cache, v_cache, w, prompt):
    B, T, D = x.shape
    new_conv, new_h, new_k, new_v = [], [], [], []
    for i in range(len(mods)):
        sh1, sc1, g1, sh2, sc2, g2 = mods[i]
        j = i // 2
        if i % 2 == 0:
            if prompt:
                cs0 = jnp.zeros((B, CONV_W - 1, LRU_WIDTH), x.dtype)
                h0 = jnp.zeros((B, LRU_WIDTH), F32)
            else:
                cs0, h0 = conv_st[j], lru_st[j]
            x, cnew, hnew = _lru_layer(x, sh1, sc1, g1, w['norm_g'][i, 0][None, :], w['lru_w_in'][j],
                                       w['lru_conv_w'][j], w['lru_conv_b'][j][None, :], w['lru_gate_w'][j],
                                       w['lru_gate_b'][j], w['lru_lambda'][j][None, :], w['lru_w_out'][j], cs0, h0)
            new_conv.append(cnew)
            new_h.append(hnew)
        else:
            if prompt:
                kc = vc = None
            else:
                kc = k_cache[j].reshape(B, LEFT_ROWS, D)
                vc = v_cache[j].reshape(B, LEFT_ROWS, D)
            x, kn, vn = _attn_layer(x, sh1, sc1, g1, w['norm_g'][i, 0][None, :], w['att_w_qkv'][j],
                                    w['att_q_gain'][j][None, :], w['att_k_gain'][j][None, :], w['att_bias'][j],
                                    w['att_w_o'][j], kc, vc)
            new_k.append(kn.reshape(B, kn.shape[1], N_HEADS, HEAD_DIM))
            new_v.append(vn.reshape(B, vn.shape[1], N_HEADS, HEAD_DIM))
        x = _peer_layer(x, sh2, sc2, g2, w['norm_g'][i, 1][None, :], w['peer_wqt'][i], w['peer_sk'][i],
                        w['peer_u'][i], w['peer_v'][i])
    return x, jnp.stack(new_conv), jnp.stack(new_h), jnp.stack(new_k), jnp.stack(new_v)


def kernel(x_prompt, x_sample, c_prompt, c_sample, state_conv, state_lru_h, cache_k, cache_v, norm_g, ada_w, ada_b, lru_w_in, lru_conv_w, lru_conv_b, lru_gate_w, lru_gate_b, lru_lambda, lru_w_out, att_w_qkv, att_q_gain, att_k_gain, att_rel_bias, att_w_o, peer_w_query, peer_sub_keys, peer_u, peer_v):
    depth = norm_g.shape[0]
    bp, bs = c_prompt.shape[0], c_sample.shape[0]
    rows = -(-(bp + bs) // SUBLANES) * SUBLANES
    c_all = jnp.concatenate([c_prompt, c_sample, jnp.zeros((rows - bp - bs, D_MODEL), F32)], axis=0)
    m_all = _ada_all(c_all, ada_w, ada_b)

    def mods(lo, n):
        return [[m_all[i, lo:lo + n, None, k * D_MODEL:(k + 1) * D_MODEL] for k in range(6)] for i in range(depth)]

    w = {
        'norm_g': norm_g,
        'lru_w_in': lru_w_in.astype(BF16), 'lru_conv_w': lru_conv_w, 'lru_conv_b': lru_conv_b,
        'lru_gate_w': lru_gate_w.astype(BF16), 'lru_gate_b': lru_gate_b, 'lru_lambda': lru_lambda,
        'lru_w_out': lru_w_out.astype(BF16),
        'att_w_qkv': att_w_qkv.astype(BF16), 'att_q_gain': att_q_gain, 'att_k_gain': att_k_gain,
        'att_bias': jax.vmap(_rel_bias_tile)(att_rel_bias), 'att_w_o': att_w_o.astype(BF16),
        'peer_wqt': jnp.swapaxes(peer_w_query, 1, 2).astype(BF16),
        'peer_sk': peer_sub_keys.astype(BF16).reshape(depth, 2 * PEER_HEADS, N_KEYS, PEER_HALF),
        'peer_u': peer_u, 'peer_v': peer_v,
    }
    y_prompt, p_conv, p_h, p_k, p_v = _trunk(x_prompt, mods(0, bp), None, None, None, None, w, True)
    y_sample, s_conv, s_h, s_k, s_v = _trunk(x_sample, mods(bp, bs), state_conv, state_lru_h,
                                              cache_k, cache_v, w, False)
    return (y_prompt, y_sample, p_conv, p_h, p_k, p_v, s_conv, s_h, s_k, s_v)
```

```python
import functools

import jax
import jax.numpy as jnp
import numpy as np
from jax import lax
from jax.experimental import pallas as pl
from jax.experimental.pallas import tpu as pltpu
from jax.experimental.pallas import tpu_sc as plsc

D_MODEL = 1024
CHUNK = 64
LRU_WIDTH = D_MODEL
LRU_BLOCKS = 4
LRU_BLOCK_W = LRU_WIDTH // LRU_BLOCKS
CONV_W = 4
RG_C = 8.0
N_HEADS = 8
HEAD_DIM = D_MODEL // N_HEADS
LEFT_CHUNKS = 8
LEFT_ROWS = LEFT_CHUNKS * CHUNK
BAND = (LEFT_CHUNKS + 1) * CHUNK
REL_CLIP = 128
N_REL = 2 * REL_CLIP + 1
PEER_HEADS = 8
N_KEYS = 128
N_EXPERTS = N_KEYS * N_KEYS
PEER_TOPK = 16
PEER_QDIM = 256
PEER_HALF = PEER_QDIM // 2
PEER_SEL = PEER_HEADS * PEER_TOPK
EPS = 1e-6
NEG_INF = -1e30
PAST_LEN = 2048

LANES = 128
SUBLANES = 8
BF16 = jnp.bfloat16
F32 = jnp.float32


_CAND_RUNS = [(a, PEER_TOPK // (a + 1)) for a in range(PEER_TOPK)]
_NCAND = sum(n for _, n in _CAND_RUNS)
_NCAND_PAD = -(-_NCAND // SUBLANES) * SUBLANES
_CAND_FLAT = np.full((_NCAND_PAD,), PEER_TOPK * PEER_TOPK, np.int32)
_off = 0
for _a, _n in _CAND_RUNS:
    _CAND_FLAT[_off:_off + _n] = _a * PEER_TOPK + np.arange(_n)
    _off += _n


def _peer_route_kernel(x_ref, sh_ref, sc_ref, ng_ref, wqt_ref, sk_ref, cand_ref,
                       hn_ref, idx_ref, g_ref,
                       q_s, cs_s, ci_s):
    tt = x_ref.shape[1]
    x = x_ref[0]
    ms = jnp.mean(x * x, axis=-1, keepdims=True)
    hn = x * lax.rsqrt(ms + EPS) * ng_ref[...]
    hn = hn * (1.0 + sc_ref[0]) + sh_ref[0]
    hn_ref[0] = hn
    qt = lax.dot_general(wqt_ref[...], hn.astype(BF16), (((1,), (1,)), ((), ())),
                         preferred_element_type=F32)
    for gidx in range(2 * PEER_HEADS):
        q_s[gidx] = qt[gidx * PEER_HALF:(gidx + 1) * PEER_HALF, :]

    key_id = lax.broadcasted_iota(jnp.int32, (N_KEYS, tt), 0)
    slot_id = lax.broadcasted_iota(jnp.int32, (PEER_TOPK, tt), 0)
    cand = cand_ref[:, 0:tt]
    big = PEER_TOPK * PEER_TOPK
    cs_s[_NCAND:_NCAND_PAD, :] = jnp.full((_NCAND_PAD - _NCAND, tt), -jnp.inf, F32)
    ci_s[_NCAND:_NCAND_PAD, :] = jnp.zeros((_NCAND_PAD - _NCAND, tt), jnp.int32)

    def head_body(h, _):
        s = [jnp.dot(sk_ref[2 * h + p], q_s[2 * h + p].astype(BF16), preferred_element_type=F32)
             for p in range(2)]
        ts = [jnp.zeros((PEER_TOPK, tt), F32) for _ in range(2)]
        ti = [jnp.zeros((PEER_TOPK, tt), jnp.int32) for _ in range(2)]
        for j in range(PEER_TOPK):
            for p in range(2):
                m = jnp.max(s[p], axis=0, keepdims=True)
                am = jnp.min(jnp.where(s[p] == m, key_id, N_KEYS), axis=0, keepdims=True)
                ts[p] = jnp.where(slot_id == j, m, ts[p])
                ti[p] = jnp.where(slot_id == j, am, ti[p])
                s[p] = jnp.where(key_id == am, -jnp.inf, s[p])
        off = 0
        for a, n in _CAND_RUNS:
            cs_s[off:off + n, :] = ts[0][a:a + 1, :] + ts[1][0:n, :]
            ci_s[off:off + n, :] = ti[0][a:a + 1, :] * N_KEYS + ti[1][0:n, :]
            off += n
        cs = cs_s[...]
        ci = ci_s[...]
        fs = jnp.zeros((PEER_TOPK, tt), F32)
        ei = jnp.zeros((PEER_TOPK, tt), jnp.int32)
        for k in range(PEER_TOPK):
            m = jnp.max(cs, axis=0, keepdims=True)
            cmin = jnp.min(jnp.where(cs == m, cand, big), axis=0, keepdims=True)
            sel = cand == cmin
            e = jnp.sum(jnp.where(sel, ci, 0), axis=0, keepdims=True)
            fs = jnp.where(slot_id == k, m, fs)
            ei = jnp.where(slot_id == k, e, ei)
            cs = jnp.where(sel, -jnp.inf, cs)
        ex = jnp.exp(fs - fs[0:1, :])
        g = ex / jnp.sum(ex, axis=0, keepdims=True)
        row0 = pl.multiple_of(h * PEER_TOPK, PEER_TOPK)
        g_ref[0, pl.ds(row0, PEER_TOPK), :] = g
        idx_ref[0, pl.ds(row0, PEER_TOPK), :] = ei
        return 0

    lax.fori_loop(0, PEER_HEADS, head_body, 0)


def _peer_route(x, sh, sc, ng, wqt, sk):
    B, T, D = x.shape
    tt = min(T, 128)
    grid = (B, T // tt)
    tok = lambda b, t: (b, t, 0)
    tok_t = lambda b, t: (b, 0, t)
    per_b = lambda b, t: (b, 0, 0)
    cand_tab = jnp.asarray(np.broadcast_to(_CAND_FLAT[:, None], (_NCAND_PAD, LANES)))
    hn, idx_t, g_t = pl.pallas_call(
        _peer_route_kernel,
        grid=grid,
        in_specs=[
            pl.BlockSpec((1, tt, D), tok),
            pl.BlockSpec((1, 1, D), per_b),
            pl.BlockSpec((1, 1, D), per_b),
            pl.BlockSpec((1, D), lambda b, t: (0, 0)),
            pl.BlockSpec(wqt.shape, lambda b, t: (0, 0)),
            pl.BlockSpec(sk.shape, lambda b, t: (0, 0, 0)),
            pl.BlockSpec(cand_tab.shape, lambda b, t: (0, 0)),
        ],
        out_specs=[
            pl.BlockSpec((1, tt, D), tok),
            pl.BlockSpec((1, PEER_SEL, tt), tok_t),
            pl.BlockSpec((1, PEER_SEL, tt), tok_t),
        ],
        out_shape=[
            jax.ShapeDtypeStruct((B, T, D), F32),
            jax.ShapeDtypeStruct((B, PEER_SEL, T), jnp.int32),
            jax.ShapeDtypeStruct((B, PEER_SEL, T), F32),
        ],
        scratch_shapes=[
            pltpu.VMEM((2 * PEER_HEADS, PEER_HALF, tt), F32),
            pltpu.VMEM((_NCAND_PAD, tt), F32),
            pltpu.VMEM((_NCAND_PAD, tt), jnp.int32),
        ],
        compiler_params=pltpu.CompilerParams(
            dimension_semantics=("parallel", "parallel"), vmem_limit_bytes=48 * 1024 * 1024),
        name="peer_route",
    )(x, sh, sc, ng, wqt, sk, cand_tab)
    return hn, jnp.swapaxes(idx_t, 1, 2), jnp.swapaxes(g_t, 1, 2)


SC_CORES = 2
SC_SUBCORES = 16
SC_LANES = 16
SC_WORKERS = SC_CORES * SC_SUBCORES
NCHUNK = D_MODEL // SC_LANES
TOK_BLOCK = 8
NSLOT = 3


def _peer_expert_sc(x, hn, eidx, g, g2, u_tab, v_tab, tokens_per_seq):
    N, D = x.shape
    tpw = N // SC_WORKERS
    assert tpw * SC_WORKERS == N and tpw % TOK_BLOCK == 0
    assert tokens_per_seq % tpw == 0
    workers_per_seq = tokens_per_seq // tpw
    nblocks = tpw // TOK_BLOCK
    steps = TOK_BLOCK * PEER_HEADS
    mesh = plsc.VectorSubcoreMesh(core_axis_name="c", subcore_axis_name="s",
                                  num_cores=SC_CORES, num_subcores=SC_SUBCORES)

    @functools.partial(
        pl.kernel, out_type=jax.ShapeDtypeStruct((N, D), F32), mesh=mesh,
        scratch_types=[
            pltpu.VMEM((TOK_BLOCK, D), F32),
            pltpu.VMEM((TOK_BLOCK, D), F32),
            pltpu.VMEM((TOK_BLOCK, PEER_SEL), jnp.int32),
            pltpu.VMEM((TOK_BLOCK, PEER_SEL), F32),
            pltpu.VMEM((D,), F32),
            pltpu.VMEM((NSLOT, PEER_TOPK, D), F32),
            pltpu.VMEM((NSLOT, PEER_TOPK, D), F32),
            pltpu.VMEM((D,), F32),
            pltpu.SemaphoreType.DMA((NSLOT,)),
            pltpu.SemaphoreType.DMA((NSLOT,)),
        ],
        compiler_params=pltpu.CompilerParams(needs_layout_passes=False),
        name="peer_expert_sc",
    )
    def k(x_hbm, hn_hbm, idx_hbm, g_hbm, g2_hbm, u_hbm, v_hbm, out_hbm,
          hn_b, x_b, idx_b, g_b, g2_v, ubuf, vbuf, o_v, usem, vsem):
        wid = lax.axis_index("s") * SC_CORES + lax.axis_index("c")
        tok0 = wid * tpw
        pltpu.sync_copy(g2_hbm.at[wid // workers_per_seq], g2_v)
        lane = lax.iota(jnp.int32, SC_LANES)

        def gather_descs(s, slot):
            t = s // PEER_HEADS
            h0 = pl.multiple_of((s % PEER_HEADS) * PEER_TOPK, PEER_TOPK)
            ids = idx_b.at[t, pl.ds(h0, PEER_TOPK)]
            return (pltpu.make_async_copy(u_hbm.at[ids], ubuf.at[slot], usem.at[slot]),
                    pltpu.make_async_copy(v_hbm.at[ids], vbuf.at[slot], vsem.at[slot]))

        def issue(s, slot):
            du, dv = gather_descs(s, slot)
            du.start()
            dv.start()

        def wait(s, slot):
            du, dv = gather_descs(s, slot)
            du.wait()
            dv.wait()

        def compute(s, slot):
            t = s // PEER_HEADS
            h = s % PEER_HEADS
            h0 = pl.multiple_of(h * PEER_TOPK, PEER_TOPK)

            @pl.when(h == 0)
            def _():
                @pl.loop(0, NCHUNK)
                def _(c):
                    o_v[pl.ds(pl.multiple_of(c * SC_LANES, SC_LANES), SC_LANES)] = jnp.zeros((SC_LANES,), F32)

            def ubody(c, accs):
                c0 = pl.multiple_of(c * SC_LANES, SC_LANES)
                xv = hn_b[t, pl.ds(c0, SC_LANES)]
                return tuple(accs[kk] + xv * ubuf[slot, kk, pl.ds(c0, SC_LANES)] for kk in range(PEER_TOPK))

            accs = lax.fori_loop(0, NCHUNK, ubody,
                                 tuple(jnp.zeros((SC_LANES,), F32) for _ in range(PEER_TOPK)))
            last = jnp.full((SC_LANES,), SC_LANES - 1, jnp.int32)
            r = jnp.zeros((SC_LANES,), F32)
            for kk in range(PEER_TOPK):
                tot = jnp.take_along_axis(plsc.cumsum(accs[kk]), last, axis=0)
                r = jnp.where(lane == kk, tot, r)
            gv = g_b[t, pl.ds(h0, PEER_TOPK)]
            z = 0.7978845608028654 * (r + 0.044715 * (r * r * r))
            w = gv * r / (1.0 + jnp.exp(-2.0 * z))
            wk = [jnp.take_along_axis(w, jnp.full((SC_LANES,), kk, jnp.int32), axis=0) for kk in range(PEER_TOPK)]

            @plsc.parallel_loop(0, NCHUNK, unroll=2)
            def _(c):
                c0 = pl.multiple_of(c * SC_LANES, SC_LANES)
                parts = [wk[kk] * vbuf[slot, kk, pl.ds(c0, SC_LANES)] for kk in range(PEER_TOPK)]
                while len(parts) > 1:
                    parts = [parts[i] + parts[i + 1] for i in range(0, len(parts), 2)]
                plsc.addupdate(o_v.at[pl.ds(c0, SC_LANES)], parts[0])

            @pl.when(h == PEER_HEADS - 1)
            def _():
                @pl.loop(0, NCHUNK)
                def _(c):
                    c0 = pl.multiple_of(c * SC_LANES, SC_LANES)
                    x_b[t, pl.ds(c0, SC_LANES)] = (x_b[t, pl.ds(c0, SC_LANES)]
                                                   + g2_v[pl.ds(c0, SC_LANES)] * o_v[pl.ds(c0, SC_LANES)])

        @pl.loop(0, nblocks)
        def _(blk):
            tok = pl.multiple_of(tok0 + blk * TOK_BLOCK, TOK_BLOCK)
            pltpu.sync_copy(idx_hbm.at[pl.ds(tok, TOK_BLOCK)], idx_b)
            pltpu.sync_copy(g_hbm.at[pl.ds(tok, TOK_BLOCK)], g_b)
            pltpu.sync_copy(hn_hbm.at[pl.ds(tok, TOK_BLOCK)], hn_b)
            pltpu.sync_copy(x_hbm.at[pl.ds(tok, TOK_BLOCK)], x_b)
            for s0 in range(NSLOT - 1):
                issue(s0, s0)

            @pl.loop(0, steps)
            def _(s):
                @pl.when(s + (NSLOT - 1) < steps)
                def _():
                    issue(s + (NSLOT - 1), (s + (NSLOT - 1)) % NSLOT)

                slot = s % NSLOT
                wait(s, slot)
                compute(s, slot)

            pltpu.sync_copy(x_b, out_hbm.at[pl.ds(tok, TOK_BLOCK)])

    return k(x, hn, eidx, g, g2, u_tab, v_tab)


def _peer_layer(x, sh2, sc2, g2, ng, wqt, sk, u_tab, v_tab):
    B, T, D = x.shape
    hn, eidx, g = _peer_route(x, sh2, sc2, ng, wqt, sk)
    y = _peer_expert_sc(x.reshape(B * T, D), hn.reshape(B * T, D), eidx.reshape(B * T, PEER_SEL),
                        g.reshape(B * T, PEER_SEL), g2.reshape(B, D), u_tab, v_tab, tokens_per_seq=T)
    return y.reshape(B, T, D)


def _attn_layer_kernel(*refs, tq, ch, has_cache):
    if has_cache:
        (x_ref, sh_ref, sc_ref, g1_ref, ng_ref, wqkv_ref, qg_ref, kg_ref, bias_ref, wo_ref, kc_ref, vc_ref,
         y_ref, ko_ref, vo_ref, kcat, vcat, qn_s, o_s) = refs
    else:
        (x_ref, sh_ref, sc_ref, g1_ref, ng_ref, wqkv_ref, qg_ref, kg_ref, bias_ref, wo_ref,
         y_ref, ko_ref, vo_ref, kcat, vcat, qn_s, o_s) = refs
    t = pl.program_id(1)
    band = LEFT_ROWS + ch

    if has_cache:
        kcat[0:LEFT_ROWS, :] = kc_ref[0].astype(BF16)
        vcat[0:LEFT_ROWS, :] = vc_ref[0].astype(BF16)
    else:
        @pl.when(t == 0)
        def _():
            kcat[0:LEFT_ROWS, :] = jnp.zeros((LEFT_ROWS, D_MODEL), BF16)
            vcat[0:LEFT_ROWS, :] = jnp.zeros((LEFT_ROWS, D_MODEL), BF16)

    x = x_ref[0]
    ms = jnp.mean(x * x, axis=-1, keepdims=True)
    hn = x * lax.rsqrt(ms + EPS) * ng_ref[...]
    hn = hn * (1.0 + sc_ref[0]) + sh_ref[0]
    qkv = jnp.dot(hn.astype(BF16), wqkv_ref[...], preferred_element_type=F32)
    for h in range(N_HEADS):
        lo = h * HEAD_DIM
        qh = qkv[:, lo:lo + HEAD_DIM]
        kh = qkv[:, D_MODEL + lo:D_MODEL + lo + HEAD_DIM]
        qh = qh * lax.rsqrt(jnp.mean(qh * qh, axis=-1, keepdims=True) + EPS) * qg_ref[...]
        kh = kh * lax.rsqrt(jnp.mean(kh * kh, axis=-1, keepdims=True) + EPS) * kg_ref[...]
        qn_s[:, lo:lo + HEAD_DIM] = qh.astype(BF16)
        ko_ref[0, :, lo:lo + HEAD_DIM] = kh
        kcat[LEFT_ROWS:LEFT_ROWS + tq, lo:lo + HEAD_DIM] = kh.astype(BF16)
    v = qkv[:, 2 * D_MODEL:]
    vo_ref[0] = v
    vcat[LEFT_ROWS:LEFT_ROWS + tq, :] = v.astype(BF16)

    scale = HEAD_DIM ** -0.5
    kpos_rel = lax.broadcasted_iota(jnp.int32, (ch, band), 1)

    def chunk_body(c, _):
        r0 = pl.multiple_of(c * ch, ch)
        if not has_cache:
            valid = kpos_rel + (t * tq + r0) >= LEFT_ROWS
        for h in range(N_HEADS):
            lo = h * HEAD_DIM
            q = qn_s[pl.ds(r0, ch), lo:lo + HEAD_DIM]
            kb = kcat[pl.ds(r0, band), lo:lo + HEAD_DIM]
            vb = vcat[pl.ds(r0, band), lo:lo + HEAD_DIM]
            s = lax.dot_general(q, kb, (((1,), (1,)), ((), ())), preferred_element_type=F32) * scale
            s = s + bias_ref[h, 0:ch, 0:band]
            if not has_cache:
                s = jnp.where(valid, s, NEG_INF)
            m = jnp.max(s, axis=-1, keepdims=True)
            e = jnp.exp(s - m)
            p = e / jnp.sum(e, axis=-1, keepdims=True)
            o_s[pl.ds(r0, ch), lo:lo + HEAD_DIM] = jnp.dot(p.astype(BF16), vb, preferred_element_type=F32).astype(BF16)
        return 0

    lax.fori_loop(0, tq // ch, chunk_body, 0)

    out = jnp.dot(o_s[...], wo_ref[...], preferred_element_type=F32)
    y_ref[0] = x + g1_ref[0] * out

    if not has_cache:
        kcat[0:LEFT_ROWS, :] = kcat[tq:tq + LEFT_ROWS, :]
        vcat[0:LEFT_ROWS, :] = vcat[tq:tq + LEFT_ROWS, :]


def _rel_bias_tile(rel_bias):
    i = jnp.arange(CHUNK)[:, None]
    r = jnp.arange(LEFT_ROWS + CHUNK)[None, :]
    rel = jnp.clip(i + LEFT_ROWS - r, -REL_CLIP, REL_CLIP) + REL_CLIP
    return rel_bias[:, rel]


def _attn_layer(x, sh, sc, g1, ng, wqkv, qg, kg, bias, wo, k_cache=None, v_cache=None):
    B, T, D = x.shape
    has_cache = k_cache is not None
    if has_cache:
        tq, ch = T, T
        keep_blocks = 1
    else:
        tq, ch = 256, CHUNK
        keep_blocks = LEFT_ROWS // tq
    nt = T // tq
    assert nt >= keep_blocks
    tok = lambda b, t: (b, t, 0)
    per_b = lambda b, t: (b, 0, 0)
    c2 = lambda b, t: (0, 0)
    c3 = lambda b, t: (0, 0, 0)
    keep = lambda b, t: (b, jnp.maximum(t - (nt - keep_blocks), 0), 0)
    in_specs = [
        pl.BlockSpec((1, tq, D), tok),
        pl.BlockSpec((1, 1, D), per_b), pl.BlockSpec((1, 1, D), per_b), pl.BlockSpec((1, 1, D), per_b),
        pl.BlockSpec((1, D), c2),
        pl.BlockSpec(wqkv.shape, c2),
        pl.BlockSpec((1, HEAD_DIM), c2), pl.BlockSpec((1, HEAD_DIM), c2),
        pl.BlockSpec(bias.shape, c3),
        pl.BlockSpec(wo.shape, c2),
    ]
    args = [x, sh, sc, g1, ng, wqkv, qg, kg, bias, wo]
    if has_cache:
        in_specs += [pl.BlockSpec((1, LEFT_ROWS, D), per_b), pl.BlockSpec((1, LEFT_ROWS, D), per_b)]
        args += [k_cache, v_cache]
    return pl.pallas_call(
        functools.partial(_attn_layer_kernel, tq=tq, ch=ch, has_cache=has_cache),
        grid=(B, nt),
        in_specs=in_specs,
        out_specs=[pl.BlockSpec((1, tq, D), tok), pl.BlockSpec((1, tq, D), keep), pl.BlockSpec((1, tq, D), keep)],
        out_shape=[jax.ShapeDtypeStruct((B, T, D), F32),
                   jax.ShapeDtypeStruct((B, keep_blocks * tq, D), F32),
                   jax.ShapeDtypeStruct((B, keep_blocks * tq, D), F32)],
        scratch_shapes=[
            pltpu.VMEM((LEFT_ROWS + tq, D), BF16),
            pltpu.VMEM((LEFT_ROWS + tq, D), BF16),
            pltpu.VMEM((tq, D), BF16),
            pltpu.VMEM((tq, D), BF16),
        ],
        compiler_params=pltpu.CompilerParams(
            dimension_semantics=("parallel", "arbitrary"), vmem_limit_bytes=48 * 1024 * 1024),
        name="attn_layer",
    )(*args)


def _gelu_tanh(x):
    return 0.5 * x * (1.0 + jnp.tanh(0.7978845608028654 * (x + 0.044715 * (x * x * x))))


def _lru_layer_kernel(x_ref, sh_ref, sc_ref, g1_ref, ng_ref, win_ref, cw_ref, cb_ref, gw_ref, gb_ref, lam_ref,
                      wout_ref, cs0_ref, h0_ref,
                      y_ref, cs_ref, hl_ref,
                      ext_s, a_s, b_s, hcar_s):
    tt = x_ref.shape[1]
    W = LRU_WIDTH
    t = pl.program_id(1)

    @pl.when(t == 0)
    def _():
        ext_s[0:SUBLANES, :] = cs0_ref[0]
        hcar_s[...] = h0_ref[0]

    x = x_ref[0]
    ms = jnp.mean(x * x, axis=-1, keepdims=True)
    hn = x * lax.rsqrt(ms + EPS) * ng_ref[...]
    hn = hn * (1.0 + sc_ref[0]) + sh_ref[0]
    proj = jnp.dot(hn.astype(BF16), win_ref[...], preferred_element_type=F32)
    gate = _gelu_tanh(proj[:, :W])
    rec = proj[:, W:]
    ext_s[SUBLANES:SUBLANES + tt, :] = rec
    u = cb_ref[...] + cw_ref[3:4, :] * rec
    for k in range(CONV_W - 1):
        u = u + cw_ref[k:k + 1, :] * ext_s[SUBLANES - 3 + k:SUBLANES - 3 + k + tt, :]
    tail = ext_s[tt:tt + SUBLANES, :]
    cs_ref[0] = tail
    ext_s[0:SUBLANES, :] = tail

    sp_in = -lam_ref[...]
    softplus = jnp.maximum(sp_in, 0.0) + jnp.log1p(jnp.exp(-jnp.abs(sp_in)))
    for n in range(LRU_BLOCKS):
        lo = n * LRU_BLOCK_W
        ub = u[:, lo:lo + LRU_BLOCK_W]
        ubb = ub.astype(BF16)
        gr = jnp.dot(ubb, gw_ref[0, n], preferred_element_type=F32) + gb_ref[0:1, lo:lo + LRU_BLOCK_W]
        gi = jnp.dot(ubb, gw_ref[1, n], preferred_element_type=F32) + gb_ref[1:2, lo:lo + LRU_BLOCK_W]
        r = jax.nn.sigmoid(gr)
        i = jax.nn.sigmoid(gi)
        log_a = -RG_C * r * softplus[:, lo:lo + LRU_BLOCK_W]
        a = jnp.exp(log_a)
        a_s[:, lo:lo + LRU_BLOCK_W] = a
        b_s[:, lo:lo + LRU_BLOCK_W] = jnp.sqrt(1.0 - a * a) * i * ub

    def group(gi_, h):
        r0 = pl.multiple_of(gi_ * SUBLANES, SUBLANES)
        a8 = a_s[pl.ds(r0, SUBLANES), :]
        b8 = b_s[pl.ds(r0, SUBLANES), :]
        rows = []
        for j in range(SUBLANES):
            h = a8[j:j + 1, :] * h + b8[j:j + 1, :]
            rows.append(h)
        b_s[pl.ds(r0, SUBLANES), :] = jnp.concatenate(rows, axis=0)
        return h

    h_last = lax.fori_loop(0, tt // SUBLANES, group, hcar_s[...])
    hcar_s[...] = h_last
    hl_ref[0] = h_last
    yv = b_s[...] * gate
    out = jnp.dot(yv.astype(BF16), wout_ref[...], preferred_element_type=F32)
    y_ref[0] = x + g1_ref[0] * out


def _lru_layer(x, sh, sc, g1, ng, w_in, conv_w, conv_b, gate_w, gate_b, lam, w_out, conv_state, h0):
    B, T, D = x.shape
    W = LRU_WIDTH
    tt = min(T, 256)
    cs0 = jnp.pad(conv_state, ((0, 0), (SUBLANES - (CONV_W - 1), 0), (0, 0)))
    tok = lambda b, t: (b, t, 0)
    per_b = lambda b, t: (b, 0, 0)
    c2 = lambda b, t: (0, 0)
    y, cs, hl = pl.pallas_call(
        _lru_layer_kernel,
        grid=(B, T // tt),
        in_specs=[
            pl.BlockSpec((1, tt, D), tok),
            pl.BlockSpec((1, 1, D), per_b), pl.BlockSpec((1, 1, D), per_b), pl.BlockSpec((1, 1, D), per_b),
            pl.BlockSpec((1, D), c2),
            pl.BlockSpec(w_in.shape, c2),
            pl.BlockSpec((CONV_W, W), c2), pl.BlockSpec((1, W), c2),
            pl.BlockSpec(gate_w.shape, lambda b, t: (0, 0, 0, 0)),
            pl.BlockSpec((2, W), c2), pl.BlockSpec((1, W), c2),
            pl.BlockSpec(w_out.shape, c2),
            pl.BlockSpec((1, SUBLANES, W), per_b), pl.BlockSpec((1, 1, W), per_b),
        ],
        out_specs=[pl.BlockSpec((1, tt, D), tok), pl.BlockSpec((1, SUBLANES, W), per_b),
                   pl.BlockSpec((1, 1, W), per_b)],
        out_shape=[jax.ShapeDtypeStruct((B, T, D), F32), jax.ShapeDtypeStruct((B, SUBLANES, W), F32),
                   jax.ShapeDtypeStruct((B, 1, W), F32)],
        scratch_shapes=[
            pltpu.VMEM((tt + SUBLANES, W), F32),
            pltpu.VMEM((tt, W), F32),
            pltpu.VMEM((tt, W), F32),
            pltpu.VMEM((1, W), F32),
        ],
        compiler_params=pltpu.CompilerParams(
            dimension_semantics=("parallel", "arbitrary"), vmem_limit_bytes=48 * 1024 * 1024),
        name="lru_layer",
    )(x, sh, sc, g1, ng, w_in, conv_w, conv_b, gate_w, gate_b, lam, w_out, cs0, h0[:, None, :])
    return y, cs[:, SUBLANES - (CONV_W - 1):], hl[:, 0]


def _ada_kernel(c_ref, w_ref, b_ref, m_ref):
    c = c_ref[...]
    s = c * jax.nn.sigmoid(c)
    m_ref[0] = jnp.dot(s, w_ref[0], preferred_element_type=F32, precision=lax.Precision.HIGHEST) + b_ref[0]


def _ada_all(c_all, ada_w, ada_b):
    L, D, E = ada_w.shape
    R = c_all.shape[0]
    tn = 1536
    return pl.pallas_call(
        _ada_kernel,
        grid=(L, E // tn),
        in_specs=[pl.BlockSpec((R, D), lambda l, j: (0, 0)),
                  pl.BlockSpec((1, D, tn), lambda l, j: (l, 0, j)),
                  pl.BlockSpec((1, 1, tn), lambda l, j: (l, 0, j))],
        out_specs=pl.BlockSpec((1, R, tn), lambda l, j: (l, 0, j)),
        out_shape=jax.ShapeDtypeStruct((L, R, E), F32),
        compiler_params=pltpu.CompilerParams(
            dimension_semantics=("parallel", "parallel"), vmem_limit_bytes=32 * 1024 * 1024),
        name="ada_mod",
    )(c_all, ada_w, ada_b[:, None, :])


class _Chain:
    def __init__(self, x, mods, conv_st, lru_st, k_cache, v_cache, w, prompt):
        self.x, self.mods, self.w, self.prompt = x, mods, w, prompt
        self.conv_st, self.lru_st, self.k_cache, self.v_cache = conv_st, lru_st, k_cache, v_cache
        self.new_conv, self.new_h, self.new_k, self.new_v = [], [], [], []
        self.layer = 0

    def step(self):
        i, w, x = self.layer, self.w, self.x
        B, T, D = x.shape
        sh1, sc1, g1, sh2, sc2, g2 = self.mods[i]
        j = i // 2
        if i % 2 == 0:
            if self.prompt:
                cs0 = jnp.zeros((B, CONV_W - 1, LRU_WIDTH), x.dtype)
                h0 = jnp.zeros((B, LRU_WIDTH), F32)
            else:
                cs0, h0 = self.conv_st[j], self.lru_st[j]
            x, cnew, hnew = _lru_layer(x, sh1, sc1, g1, w['norm_g'][i, 0][None, :], w['lru_w_in'][j],
                                       w['lru_conv_w'][j], w['lru_conv_b'][j][None, :], w['lru_gate_w'][j],
                                       w['lru_gate_b'][j], w['lru_lambda'][j][None, :], w['lru_w_out'][j], cs0, h0)
            self.new_conv.append(cnew)
            self.new_h.append(hnew)
        else:
            if self.prompt:
                kc = vc = None
            else:
                kc = self.k_cache[j].reshape(B, LEFT_ROWS, D)
                vc = self.v_cache[j].reshape(B, LEFT_ROWS, D)
            x, kn, vn = _attn_layer(x, sh1, sc1, g1, w['norm_g'][i, 0][None, :], w['att_w_qkv'][j],
                                    w['att_q_gain'][j][None, :], w['att_k_gain'][j][None, :], w['att_bias'][j],
                                    w['att_w_o'][j], kc, vc)
            self.new_k.append(kn.reshape(B, kn.shape[1], N_HEADS, HEAD_DIM))
            self.new_v.append(vn.reshape(B, vn.shape[1], N_HEADS, HEAD_DIM))
        self.x = _peer_layer(x, sh2, sc2, g2, w['norm_g'][i, 1][None, :], w['peer_wqt'][i], w['peer_sk'][i],
                             w['peer_u'][i], w['peer_v'][i])
        self.layer += 1

    def results(self):
        return (self.x, jnp.stack(self.new_conv), jnp.stack(self.new_h), jnp.stack(self.new_k),
                jnp.stack(self.new_v))


def _run_chains(chains, depth):
    for wave in range(depth + len(chains) - 1):
        for ci, chain in enumerate(chains):
            if 0 <= wave - ci < depth:
                chain.step()


def kernel(x_prompt, x_sample, c_prompt, c_sample, state_conv, state_lru_h, cache_k, cache_v, norm_g, ada_w, ada_b, lru_w_in, lru_conv_w, lru_conv_b, lru_gate_w, lru_gate_b, lru_lambda, lru_w_out, att_w_qkv, att_q_gain, att_k_gain, att_rel_bias, att_w_o, peer_w_query, peer_sub_keys, peer_u, peer_v):
    depth = norm_g.shape[0]
    bp, bs = c_prompt.shape[0], c_sample.shape[0]
    rows = -(-(bp + bs) // SUBLANES) * SUBLANES
    c_all = jnp.concatenate([c_prompt, c_sample, jnp.zeros((rows - bp - bs, D_MODEL), F32)], axis=0)
    m_all = _ada_all(c_all, ada_w, ada_b)

    def mods(lo, n):
        return [[m_all[i, lo:lo + n, None, k * D_MODEL:(k + 1) * D_MODEL] for k in range(6)] for i in range(depth)]

    w = {
        'norm_g': norm_g,
        'lru_w_in': lru_w_in.astype(BF16), 'lru_conv_w': lru_conv_w, 'lru_conv_b': lru_conv_b,
        'lru_gate_w': lru_gate_w.astype(BF16), 'lru_gate_b': lru_gate_b, 'lru_lambda': lru_lambda,
        'lru_w_out': lru_w_out.astype(BF16),
        'att_w_qkv': att_w_qkv.astype(BF16), 'att_q_gain': att_q_gain, 'att_k_gain': att_k_gain,
        'att_bias': jax.vmap(_rel_bias_tile)(att_rel_bias), 'att_w_o': att_w_o.astype(BF16),
        'peer_wqt': jnp.swapaxes(peer_w_query, 1, 2).astype(BF16),
        'peer_sk': peer_sub_keys.astype(BF16).reshape(depth, 2 * PEER_HEADS, N_KEYS, PEER_HALF),
        'peer_u': peer_u, 'peer_v': peer_v,
    }
    prompt_chains = [_Chain(x_prompt[b:b + 1], mods(b, 1), None, None, None, None, w, True) for b in range(bp)]
    sample_chain = _Chain(x_sample, mods(bp, bs), state_conv, state_lru_h, cache_k, cache_v, w, False)
    _run_chains(prompt_chains + [sample_chain], depth)
    y_prompt, p_conv, p_h, p_k, p_v = (
        jnp.concatenate(parts, axis=ax)
        for parts, ax in zip(zip(*(c.results() for c in prompt_chains)), (0, 1, 1, 1, 1)))
    y_sample, s_conv, s_h, s_k, s_v = sample_chain.results()
    return (y_prompt, y_sample, p_conv, p_h, p_k, p_v, s_conv, s_h, s_k, s_v)
```

```python
import functools

import jax
import jax.numpy as jnp
import numpy as np
from jax import lax
from jax.experimental import pallas as pl
from jax.experimental.pallas import tpu as pltpu
from jax.experimental.pallas import tpu_sc as plsc

D_MODEL = 1024
CHUNK = 64
LRU_WIDTH = D_MODEL
LRU_BLOCKS = 4
LRU_BLOCK_W = LRU_WIDTH // LRU_BLOCKS
CONV_W = 4
RG_C = 8.0
N_HEADS = 8
HEAD_DIM = D_MODEL // N_HEADS
LEFT_CHUNKS = 8
LEFT_ROWS = LEFT_CHUNKS * CHUNK
BAND = (LEFT_CHUNKS + 1) * CHUNK
REL_CLIP = 128
N_REL = 2 * REL_CLIP + 1
PEER_HEADS = 8
N_KEYS = 128
N_EXPERTS = N_KEYS * N_KEYS
PEER_TOPK = 16
PEER_QDIM = 256
PEER_HALF = PEER_QDIM // 2
PEER_SEL = PEER_HEADS * PEER_TOPK
EPS = 1e-6
NEG_INF = -1e30
PAST_LEN = 2048

LANES = 128
SUBLANES = 8
BF16 = jnp.bfloat16
F32 = jnp.float32


_CAND_RUNS = [(a, PEER_TOPK // (a + 1)) for a in range(PEER_TOPK)]
_NCAND = sum(n for _, n in _CAND_RUNS)
_NCAND_PAD = -(-_NCAND // SUBLANES) * SUBLANES
_CAND_FLAT = np.full((_NCAND_PAD,), PEER_TOPK * PEER_TOPK, np.int32)
_off = 0
for _a, _n in _CAND_RUNS:
    _CAND_FLAT[_off:_off + _n] = _a * PEER_TOPK + np.arange(_n)
    _off += _n


def _peer_route_kernel(x_ref, sh_ref, sc_ref, ng_ref, wqt_ref, sk_ref, cand_ref,
                       hn_ref, idx_ref, g_ref,
                       q_s, cs_s, ci_s):
    tt = x_ref.shape[1]
    x = x_ref[0]
    ms = jnp.mean(x * x, axis=-1, keepdims=True)
    hn = x * lax.rsqrt(ms + EPS) * ng_ref[...]
    hn = hn * (1.0 + sc_ref[0]) + sh_ref[0]
    hn_ref[0] = hn
    qt = lax.dot_general(wqt_ref[...], hn.astype(BF16), (((1,), (1,)), ((), ())),
                         preferred_element_type=F32)
    for gidx in range(2 * PEER_HEADS):
        q_s[gidx] = qt[gidx * PEER_HALF:(gidx + 1) * PEER_HALF, :]

    key_id = lax.broadcasted_iota(jnp.int32, (N_KEYS, tt), 0)
    slot_id = lax.broadcasted_iota(jnp.int32, (PEER_TOPK, tt), 0)
    cand = cand_ref[:, 0:tt]
    big = PEER_TOPK * PEER_TOPK
    cs_s[_NCAND:_NCAND_PAD, :] = jnp.full((_NCAND_PAD - _NCAND, tt), -jnp.inf, F32)
    ci_s[_NCAND:_NCAND_PAD, :] = jnp.zeros((_NCAND_PAD - _NCAND, tt), jnp.int32)

    def head_body(h, _):
        s = [jnp.dot(sk_ref[2 * h + p], q_s[2 * h + p].astype(BF16), preferred_element_type=F32)
             for p in range(2)]
        ts = [jnp.zeros((PEER_TOPK, tt), F32) for _ in range(2)]
        ti = [jnp.zeros((PEER_TOPK, tt), jnp.int32) for _ in range(2)]
        for j in range(PEER_TOPK):
            for p in range(2):
                m = jnp.max(s[p], axis=0, keepdims=True)
                am = jnp.min(jnp.where(s[p] == m, key_id, N_KEYS), axis=0, keepdims=True)
                ts[p] = jnp.where(slot_id == j, m, ts[p])
                ti[p] = jnp.where(slot_id == j, am, ti[p])
                s[p] = jnp.where(key_id == am, -jnp.inf, s[p])
        off = 0
        for a, n in _CAND_RUNS:
            cs_s[off:off + n, :] = ts[0][a:a + 1, :] + ts[1][0:n, :]
            ci_s[off:off + n, :] = ti[0][a:a + 1, :] * N_KEYS + ti[1][0:n, :]
            off += n
        cs = cs_s[...]
        ci = ci_s[...]
        fs = jnp.zeros((PEER_TOPK, tt), F32)
        ei = jnp.zeros((PEER_TOPK, tt), jnp.int32)
        for k in range(PEER_TOPK):
            m = jnp.max(cs, axis=0, keepdims=True)
            cmin = jnp.min(jnp.where(cs == m, cand, big), axis=0, keepdims=True)
            sel = cand == cmin
            e = jnp.sum(jnp.where(sel, ci, 0), axis=0, keepdims=True)
            fs = jnp.where(slot_id == k, m, fs)
            ei = jnp.where(slot_id == k, e, ei)
            cs = jnp.where(sel, -jnp.inf, cs)
        ex = jnp.exp(fs - fs[0:1, :])
        g = ex / jnp.sum(ex, axis=0, keepdims=True)
        row0 = pl.multiple_of(h * PEER_TOPK, PEER_TOPK)
        g_ref[0, pl.ds(row0, PEER_TOPK), :] = g
        idx_ref[0, pl.ds(row0, PEER_TOPK), :] = ei
        return 0

    lax.fori_loop(0, PEER_HEADS, head_body, 0)


def _peer_route(x, sh, sc, ng, wqt, sk):
    B, T, D = x.shape
    tt = min(T, 128)
    grid = (B, T // tt)
    tok = lambda b, t: (b, t, 0)
    tok_t = lambda b, t: (b, 0, t)
    per_b = lambda b, t: (b, 0, 0)
    cand_tab = jnp.asarray(np.broadcast_to(_CAND_FLAT[:, None], (_NCAND_PAD, LANES)))
    hn, idx_t, g_t = pl.pallas_call(
        _peer_route_kernel,
        grid=grid,
        in_specs=[
            pl.BlockSpec((1, tt, D), tok),
            pl.BlockSpec((1, 1, D), per_b),
            pl.BlockSpec((1, 1, D), per_b),
            pl.BlockSpec((1, D), lambda b, t: (0, 0)),
            pl.BlockSpec(wqt.shape, lambda b, t: (0, 0)),
            pl.BlockSpec(sk.shape, lambda b, t: (0, 0, 0)),
            pl.BlockSpec(cand_tab.shape, lambda b, t: (0, 0)),
        ],
        out_specs=[
            pl.BlockSpec((1, tt, D), tok),
            pl.BlockSpec((1, PEER_SEL, tt), tok_t),
            pl.BlockSpec((1, PEER_SEL, tt), tok_t),
        ],
        out_shape=[
            jax.ShapeDtypeStruct((B, T, D), F32),
            jax.ShapeDtypeStruct((B, PEER_SEL, T), jnp.int32),
            jax.ShapeDtypeStruct((B, PEER_SEL, T), F32),
        ],
        scratch_shapes=[
            pltpu.VMEM((2 * PEER_HEADS, PEER_HALF, tt), F32),
            pltpu.VMEM((_NCAND_PAD, tt), F32),
            pltpu.VMEM((_NCAND_PAD, tt), jnp.int32),
        ],
        compiler_params=pltpu.CompilerParams(
            dimension_semantics=("parallel", "parallel"), vmem_limit_bytes=48 * 1024 * 1024),
        name="peer_route",
    )(x, sh, sc, ng, wqt, sk, cand_tab)
    return hn, jnp.swapaxes(idx_t, 1, 2), jnp.swapaxes(g_t, 1, 2)


SC_CORES = 2
SC_SUBCORES = 16
SC_LANES = 16
SC_WORKERS = SC_CORES * SC_SUBCORES
NCHUNK = D_MODEL // SC_LANES
TOK_BLOCK = 8
NSLOT = 3


def _peer_expert_sc(x, hn, eidx, g, g2, u_tab, v_tab, tokens_per_seq, n_tokens):
    N, D = n_tokens, x.shape[1]
    tpw = N // SC_WORKERS
    assert tpw * SC_WORKERS == N and tpw % TOK_BLOCK == 0
    assert g2.shape[0] == 1 or tokens_per_seq % tpw == 0
    workers_per_seq = max(tokens_per_seq // tpw, 1) if g2.shape[0] > 1 else SC_WORKERS
    nblocks = tpw // TOK_BLOCK
    steps = TOK_BLOCK * PEER_HEADS
    mesh = plsc.VectorSubcoreMesh(core_axis_name="c", subcore_axis_name="s",
                                  num_cores=SC_CORES, num_subcores=SC_SUBCORES)

    @functools.partial(
        pl.kernel, out_type=jax.ShapeDtypeStruct((N, D), F32), mesh=mesh,
        scratch_types=[
            pltpu.VMEM((TOK_BLOCK, D), F32),
            pltpu.VMEM((TOK_BLOCK, D), F32),
            pltpu.VMEM((TOK_BLOCK, PEER_SEL), jnp.int32),
            pltpu.VMEM((TOK_BLOCK, PEER_SEL), F32),
            pltpu.VMEM((D,), F32),
            pltpu.VMEM((NSLOT, PEER_TOPK, D), F32),
            pltpu.VMEM((NSLOT, PEER_TOPK, D), F32),
            pltpu.VMEM((D,), F32),
            pltpu.SemaphoreType.DMA((NSLOT,)),
            pltpu.SemaphoreType.DMA((NSLOT,)),
        ],
        compiler_params=pltpu.CompilerParams(needs_layout_passes=False),
        name="peer_expert_sc",
    )
    def k(x_hbm, hn_hbm, idx_hbm, g_hbm, g2_hbm, u_hbm, v_hbm, out_hbm,
          hn_b, x_b, idx_b, g_b, g2_v, ubuf, vbuf, o_v, usem, vsem):
        wid = lax.axis_index("s") * SC_CORES + lax.axis_index("c")
        tok0 = wid * tpw
        pltpu.sync_copy(g2_hbm.at[wid // workers_per_seq], g2_v)
        lane = lax.iota(jnp.int32, SC_LANES)

        def gather_descs(s, slot):
            t = s // PEER_HEADS
            h0 = pl.multiple_of((s % PEER_HEADS) * PEER_TOPK, PEER_TOPK)
            ids = idx_b.at[t, pl.ds(h0, PEER_TOPK)]
            return (pltpu.make_async_copy(u_hbm.at[ids], ubuf.at[slot], usem.at[slot]),
                    pltpu.make_async_copy(v_hbm.at[ids], vbuf.at[slot], vsem.at[slot]))

        def issue(s, slot):
            du, dv = gather_descs(s, slot)
            du.start()
            dv.start()

        def wait(s, slot):
            du, dv = gather_descs(s, slot)
            du.wait()
            dv.wait()

        def compute(s, slot):
            t = s // PEER_HEADS
            h = s % PEER_HEADS
            h0 = pl.multiple_of(h * PEER_TOPK, PEER_TOPK)

            @pl.when(h == 0)
            def _():
                @pl.loop(0, NCHUNK)
                def _(c):
                    o_v[pl.ds(pl.multiple_of(c * SC_LANES, SC_LANES), SC_LANES)] = jnp.zeros((SC_LANES,), F32)

            def ubody(c, accs):
                c0 = pl.multiple_of(c * SC_LANES, SC_LANES)
                xv = hn_b[t, pl.ds(c0, SC_LANES)]
                return tuple(accs[kk] + xv * ubuf[slot, kk, pl.ds(c0, SC_LANES)] for kk in range(PEER_TOPK))

            accs = lax.fori_loop(0, NCHUNK, ubody,
                                 tuple(jnp.zeros((SC_LANES,), F32) for _ in range(PEER_TOPK)))
            last = jnp.full((SC_LANES,), SC_LANES - 1, jnp.int32)
            r = jnp.zeros((SC_LANES,), F32)
            for kk in range(PEER_TOPK):
                tot = jnp.take_along_axis(plsc.cumsum(accs[kk]), last, axis=0)
                r = jnp.where(lane == kk, tot, r)
            gv = g_b[t, pl.ds(h0, PEER_TOPK)]
            z = 0.7978845608028654 * (r + 0.044715 * (r * r * r))
            w = gv * r / (1.0 + jnp.exp(-2.0 * z))
            wk = [jnp.take_along_axis(w, jnp.full((SC_LANES,), kk, jnp.int32), axis=0) for kk in range(PEER_TOPK)]

            @plsc.parallel_loop(0, NCHUNK, unroll=2)
            def _(c):
                c0 = pl.multiple_of(c * SC_LANES, SC_LANES)
                parts = [wk[kk] * vbuf[slot, kk, pl.ds(c0, SC_LANES)] for kk in range(PEER_TOPK)]
                while len(parts) > 1:
                    parts = [parts[i] + parts[i + 1] for i in range(0, len(parts), 2)]
                plsc.addupdate(o_v.at[pl.ds(c0, SC_LANES)], parts[0])

            @pl.when(h == PEER_HEADS - 1)
            def _():
                @pl.loop(0, NCHUNK)
                def _(c):
                    c0 = pl.multiple_of(c * SC_LANES, SC_LANES)
                    x_b[t, pl.ds(c0, SC_LANES)] = (x_b[t, pl.ds(c0, SC_LANES)]
                                                   + g2_v[pl.ds(c0, SC_LANES)] * o_v[pl.ds(c0, SC_LANES)])

        @pl.loop(0, nblocks)
        def _(blk):
            tok = pl.multiple_of(tok0 + blk * TOK_BLOCK, TOK_BLOCK)
            pltpu.sync_copy(idx_hbm.at[pl.ds(tok, TOK_BLOCK)], idx_b)
            pltpu.sync_copy(g_hbm.at[pl.ds(tok, TOK_BLOCK)], g_b)
            pltpu.sync_copy(hn_hbm.at[pl.ds(tok, TOK_BLOCK)], hn_b)
            pltpu.sync_copy(x_hbm.at[pl.ds(tok, TOK_BLOCK)], x_b)
            for s0 in range(NSLOT - 1):
                issue(s0, s0)

            @pl.loop(0, steps)
            def _(s):
                @pl.when(s + (NSLOT - 1) < steps)
                def _():
                    issue(s + (NSLOT - 1), (s + (NSLOT - 1)) % NSLOT)

                slot = s % NSLOT
                wait(s, slot)
                compute(s, slot)

            pltpu.sync_copy(x_b, out_hbm.at[pl.ds(tok, TOK_BLOCK)])

    return k(x, hn, eidx, g, g2, u_tab, v_tab)


TC_TOK_BLOCK = 64
TC_ISSUE_UNROLL = 32


def _peer_expert_tc_kernel(idx_hbm, x_ref, hn_ref, g_ref, g2_ref, u_hbm, v_hbm, out_ref,
                           idx_s, ubuf, vbuf, sem, isem, *, first_block):
    tb = x_ref.shape[0]
    row0 = (pl.program_id(0) + first_block) * tb
    cp = pltpu.make_async_copy(idx_hbm.at[pl.ds(row0, tb)], idx_s, isem)
    cp.start()
    cp.wait()

    def row_copies(e, j, slot):
        return (pltpu.make_async_copy(u_hbm.at[pl.ds(e, 1)], ubuf.at[slot, pl.ds(j, 1)], sem.at[0, slot]),
                pltpu.make_async_copy(v_hbm.at[pl.ds(e, 1)], vbuf.at[slot, pl.ds(j, 1)], sem.at[1, slot]))

    def issue(t, slot):
        def body(jb, _):
            for jj in range(TC_ISSUE_UNROLL):
                j = jb * TC_ISSUE_UNROLL + jj
                cu, cv = row_copies(idx_s[t, j], j, slot)
                cu.start()
                cv.start()
            return 0

        lax.fori_loop(0, PEER_SEL // TC_ISSUE_UNROLL, body, 0)

    def wait(slot):
        pltpu.make_async_copy(u_hbm.at[pl.ds(0, PEER_SEL)], ubuf.at[slot], sem.at[0, slot]).wait()
        pltpu.make_async_copy(v_hbm.at[pl.ds(0, PEER_SEL)], vbuf.at[slot], sem.at[1, slot]).wait()

    issue(0, 0)
    r2 = lax.broadcasted_iota(jnp.int32, (PEER_SEL, LANES), 0)
    c2 = lax.broadcasted_iota(jnp.int32, (PEER_SEL, LANES), 1)
    diag = r2 == c2

    def tok(t, _):
        slot = t % 2

        @pl.when(t + 1 < tb)
        def _():
            issue(t + 1, 1 - slot)

        wait(slot)
        xr = hn_ref[pl.ds(t, 1), :]
        prod = ubuf[slot] * xr
        p = prod[:, 0:LANES]
        for c in range(1, D_MODEL // LANES):
            p = p + prod[:, c * LANES:(c + 1) * LANES]
        act = jnp.sum(p, axis=-1, keepdims=True)
        grow = g_ref[pl.ds(t, 1), :]
        gcol = jnp.sum(jnp.where(diag, jnp.broadcast_to(grow, (PEER_SEL, LANES)), 0.0),
                       axis=-1, keepdims=True)
        wgt = gcol * _gelu_tanh(act)
        o = jnp.sum(vbuf[slot] * wgt, axis=0, keepdims=True)
        out_ref[pl.ds(t, 1), :] = x_ref[pl.ds(t, 1), :] + g2_ref[...] * o
        return 0

    lax.fori_loop(0, tb, tok, 0)


def _peer_expert_tc(x, hn, eidx, g, g2, u_tab, v_tab, first_token):
    N, D = x.shape
    tb = TC_TOK_BLOCK
    n_out = N - first_token
    assert first_token % tb == 0 and n_out % tb == 0
    first_block = first_token // tb
    tok = lambda t: (t + first_block, 0)
    return pl.pallas_call(
        functools.partial(_peer_expert_tc_kernel, first_block=first_block),
        grid=(n_out // tb,),
        in_specs=[
            pl.BlockSpec(memory_space=pl.ANY),
            pl.BlockSpec((tb, D), tok),
            pl.BlockSpec((tb, D), tok),
            pl.BlockSpec((tb, PEER_SEL), tok),
            pl.BlockSpec((1, D), lambda t: (0, 0)),
            pl.BlockSpec(memory_space=pl.ANY),
            pl.BlockSpec(memory_space=pl.ANY),
        ],
        out_specs=pl.BlockSpec((tb, D), lambda t: (t, 0)),
        out_shape=jax.ShapeDtypeStruct((n_out, D), F32),
        scratch_shapes=[
            pltpu.SMEM((tb, PEER_SEL), jnp.int32),
            pltpu.VMEM((2, PEER_SEL, D), F32),
            pltpu.VMEM((2, PEER_SEL, D), F32),
            pltpu.SemaphoreType.DMA((2, 2)),
            pltpu.SemaphoreType.DMA,
        ],
        compiler_params=pltpu.CompilerParams(
            dimension_semantics=("arbitrary",), vmem_limit_bytes=32 * 1024 * 1024,
            disable_bounds_checks=True),
        name="peer_expert_tc",
    )(eidx, x, hn, g, g2, u_tab, v_tab)


TC_SHARE_TOKENS = 1280


def _peer_layer(x, sh2, sc2, g2, ng, wqt, sk, u_tab, v_tab):
    B, T, D = x.shape
    hn, eidx, g = _peer_route(x, sh2, sc2, ng, wqt, sk)
    N = B * T
    flat = (x.reshape(N, D), hn.reshape(N, D), eidx.reshape(N, PEER_SEL), g.reshape(N, PEER_SEL),
            g2.reshape(B, D), u_tab, v_tab)
    n_tc = TC_SHARE_TOKENS if (B == 1 and N >= 4 * TC_SHARE_TOKENS) else 0
    y = _peer_expert_sc(*flat, tokens_per_seq=T, n_tokens=N - n_tc)
    if n_tc:
        y = jnp.concatenate([y, _peer_expert_tc(*flat, first_token=N - n_tc)], axis=0)
    return y.reshape(B, T, D)


def _attn_layer_kernel(*refs, tq, ch, has_cache):
    if has_cache:
        (x_ref, sh_ref, sc_ref, g1_ref, ng_ref, wqkv_ref, qg_ref, kg_ref, bias_ref, wo_ref, kc_ref, vc_ref,
         y_ref, ko_ref, vo_ref, kcat, vcat, qn_s, o_s) = refs
    else:
        (x_ref, sh_ref, sc_ref, g1_ref, ng_ref, wqkv_ref, qg_ref, kg_ref, bias_ref, wo_ref,
         y_ref, ko_ref, vo_ref, kcat, vcat, qn_s, o_s) = refs
    t = pl.program_id(1)
    band = LEFT_ROWS + ch

    if has_cache:
        kcat[0:LEFT_ROWS, :] = kc_ref[0].astype(BF16)
        vcat[0:LEFT_ROWS, :] = vc_ref[0].astype(BF16)
    else:
        @pl.when(t == 0)
        def _():
            kcat[0:LEFT_ROWS, :] = jnp.zeros((LEFT_ROWS, D_MODEL), BF16)
            vcat[0:LEFT_ROWS, :] = jnp.zeros((LEFT_ROWS, D_MODEL), BF16)

    x = x_ref[0]
    ms = jnp.mean(x * x, axis=-1, keepdims=True)
    hn = x * lax.rsqrt(ms + EPS) * ng_ref[...]
    hn = hn * (1.0 + sc_ref[0]) + sh_ref[0]
    qkv = jnp.dot(hn.astype(BF16), wqkv_ref[...], preferred_element_type=F32)
    for h in range(N_HEADS):
        lo = h * HEAD_DIM
        qh = qkv[:, lo:lo + HEAD_DIM]
        kh = qkv[:, D_MODEL + lo:D_MODEL + lo + HEAD_DIM]
        qh = qh * lax.rsqrt(jnp.mean(qh * qh, axis=-1, keepdims=True) + EPS) * qg_ref[...]
        kh = kh * lax.rsqrt(jnp.mean(kh * kh, axis=-1, keepdims=True) + EPS) * kg_ref[...]
        qn_s[:, lo:lo + HEAD_DIM] = qh.astype(BF16)
        ko_ref[0, :, lo:lo + HEAD_DIM] = kh
        kcat[LEFT_ROWS:LEFT_ROWS + tq, lo:lo + HEAD_DIM] = kh.astype(BF16)
    v = qkv[:, 2 * D_MODEL:]
    vo_ref[0] = v
    vcat[LEFT_ROWS:LEFT_ROWS + tq, :] = v.astype(BF16)

    scale = HEAD_DIM ** -0.5
    kpos_rel = lax.broadcasted_iota(jnp.int32, (ch, band), 1)

    def chunk_body(c, _):
        r0 = pl.multiple_of(c * ch, ch)
        if not has_cache:
            valid = kpos_rel + (t * tq + r0) >= LEFT_ROWS
        for h in range(N_HEADS):
            lo = h * HEAD_DIM
            q = qn_s[pl.ds(r0, ch), lo:lo + HEAD_DIM]
            kb = kcat[pl.ds(r0, band), lo:lo + HEAD_DIM]
            vb = vcat[pl.ds(r0, band), lo:lo + HEAD_DIM]
            s = lax.dot_general(q, kb, (((1,), (1,)), ((), ())), preferred_element_type=F32) * scale
            s = s + bias_ref[h, 0:ch, 0:band]
            if not has_cache:
                s = jnp.where(valid, s, NEG_INF)
            m = jnp.max(s, axis=-1, keepdims=True)
            e = jnp.exp(s - m)
            p = e / jnp.sum(e, axis=-1, keepdims=True)
            o_s[pl.ds(r0, ch), lo:lo + HEAD_DIM] = jnp.dot(p.astype(BF16), vb, preferred_element_type=F32).astype(BF16)
        return 0

    lax.fori_loop(0, tq // ch, chunk_body, 0)

    out = jnp.dot(o_s[...], wo_ref[...], preferred_element_type=F32)
    y_ref[0] = x + g1_ref[0] * out

    if not has_cache:
        kcat[0:LEFT_ROWS, :] = kcat[tq:tq + LEFT_ROWS, :]
        vcat[0:LEFT_ROWS, :] = vcat[tq:tq + LEFT_ROWS, :]


def _rel_bias_tile(rel_bias):
    i = jnp.arange(CHUNK)[:, None]
    r = jnp.arange(LEFT_ROWS + CHUNK)[None, :]
    rel = jnp.clip(i + LEFT_ROWS - r, -REL_CLIP, REL_CLIP) + REL_CLIP
    return rel_bias[:, rel]


def _attn_layer(x, sh, sc, g1, ng, wqkv, qg, kg, bias, wo, k_cache=None, v_cache=None):
    B, T, D = x.shape
    has_cache = k_cache is not None
    if has_cache:
        tq, ch = T, T
        keep_blocks = 1
    else:
        tq, ch = 256, CHUNK
        keep_blocks = LEFT_ROWS // tq
    nt = T // tq
    assert nt >= keep_blocks
    tok = lambda b, t: (b, t, 0)
    per_b = lambda b, t: (b, 0, 0)
    c2 = lambda b, t: (0, 0)
    c3 = lambda b, t: (0, 0, 0)
    keep = lambda b, t: (b, jnp.maximum(t - (nt - keep_blocks), 0), 0)
    in_specs = [
        pl.BlockSpec((1, tq, D), tok),
        pl.BlockSpec((1, 1, D), per_b), pl.BlockSpec((1, 1, D), per_b), pl.BlockSpec((1, 1, D), per_b),
        pl.BlockSpec((1, D), c2),
        pl.BlockSpec(wqkv.shape, c2),
        pl.BlockSpec((1, HEAD_DIM), c2), pl.BlockSpec((1, HEAD_DIM), c2),
        pl.BlockSpec(bias.shape, c3),
        pl.BlockSpec(wo.shape, c2),
    ]
    args = [x, sh, sc, g1, ng, wqkv, qg, kg, bias, wo]
    if has_cache:
        in_specs += [pl.BlockSpec((1, LEFT_ROWS, D), per_b), pl.BlockSpec((1, LEFT_ROWS, D), per_b)]
        args += [k_cache, v_cache]
    return pl.pallas_call(
        functools.partial(_attn_layer_kernel, tq=tq, ch=ch, has_cache=has_cache),
        grid=(B, nt),
        in_specs=in_specs,
        out_specs=[pl.BlockSpec((1, tq, D), tok), pl.BlockSpec((1, tq, D), keep), pl.BlockSpec((1, tq, D), keep)],
        out_shape=[jax.ShapeDtypeStruct((B, T, D), F32),
                   jax.ShapeDtypeStruct((B, keep_blocks * tq, D), F32),
                   jax.ShapeDtypeStruct((B, keep_blocks * tq, D), F32)],
        scratch_shapes=[
            pltpu.VMEM((LEFT_ROWS + tq, D), BF16),
            pltpu.VMEM((LEFT_ROWS + tq, D), BF16),
            pltpu.VMEM((tq, D), BF16),
            pltpu.VMEM((tq, D), BF16),
        ],
        compiler_params=pltpu.CompilerParams(
            dimension_semantics=("parallel", "arbitrary"), vmem_limit_bytes=48 * 1024 * 1024),
        name="attn_layer",
    )(*args)


def _gelu_tanh(x):
    return 0.5 * x * (1.0 + jnp.tanh(0.7978845608028654 * (x + 0.044715 * (x * x * x))))


def _lru_layer_kernel(x_ref, sh_ref, sc_ref, g1_ref, ng_ref, win_ref, cw_ref, cb_ref, gw_ref, gb_ref, lam_ref,
                      wout_ref, cs0_ref, h0_ref,
                      y_ref, cs_ref, hl_ref,
                      ext_s, a_s, b_s, hcar_s):
    tt = x_ref.shape[1]
    W = LRU_WIDTH
    t = pl.program_id(1)

    @pl.when(t == 0)
    def _():
        ext_s[0:SUBLANES, :] = cs0_ref[0]
        hcar_s[...] = h0_ref[0]

    x = x_ref[0]
    ms = jnp.mean(x * x, axis=-1, keepdims=True)
    hn = x * lax.rsqrt(ms + EPS) * ng_ref[...]
    hn = hn * (1.0 + sc_ref[0]) + sh_ref[0]
    proj = jnp.dot(hn.astype(BF16), win_ref[...], preferred_element_type=F32)
    gate = _gelu_tanh(proj[:, :W])
    rec = proj[:, W:]
    ext_s[SUBLANES:SUBLANES + tt, :] = rec
    u = cb_ref[...] + cw_ref[3:4, :] * rec
    for k in range(CONV_W - 1):
        u = u + cw_ref[k:k + 1, :] * ext_s[SUBLANES - 3 + k:SUBLANES - 3 + k + tt, :]
    tail = ext_s[tt:tt + SUBLANES, :]
    cs_ref[0] = tail
    ext_s[0:SUBLANES, :] = tail

    sp_in = -lam_ref[...]
    softplus = jnp.maximum(sp_in, 0.0) + jnp.log1p(jnp.exp(-jnp.abs(sp_in)))
    for n in range(LRU_BLOCKS):
        lo = n * LRU_BLOCK_W
        ub = u[:, lo:lo + LRU_BLOCK_W]
        ubb = ub.astype(BF16)
        gr = jnp.dot(ubb, gw_ref[0, n], preferred_element_type=F32) + gb_ref[0:1, lo:lo + LRU_BLOCK_W]
        gi = jnp.dot(ubb, gw_ref[1, n], preferred_element_type=F32) + gb_ref[1:2, lo:lo + LRU_BLOCK_W]
        r = jax.nn.sigmoid(gr)
        i = jax.nn.sigmoid(gi)
        log_a = -RG_C * r * softplus[:, lo:lo + LRU_BLOCK_W]
        a = jnp.exp(log_a)
        a_s[:, lo:lo + LRU_BLOCK_W] = a
        b_s[:, lo:lo + LRU_BLOCK_W] = jnp.sqrt(1.0 - a * a) * i * ub

    def group(gi_, h):
        r0 = pl.multiple_of(gi_ * SUBLANES, SUBLANES)
        a8 = a_s[pl.ds(r0, SUBLANES), :]
        b8 = b_s[pl.ds(r0, SUBLANES), :]
        rows = []
        for j in range(SUBLANES):
            h = a8[j:j + 1, :] * h + b8[j:j + 1, :]
            rows.append(h)
        b_s[pl.ds(r0, SUBLANES), :] = jnp.concatenate(rows, axis=0)
        return h

    h_last = lax.fori_loop(0, tt // SUBLANES, group, hcar_s[...])
    hcar_s[...] = h_last
    hl_ref[0] = h_last
    yv = b_s[...] * gate
    out = jnp.dot(yv.astype(BF16), wout_ref[...], preferred_element_type=F32)
    y_ref[0] = x + g1_ref[0] * out


def _lru_layer(x, sh, sc, g1, ng, w_in, conv_w, conv_b, gate_w, gate_b, lam, w_out, conv_state, h0):
    B, T, D = x.shape
    W = LRU_WIDTH
    tt = min(T, 256)
    cs0 = jnp.pad(conv_state, ((0, 0), (SUBLANES - (CONV_W - 1), 0), (0, 0)))
    tok = lambda b, t: (b, t, 0)
    per_b = lambda b, t: (b, 0, 0)
    c2 = lambda b, t: (0, 0)
    y, cs, hl = pl.pallas_call(
        _lru_layer_kernel,
        grid=(B, T // tt),
        in_specs=[
            pl.BlockSpec((1, tt, D), tok),
            pl.BlockSpec((1, 1, D), per_b), pl.BlockSpec((1, 1, D), per_b), pl.BlockSpec((1, 1, D), per_b),
            pl.BlockSpec((1, D), c2),
            pl.BlockSpec(w_in.shape, c2),
            pl.BlockSpec((CONV_W, W), c2), pl.BlockSpec((1, W), c2),
            pl.BlockSpec(gate_w.shape, lambda b, t: (0, 0, 0, 0)),
            pl.BlockSpec((2, W), c2), pl.BlockSpec((1, W), c2),
            pl.BlockSpec(w_out.shape, c2),
            pl.BlockSpec((1, SUBLANES, W), per_b), pl.BlockSpec((1, 1, W), per_b),
        ],
        out_specs=[pl.BlockSpec((1, tt, D), tok), pl.BlockSpec((1, SUBLANES, W), per_b),
                   pl.BlockSpec((1, 1, W), per_b)],
        out_shape=[jax.ShapeDtypeStruct((B, T, D), F32), jax.ShapeDtypeStruct((B, SUBLANES, W), F32),
                   jax.ShapeDtypeStruct((B, 1, W), F32)],
        scratch_shapes=[
            pltpu.VMEM((tt + SUBLANES, W), F32),
            pltpu.VMEM((tt, W), F32),
            pltpu.VMEM((tt, W), F32),
            pltpu.VMEM((1, W), F32),
        ],
        compiler_params=pltpu.CompilerParams(
            dimension_semantics=("parallel", "arbitrary"), vmem_limit_bytes=48 * 1024 * 1024),
        name="lru_layer",
    )(x, sh, sc, g1, ng, w_in, conv_w, conv_b, gate_w, gate_b, lam, w_out, cs0, h0[:, None, :])
    return y, cs[:, SUBLANES - (CONV_W - 1):], hl[:, 0]


def _ada_kernel(c_ref, w_ref, b_ref, m_ref):
    c = c_ref[...]
    s = c * jax.nn.sigmoid(c)
    m_ref[0] = jnp.dot(s, w_ref[0], preferred_element_type=F32, precision=lax.Precision.HIGHEST) + b_ref[0]


def _ada_all(c_all, ada_w, ada_b):
    L, D, E = ada_w.shape
    R = c_all.shape[0]
    tn = 1536
    return pl.pallas_call(
        _ada_kernel,
        grid=(L, E // tn),
        in_specs=[pl.BlockSpec((R, D), lambda l, j: (0, 0)),
                  pl.BlockSpec((1, D, tn), lambda l, j: (l, 0, j)),
                  pl.BlockSpec((1, 1, tn), lambda l, j: (l, 0, j))],
        out_specs=pl.BlockSpec((1, R, tn), lambda l, j: (l, 0, j)),
        out_shape=jax.ShapeDtypeStruct((L, R, E), F32),
        compiler_params=pltpu.CompilerParams(
            dimension_semantics=("parallel", "parallel"), vmem_limit_bytes=32 * 1024 * 1024),
        name="ada_mod",
    )(c_all, ada_w, ada_b[:, None, :])


class _Chain:
    def __init__(self, x, mods, conv_st, lru_st, k_cache, v_cache, w, prompt):
        self.x, self.mods, self.w, self.prompt = x, mods, w, prompt
        self.conv_st, self.lru_st, self.k_cache, self.v_cache = conv_st, lru_st, k_cache, v_cache
        self.new_conv, self.new_h, self.new_k, self.new_v = [], [], [], []
        self.layer = 0

    def step(self):
        i, w, x = self.layer, self.w, self.x
        B, T, D = x.shape
        sh1, sc1, g1, sh2, sc2, g2 = self.mods[i]
        j = i // 2
        if i % 2 == 0:
            if self.prompt:
                cs0 = jnp.zeros((B, CONV_W - 1, LRU_WIDTH), x.dtype)
                h0 = jnp.zeros((B, LRU_WIDTH), F32)
            else:
                cs0, h0 = self.conv_st[j], self.lru_st[j]
            x, cnew, hnew = _lru_layer(x, sh1, sc1, g1, w['norm_g'][i, 0][None, :], w['lru_w_in'][j],
                                       w['lru_conv_w'][j], w['lru_conv_b'][j][None, :], w['lru_gate_w'][j],
                                       w['lru_gate_b'][j], w['lru_lambda'][j][None, :], w['lru_w_out'][j], cs0, h0)
            self.new_conv.append(cnew)
            self.new_h.append(hnew)
        else:
            if self.prompt:
                kc = vc = None
            else:
                kc = self.k_cache[j].reshape(B, LEFT_ROWS, D)
                vc = self.v_cache[j].reshape(B, LEFT_ROWS, D)
            x, kn, vn = _attn_layer(x, sh1, sc1, g1, w['norm_g'][i, 0][None, :], w['att_w_qkv'][j],
                                    w['att_q_gain'][j][None, :], w['att_k_gain'][j][None, :], w['att_bias'][j],
                                    w['att_w_o'][j], kc, vc)
            self.new_k.append(kn.reshape(B, kn.shape[1], N_HEADS, HEAD_DIM))
            self.new_v.append(vn.reshape(B, vn.shape[1], N_HEADS, HEAD_DIM))
        self.x = _peer_layer(x, sh2, sc2, g2, w['norm_g'][i, 1][None, :], w['peer_wqt'][i], w['peer_sk'][i],
                             w['peer_u'][i], w['peer_v'][i])
        self.layer += 1

    def results(self):
        return (self.x, jnp.stack(self.new_conv), jnp.stack(self.new_h), jnp.stack(self.new_k),
                jnp.stack(self.new_v))


def _run_chains(chains, depth):
    for wave in range(depth + len(chains) - 1):
        for ci, chain in enumerate(chains):
            if 0 <= wave - ci < depth:
                chain.step()


def kernel(x_prompt, x_sample, c_prompt, c_sample, state_conv, state_lru_h, cache_k, cache_v, norm_g, ada_w, ada_b, lru_w_in, lru_conv_w, lru_conv_b, lru_gate_w, lru_gate_b, lru_lambda, lru_w_out, att_w_qkv, att_q_gain, att_k_gain, att_rel_bias, att_w_o, peer_w_query, peer_sub_keys, peer_u, peer_v):
    depth = norm_g.shape[0]
    bp, bs = c_prompt.shape[0], c_sample.shape[0]
    rows = -(-(bp + bs) // SUBLANES) * SUBLANES
    c_all = jnp.concatenate([c_prompt, c_sample, jnp.zeros((rows - bp - bs, D_MODEL), F32)], axis=0)
    m_all = _ada_all(c_all, ada_w, ada_b)

    def mods(lo, n):
        return [[m_all[i, lo:lo + n, None, k * D_MODEL:(k + 1) * D_MODEL] for k in range(6)] for i in range(depth)]

    w = {
        'norm_g': norm_g,
        'lru_w_in': lru_w_in.astype(BF16), 'lru_conv_w': lru_conv_w, 'lru_conv_b': lru_conv_b,
        'lru_gate_w': lru_gate_w.astype(BF16), 'lru_gate_b': lru_gate_b, 'lru_lambda': lru_lambda,
        'lru_w_out': lru_w_out.astype(BF16),
        'att_w_qkv': att_w_qkv.astype(BF16), 'att_q_gain': att_q_gain, 'att_k_gain': att_k_gain,
        'att_bias': jax.vmap(_rel_bias_tile)(att_rel_bias), 'att_w_o': att_w_o.astype(BF16),
        'peer_wqt': jnp.swapaxes(peer_w_query, 1, 2).astype(BF16),
        'peer_sk': peer_sub_keys.astype(BF16).reshape(depth, 2 * PEER_HEADS, N_KEYS, PEER_HALF),
        'peer_u': peer_u, 'peer_v': peer_v,
    }
    prompt_chains = [_Chain(x_prompt[b:b + 1], mods(b, 1), None, None, None, None, w, True) for b in range(bp)]
    sample_chain = _Chain(x_sample, mods(bp, bs), state_conv, state_lru_h, cache_k, cache_v, w, False)
    _run_chains(prompt_chains + [sample_chain], depth)
    y_prompt, p_conv, p_h, p_k, p_v = (
        jnp.concatenate(parts, axis=ax)
        for parts, ax in zip(zip(*(c.results() for c in prompt_chains)), (0, 1, 1, 1, 1)))
    y_sample, s_conv, s_h, s_k, s_v = sample_chain.results()
    return (y_prompt, y_sample, p_conv, p_h, p_k, p_v, s_conv, s_h, s_k, s_v)
```

```python
import functools

import jax
import jax.numpy as jnp
import numpy as np
from jax import lax
from jax.experimental import pallas as pl
from jax.experimental.pallas import tpu as pltpu
from jax.experimental.pallas import tpu_sc as plsc

D_MODEL = 1024
CHUNK = 64
LRU_WIDTH = D_MODEL
LRU_BLOCKS = 4
LRU_BLOCK_W = LRU_WIDTH // LRU_BLOCKS
CONV_W = 4
RG_C = 8.0
N_HEADS = 8
HEAD_DIM = D_MODEL // N_HEADS
LEFT_CHUNKS = 8
LEFT_ROWS = LEFT_CHUNKS * CHUNK
BAND = (LEFT_CHUNKS + 1) * CHUNK
REL_CLIP = 128
N_REL = 2 * REL_CLIP + 1
PEER_HEADS = 8
N_KEYS = 128
N_EXPERTS = N_KEYS * N_KEYS
PEER_TOPK = 16
PEER_QDIM = 256
PEER_HALF = PEER_QDIM // 2
PEER_SEL = PEER_HEADS * PEER_TOPK
EPS = 1e-6
NEG_INF = -1e30
PAST_LEN = 2048

LANES = 128
SUBLANES = 8
BF16 = jnp.bfloat16
F32 = jnp.float32


_CAND_RUNS = [(a, PEER_TOPK // (a + 1)) for a in range(PEER_TOPK)]
_NCAND = sum(n for _, n in _CAND_RUNS)
_NCAND_PAD = -(-_NCAND // SUBLANES) * SUBLANES
_CAND_FLAT = np.full((_NCAND_PAD,), PEER_TOPK * PEER_TOPK, np.int32)
_off = 0
for _a, _n in _CAND_RUNS:
    _CAND_FLAT[_off:_off + _n] = _a * PEER_TOPK + np.arange(_n)
    _off += _n


def _peer_route_kernel(x_ref, sh_ref, sc_ref, ng_ref, wqt_ref, sk_ref, cand_ref,
                       hn_ref, idx_ref, g_ref,
                       q_s, cs_s, ci_s):
    tt = x_ref.shape[1]
    x = x_ref[0]
    ms = jnp.mean(x * x, axis=-1, keepdims=True)
    hn = x * lax.rsqrt(ms + EPS) * ng_ref[...]
    hn = hn * (1.0 + sc_ref[0]) + sh_ref[0]
    hn_ref[0] = hn
    qt = lax.dot_general(wqt_ref[...], hn.astype(BF16), (((1,), (1,)), ((), ())),
                         preferred_element_type=F32)
    for gidx in range(2 * PEER_HEADS):
        q_s[gidx] = qt[gidx * PEER_HALF:(gidx + 1) * PEER_HALF, :]

    key_id = lax.broadcasted_iota(jnp.int32, (N_KEYS, tt), 0)
    slot_id = lax.broadcasted_iota(jnp.int32, (PEER_TOPK, tt), 0)
    cand = cand_ref[:, 0:tt]
    big = PEER_TOPK * PEER_TOPK
    cs_s[_NCAND:_NCAND_PAD, :] = jnp.full((_NCAND_PAD - _NCAND, tt), -jnp.inf, F32)
    ci_s[_NCAND:_NCAND_PAD, :] = jnp.zeros((_NCAND_PAD - _NCAND, tt), jnp.int32)

    def head_body(h, _):
        s = [jnp.dot(sk_ref[2 * h + p], q_s[2 * h + p].astype(BF16), preferred_element_type=F32)
             for p in range(2)]
        ts = [jnp.zeros((PEER_TOPK, tt), F32) for _ in range(2)]
        ti = [jnp.zeros((PEER_TOPK, tt), jnp.int32) for _ in range(2)]
        for j in range(PEER_TOPK):
            for p in range(2):
                m = jnp.max(s[p], axis=0, keepdims=True)
                am = jnp.min(jnp.where(s[p] == m, key_id, N_KEYS), axis=0, keepdims=True)
                ts[p] = jnp.where(slot_id == j, m, ts[p])
                ti[p] = jnp.where(slot_id == j, am, ti[p])
                s[p] = jnp.where(key_id == am, -jnp.inf, s[p])
        off = 0
        for a, n in _CAND_RUNS:
            cs_s[off:off + n, :] = ts[0][a:a + 1, :] + ts[1][0:n, :]
            ci_s[off:off + n, :] = ti[0][a:a + 1, :] * N_KEYS + ti[1][0:n, :]
            off += n
        cs = cs_s[...]
        ci = ci_s[...]
        fs = jnp.zeros((PEER_TOPK, tt), F32)
        ei = jnp.zeros((PEER_TOPK, tt), jnp.int32)
        for k in range(PEER_TOPK):
            m = jnp.max(cs, axis=0, keepdims=True)
            cmin = jnp.min(jnp.where(cs == m, cand, big), axis=0, keepdims=True)
            sel = cand == cmin
            e = jnp.sum(jnp.where(sel, ci, 0), axis=0, keepdims=True)
            fs = jnp.where(slot_id == k, m, fs)
            ei = jnp.where(slot_id == k, e, ei)
            cs = jnp.where(sel, -jnp.inf, cs)
        ex = jnp.exp(fs - fs[0:1, :])
        g = ex / jnp.sum(ex, axis=0, keepdims=True)
        row0 = pl.multiple_of(h * PEER_TOPK, PEER_TOPK)
        g_ref[0, pl.ds(row0, PEER_TOPK), :] = g
        idx_ref[0, pl.ds(row0, PEER_TOPK), :] = ei
        return 0

    lax.fori_loop(0, PEER_HEADS, head_body, 0)


def _peer_route(x, sh, sc, ng, wqt, sk):
    B, T, D = x.shape
    tt = min(T, 128)
    grid = (B, T // tt)
    tok = lambda b, t: (b, t, 0)
    tok_t = lambda b, t: (b, 0, t)
    per_b = lambda b, t: (b, 0, 0)
    cand_tab = jnp.asarray(np.broadcast_to(_CAND_FLAT[:, None], (_NCAND_PAD, LANES)))
    hn, idx_t, g_t = pl.pallas_call(
        _peer_route_kernel,
        grid=grid,
        in_specs=[
            pl.BlockSpec((1, tt, D), tok),
            pl.BlockSpec((1, 1, D), per_b),
            pl.BlockSpec((1, 1, D), per_b),
            pl.BlockSpec((1, D), lambda b, t: (0, 0)),
            pl.BlockSpec(wqt.shape, lambda b, t: (0, 0)),
            pl.BlockSpec(sk.shape, lambda b, t: (0, 0, 0)),
            pl.BlockSpec(cand_tab.shape, lambda b, t: (0, 0)),
        ],
        out_specs=[
            pl.BlockSpec((1, tt, D), tok),
            pl.BlockSpec((1, PEER_SEL, tt), tok_t),
            pl.BlockSpec((1, PEER_SEL, tt), tok_t),
        ],
        out_shape=[
            jax.ShapeDtypeStruct((B, T, D), F32),
            jax.ShapeDtypeStruct((B, PEER_SEL, T), jnp.int32),
            jax.ShapeDtypeStruct((B, PEER_SEL, T), F32),
        ],
        scratch_shapes=[
            pltpu.VMEM((2 * PEER_HEADS, PEER_HALF, tt), F32),
            pltpu.VMEM((_NCAND_PAD, tt), F32),
            pltpu.VMEM((_NCAND_PAD, tt), jnp.int32),
        ],
        compiler_params=pltpu.CompilerParams(
            dimension_semantics=("parallel", "parallel"), vmem_limit_bytes=48 * 1024 * 1024),
        name="peer_route",
    )(x, sh, sc, ng, wqt, sk, cand_tab)
    return hn, jnp.swapaxes(idx_t, 1, 2), jnp.swapaxes(g_t, 1, 2)


SC_CORES = 2
SC_SUBCORES = 16
SC_LANES = 16
SC_WORKERS = SC_CORES * SC_SUBCORES
NCHUNK = D_MODEL // SC_LANES
HALF_D = D_MODEL // 2
NWCHUNK = HALF_D // SC_LANES
HI_MASK = -65536
TOK_BLOCK = 8
NSLOT = 3


def _pack_expert_tables(u_tab, v_tab):
    def pack(tab):
        bits = lax.bitcast_convert_type(tab.astype(BF16), jnp.uint16).astype(jnp.uint32)
        return bits[:, :HALF_D] | (bits[:, HALF_D:] << 16)

    return lax.bitcast_convert_type(jnp.concatenate([pack(u_tab), pack(v_tab)], axis=1), jnp.int32)


def _unpack_words(words):
    lo = lax.bitcast_convert_type(lax.shift_left(words, jnp.full_like(words, 16)), F32)
    hi = lax.bitcast_convert_type(words & HI_MASK, F32)
    return lo, hi


def _peer_expert_sc(x, hn, eidx, g, g2, uv_tab, tokens_per_seq, n_tokens):
    N, D = n_tokens, x.shape[1]
    tpw = N // SC_WORKERS
    assert tpw * SC_WORKERS == N and tpw % TOK_BLOCK == 0
    assert g2.shape[0] == 1 or tokens_per_seq % tpw == 0
    workers_per_seq = max(tokens_per_seq // tpw, 1) if g2.shape[0] > 1 else SC_WORKERS
    nblocks = tpw // TOK_BLOCK
    steps = TOK_BLOCK * PEER_HEADS
    mesh = plsc.VectorSubcoreMesh(core_axis_name="c", subcore_axis_name="s",
                                  num_cores=SC_CORES, num_subcores=SC_SUBCORES)

    @functools.partial(
        pl.kernel, out_type=jax.ShapeDtypeStruct((N, D), F32), mesh=mesh,
        scratch_types=[
            pltpu.VMEM((TOK_BLOCK, D), F32),
            pltpu.VMEM((TOK_BLOCK, D), F32),
            pltpu.VMEM((TOK_BLOCK, PEER_SEL), jnp.int32),
            pltpu.VMEM((TOK_BLOCK, PEER_SEL), F32),
            pltpu.VMEM((D,), F32),
            pltpu.VMEM((NSLOT, PEER_TOPK, D), jnp.int32),
            pltpu.VMEM((D,), F32),
            pltpu.SemaphoreType.DMA((NSLOT,)),
        ],
        compiler_params=pltpu.CompilerParams(needs_layout_passes=False),
        name="peer_expert_sc",
    )
    def k(x_hbm, hn_hbm, idx_hbm, g_hbm, g2_hbm, uv_hbm, out_hbm,
          hn_b, x_b, idx_b, g_b, g2_v, uvbuf, o_v, sem):
        wid = lax.axis_index("s") * SC_CORES + lax.axis_index("c")
        tok0 = wid * tpw
        pltpu.sync_copy(g2_hbm.at[wid // workers_per_seq], g2_v)
        lane = lax.iota(jnp.int32, SC_LANES)

        def gather_desc(s, slot):
            t = s // PEER_HEADS
            h0 = pl.multiple_of((s % PEER_HEADS) * PEER_TOPK, PEER_TOPK)
            ids = idx_b.at[t, pl.ds(h0, PEER_TOPK)]
            return pltpu.make_async_copy(uv_hbm.at[ids], uvbuf.at[slot], sem.at[slot])

        def issue(s, slot):
            gather_desc(s, slot).start()

        def wait(s, slot):
            gather_desc(s, slot).wait()

        def compute(s, slot):
            t = s // PEER_HEADS
            h = s % PEER_HEADS
            h0 = pl.multiple_of(h * PEER_TOPK, PEER_TOPK)

            @pl.when(h == 0)
            def _():
                @pl.loop(0, NCHUNK)
                def _(c):
                    o_v[pl.ds(pl.multiple_of(c * SC_LANES, SC_LANES), SC_LANES)] = jnp.zeros((SC_LANES,), F32)

            def ubody(c, accs):
                c0 = pl.multiple_of(c * SC_LANES, SC_LANES)
                x_lo = hn_b[t, pl.ds(c0, SC_LANES)]
                x_hi = hn_b[t, pl.ds(HALF_D + c0, SC_LANES)]
                out = []
                for kk in range(PEER_TOPK):
                    lo, hi = _unpack_words(uvbuf[slot, kk, pl.ds(c0, SC_LANES)])
                    out.append(accs[kk] + (x_lo * lo + x_hi * hi))
                return tuple(out)

            accs = lax.fori_loop(0, NWCHUNK, ubody,
                                 tuple(jnp.zeros((SC_LANES,), F32) for _ in range(PEER_TOPK)))
            last = jnp.full((SC_LANES,), SC_LANES - 1, jnp.int32)
            r = jnp.zeros((SC_LANES,), F32)
            for kk in range(PEER_TOPK):
                tot = jnp.take_along_axis(plsc.cumsum(accs[kk]), last, axis=0)
                r = jnp.where(lane == kk, tot, r)
            gv = g_b[t, pl.ds(h0, PEER_TOPK)]
            z = 0.7978845608028654 * (r + 0.044715 * (r * r * r))
            w = gv * r / (1.0 + jnp.exp(-2.0 * z))
            wk = [jnp.take_along_axis(w, jnp.full((SC_LANES,), kk, jnp.int32), axis=0) for kk in range(PEER_TOPK)]

            @plsc.parallel_loop(0, NWCHUNK, unroll=2)
            def _(c):
                c0 = pl.multiple_of(c * SC_LANES, SC_LANES)
                los, his = [], []
                for kk in range(PEER_TOPK):
                    lo, hi = _unpack_words(uvbuf[slot, kk, pl.ds(HALF_D + c0, SC_LANES)])
                    los.append(wk[kk] * lo)
                    his.append(wk[kk] * hi)
                while len(los) > 1:
                    los = [los[i] + los[i + 1] for i in range(0, len(los), 2)]
                    his = [his[i] + his[i + 1] for i in range(0, len(his), 2)]
                plsc.addupdate(o_v.at[pl.ds(c0, SC_LANES)], los[0])
                plsc.addupdate(o_v.at[pl.ds(HALF_D + c0, SC_LANES)], his[0])

            @pl.when(h == PEER_HEADS - 1)
            def _():
                @pl.loop(0, NCHUNK)
                def _(c):
                    c0 = pl.multiple_of(c * SC_LANES, SC_LANES)
                    x_b[t, pl.ds(c0, SC_LANES)] = (x_b[t, pl.ds(c0, SC_LANES)]
                                                   + g2_v[pl.ds(c0, SC_LANES)] * o_v[pl.ds(c0, SC_LANES)])

        @pl.loop(0, nblocks)
        def _(blk):
            tok = pl.multiple_of(tok0 + blk * TOK_BLOCK, TOK_BLOCK)
            pltpu.sync_copy(idx_hbm.at[pl.ds(tok, TOK_BLOCK)], idx_b)
            pltpu.sync_copy(g_hbm.at[pl.ds(tok, TOK_BLOCK)], g_b)
            pltpu.sync_copy(hn_hbm.at[pl.ds(tok, TOK_BLOCK)], hn_b)
            pltpu.sync_copy(x_hbm.at[pl.ds(tok, TOK_BLOCK)], x_b)
            for s0 in range(NSLOT - 1):
                issue(s0, s0)

            @pl.loop(0, steps)
            def _(s):
                @pl.when(s + (NSLOT - 1) < steps)
                def _():
                    issue(s + (NSLOT - 1), (s + (NSLOT - 1)) % NSLOT)

                slot = s % NSLOT
                wait(s, slot)
                compute(s, slot)

            pltpu.sync_copy(x_b, out_hbm.at[pl.ds(tok, TOK_BLOCK)])

    return k(x, hn, eidx, g, g2, uv_tab)


TC_TOK_BLOCK = 64
TC_ISSUE_UNROLL = 32


def _peer_expert_tc_kernel(idx_hbm, x_ref, hn_ref, g_ref, g2_ref, uv_hbm, out_ref,
                           idx_s, uvbuf, sem, isem, *, first_block):
    tb = x_ref.shape[0]
    row0 = (pl.program_id(0) + first_block) * tb
    cp = pltpu.make_async_copy(idx_hbm.at[pl.ds(row0, tb)], idx_s, isem)
    cp.start()
    cp.wait()

    def issue(t, slot):
        def body(jb, _):
            for jj in range(TC_ISSUE_UNROLL):
                j = jb * TC_ISSUE_UNROLL + jj
                pltpu.make_async_copy(uv_hbm.at[pl.ds(idx_s[t, j], 1)], uvbuf.at[slot, pl.ds(j, 1)],
                                      sem.at[slot]).start()
            return 0

        lax.fori_loop(0, PEER_SEL // TC_ISSUE_UNROLL, body, 0)

    def wait(slot):
        pltpu.make_async_copy(uv_hbm.at[pl.ds(0, PEER_SEL)], uvbuf.at[slot], sem.at[slot]).wait()

    issue(0, 0)
    r2 = lax.broadcasted_iota(jnp.int32, (PEER_SEL, LANES), 0)
    c2 = lax.broadcasted_iota(jnp.int32, (PEER_SEL, LANES), 1)
    diag = r2 == c2

    def tok(t, _):
        slot = t % 2

        @pl.when(t + 1 < tb)
        def _():
            issue(t + 1, 1 - slot)

        wait(slot)
        xr = hn_ref[pl.ds(t, 1), :]
        words = uvbuf[slot]
        u_lo, u_hi = _unpack_words(words[:, :HALF_D])
        prod = u_lo * xr[:, :HALF_D] + u_hi * xr[:, HALF_D:]
        p = prod[:, 0:LANES]
        for c in range(1, HALF_D // LANES):
            p = p + prod[:, c * LANES:(c + 1) * LANES]
        act = jnp.sum(p, axis=-1, keepdims=True)
        grow = g_ref[pl.ds(t, 1), :]
        gcol = jnp.sum(jnp.where(diag, jnp.broadcast_to(grow, (PEER_SEL, LANES)), 0.0),
                       axis=-1, keepdims=True)
        wgt = gcol * _gelu_tanh(act)
        v_lo, v_hi = _unpack_words(words[:, HALF_D:])
        o = jnp.concatenate([jnp.sum(v_lo * wgt, axis=0, keepdims=True),
                             jnp.sum(v_hi * wgt, axis=0, keepdims=True)], axis=-1)
        out_ref[pl.ds(t, 1), :] = x_ref[pl.ds(t, 1), :] + g2_ref[...] * o
        return 0

    lax.fori_loop(0, tb, tok, 0)


def _peer_expert_tc(x, hn, eidx, g, g2, uv_tab, first_token):
    N, D = x.shape
    tb = TC_TOK_BLOCK
    n_out = N - first_token
    assert first_token % tb == 0 and n_out % tb == 0
    first_block = first_token // tb
    tok = lambda t: (t + first_block, 0)
    return pl.pallas_call(
        functools.partial(_peer_expert_tc_kernel, first_block=first_block),
        grid=(n_out // tb,),
        in_specs=[
            pl.BlockSpec(memory_space=pl.ANY),
            pl.BlockSpec((tb, D), tok),
            pl.BlockSpec((tb, D), tok),
            pl.BlockSpec((tb, PEER_SEL), tok),
            pl.BlockSpec((1, D), lambda t: (0, 0)),
            pl.BlockSpec(memory_space=pl.ANY),
        ],
        out_specs=pl.BlockSpec((tb, D), lambda t: (t, 0)),
        out_shape=jax.ShapeDtypeStruct((n_out, D), F32),
        scratch_shapes=[
            pltpu.SMEM((tb, PEER_SEL), jnp.int32),
            pltpu.VMEM((2, PEER_SEL, D), jnp.int32),
            pltpu.SemaphoreType.DMA((2,)),
            pltpu.SemaphoreType.DMA,
        ],
        compiler_params=pltpu.CompilerParams(
            dimension_semantics=("arbitrary",), vmem_limit_bytes=32 * 1024 * 1024,
            disable_bounds_checks=True),
        name="peer_expert_tc",
    )(eidx, x, hn, g, g2, uv_tab)


TC_SHARE_TOKENS = 1280


def _peer_layer(x, sh2, sc2, g2, ng, wqt, sk, uv_tab):
    B, T, D = x.shape
    hn, eidx, g = _peer_route(x, sh2, sc2, ng, wqt, sk)
    N = B * T
    flat = (x.reshape(N, D), hn.reshape(N, D), eidx.reshape(N, PEER_SEL), g.reshape(N, PEER_SEL),
            g2.reshape(B, D), uv_tab)
    n_tc = TC_SHARE_TOKENS if (B == 1 and N >= 4 * TC_SHARE_TOKENS) else 0
    y = _peer_expert_sc(*flat, tokens_per_seq=T, n_tokens=N - n_tc)
    if n_tc:
        y = jnp.concatenate([y, _peer_expert_tc(*flat, first_token=N - n_tc)], axis=0)
    return y.reshape(B, T, D)


def _attn_layer_kernel(*refs, tq, ch, has_cache):
    if has_cache:
        (x_ref, sh_ref, sc_ref, g1_ref, ng_ref, wqkv_ref, qg_ref, kg_ref, bias_ref, wo_ref, kc_ref, vc_ref,
         y_ref, ko_ref, vo_ref, kcat, vcat, qn_s, o_s) = refs
    else:
        (x_ref, sh_ref, sc_ref, g1_ref, ng_ref, wqkv_ref, qg_ref, kg_ref, bias_ref, wo_ref,
         y_ref, ko_ref, vo_ref, kcat, vcat, qn_s, o_s) = refs
    t = pl.program_id(1)
    band = LEFT_ROWS + ch

    if has_cache:
        kcat[0:LEFT_ROWS, :] = kc_ref[0].astype(BF16)
        vcat[0:LEFT_ROWS, :] = vc_ref[0].astype(BF16)
    else:
        @pl.when(t == 0)
        def _():
            kcat[0:LEFT_ROWS, :] = jnp.zeros((LEFT_ROWS, D_MODEL), BF16)
            vcat[0:LEFT_ROWS, :] = jnp.zeros((LEFT_ROWS, D_MODEL), BF16)

    x = x_ref[0]
    ms = jnp.mean(x * x, axis=-1, keepdims=True)
    hn = x * lax.rsqrt(ms + EPS) * ng_ref[...]
    hn = hn * (1.0 + sc_ref[0]) + sh_ref[0]
    qkv = jnp.dot(hn.astype(BF16), wqkv_ref[...], preferred_element_type=F32)
    for h in range(N_HEADS):
        lo = h * HEAD_DIM
        qh = qkv[:, lo:lo + HEAD_DIM]
        kh = qkv[:, D_MODEL + lo:D_MODEL + lo + HEAD_DIM]
        qh = qh * lax.rsqrt(jnp.mean(qh * qh, axis=-1, keepdims=True) + EPS) * qg_ref[...]
        kh = kh * lax.rsqrt(jnp.mean(kh * kh, axis=-1, keepdims=True) + EPS) * kg_ref[...]
        qn_s[:, lo:lo + HEAD_DIM] = qh.astype(BF16)
        ko_ref[0, :, lo:lo + HEAD_DIM] = kh
        kcat[LEFT_ROWS:LEFT_ROWS + tq, lo:lo + HEAD_DIM] = kh.astype(BF16)
    v = qkv[:, 2 * D_MODEL:]
    vo_ref[0] = v
    vcat[LEFT_ROWS:LEFT_ROWS + tq, :] = v.astype(BF16)

    scale = HEAD_DIM ** -0.5
    kpos_rel = lax.broadcasted_iota(jnp.int32, (ch, band), 1)

    def chunk_body(c, _):
        r0 = pl.multiple_of(c * ch, ch)
        if not has_cache:
            valid = kpos_rel + (t * tq + r0) >= LEFT_ROWS
        for h in range(N_HEADS):
            lo = h * HEAD_DIM
            q = qn_s[pl.ds(r0, ch), lo:lo + HEAD_DIM]
            kb = kcat[pl.ds(r0, band), lo:lo + HEAD_DIM]
            vb = vcat[pl.ds(r0, band), lo:lo + HEAD_DIM]
            s = lax.dot_general(q, kb, (((1,), (1,)), ((), ())), preferred_element_type=F32) * scale
            s = s + bias_ref[h, 0:ch, 0:band]
            if not has_cache:
                s = jnp.where(valid, s, NEG_INF)
            m = jnp.max(s, axis=-1, keepdims=True)
            e = jnp.exp(s - m)
            p = e / jnp.sum(e, axis=-1, keepdims=True)
            o_s[pl.ds(r0, ch), lo:lo + HEAD_DIM] = jnp.dot(p.astype(BF16), vb, preferred_element_type=F32).astype(BF16)
        return 0

    lax.fori_loop(0, tq // ch, chunk_body, 0)

    out = jnp.dot(o_s[...], wo_ref[...], preferred_element_type=F32)
    y_ref[0] = x + g1_ref[0] * out

    if not has_cache:
        kcat[0:LEFT_ROWS, :] = kcat[tq:tq + LEFT_ROWS, :]
        vcat[0:LEFT_ROWS, :] = vcat[tq:tq + LEFT_ROWS, :]


def _rel_bias_tile(rel_bias):
    i = jnp.arange(CHUNK)[:, None]
    r = jnp.arange(LEFT_ROWS + CHUNK)[None, :]
    rel = jnp.clip(i + LEFT_ROWS - r, -REL_CLIP, REL_CLIP) + REL_CLIP
    return rel_bias[:, rel]


def _attn_layer(x, sh, sc, g1, ng, wqkv, qg, kg, bias, wo, k_cache=None, v_cache=None):
    B, T, D = x.shape
    has_cache = k_cache is not None
    if has_cache:
        tq, ch = T, T
        keep_blocks = 1
    else:
        tq, ch = 256, CHUNK
        keep_blocks = LEFT_ROWS // tq
    nt = T // tq
    assert nt >= keep_blocks
    tok = lambda b, t: (b, t, 0)
    per_b = lambda b, t: (b, 0, 0)
    c2 = lambda b, t: (0, 0)
    c3 = lambda b, t: (0, 0, 0)
    keep = lambda b, t: (b, jnp.maximum(t - (nt - keep_blocks), 0), 0)
    in_specs = [
        pl.BlockSpec((1, tq, D), tok),
        pl.BlockSpec((1, 1, D), per_b), pl.BlockSpec((1, 1, D), per_b), pl.BlockSpec((1, 1, D), per_b),
        pl.BlockSpec((1, D), c2),
        pl.BlockSpec(wqkv.shape, c2),
        pl.BlockSpec((1, HEAD_DIM), c2), pl.BlockSpec((1, HEAD_DIM), c2),
        pl.BlockSpec(bias.shape, c3),
        pl.BlockSpec(wo.shape, c2),
    ]
    args = [x, sh, sc, g1, ng, wqkv, qg, kg, bias, wo]
    if has_cache:
        in_specs += [pl.BlockSpec((1, LEFT_ROWS, D), per_b), pl.BlockSpec((1, LEFT_ROWS, D), per_b)]
        args += [k_cache, v_cache]
    return pl.pallas_call(
        functools.partial(_attn_layer_kernel, tq=tq, ch=ch, has_cache=has_cache),
        grid=(B, nt),
        in_specs=in_specs,
        out_specs=[pl.BlockSpec((1, tq, D), tok), pl.BlockSpec((1, tq, D), keep), pl.BlockSpec((1, tq, D), keep)],
        out_shape=[jax.ShapeDtypeStruct((B, T, D), F32),
                   jax.ShapeDtypeStruct((B, keep_blocks * tq, D), F32),
                   jax.ShapeDtypeStruct((B, keep_blocks * tq, D), F32)],
        scratch_shapes=[
            pltpu.VMEM((LEFT_ROWS + tq, D), BF16),
            pltpu.VMEM((LEFT_ROWS + tq, D), BF16),
            pltpu.VMEM((tq, D), BF16),
            pltpu.VMEM((tq, D), BF16),
        ],
        compiler_params=pltpu.CompilerParams(
            dimension_semantics=("parallel", "arbitrary"), vmem_limit_bytes=48 * 1024 * 1024),
        name="attn_layer",
    )(*args)


def _gelu_tanh(x):
    return 0.5 * x * (1.0 + jnp.tanh(0.7978845608028654 * (x + 0.044715 * (x * x * x))))


def _lru_layer_kernel(x_ref, sh_ref, sc_ref, g1_ref, ng_ref, win_ref, cw_ref, cb_ref, gw_ref, gb_ref, lam_ref,
                      wout_ref, cs0_ref, h0_ref,
                      y_ref, cs_ref, hl_ref,
                      ext_s, a_s, b_s, hcar_s):
    tt = x_ref.shape[1]
    W = LRU_WIDTH
    t = pl.program_id(1)

    @pl.when(t == 0)
    def _():
        ext_s[0:SUBLANES, :] = cs0_ref[0]
        hcar_s[...] = h0_ref[0]

    x = x_ref[0]
    ms = jnp.mean(x * x, axis=-1, keepdims=True)
    hn = x * lax.rsqrt(ms + EPS) * ng_ref[...]
    hn = hn * (1.0 + sc_ref[0]) + sh_ref[0]
    proj = jnp.dot(hn.astype(BF16), win_ref[...], preferred_element_type=F32)
    gate = _gelu_tanh(proj[:, :W])
    rec = proj[:, W:]
    ext_s[SUBLANES:SUBLANES + tt, :] = rec
    u = cb_ref[...] + cw_ref[3:4, :] * rec
    for k in range(CONV_W - 1):
        u = u + cw_ref[k:k + 1, :] * ext_s[SUBLANES - 3 + k:SUBLANES - 3 + k + tt, :]
    tail = ext_s[tt:tt + SUBLANES, :]
    cs_ref[0] = tail
    ext_s[0:SUBLANES, :] = tail

    sp_in = -lam_ref[...]
    softplus = jnp.maximum(sp_in, 0.0) + jnp.log1p(jnp.exp(-jnp.abs(sp_in)))
    for n in range(LRU_BLOCKS):
        lo = n * LRU_BLOCK_W
        ub = u[:, lo:lo + LRU_BLOCK_W]
        ubb = ub.astype(BF16)
        gr = jnp.dot(ubb, gw_ref[0, n], preferred_element_type=F32) + gb_ref[0:1, lo:lo + LRU_BLOCK_W]
        gi = jnp.dot(ubb, gw_ref[1, n], preferred_element_type=F32) + gb_ref[1:2, lo:lo + LRU_BLOCK_W]
        r = jax.nn.sigmoid(gr)
        i = jax.nn.sigmoid(gi)
        log_a = -RG_C * r * softplus[:, lo:lo + LRU_BLOCK_W]
        a = jnp.exp(log_a)
        a_s[:, lo:lo + LRU_BLOCK_W] = a
        b_s[:, lo:lo + LRU_BLOCK_W] = jnp.sqrt(1.0 - a * a) * i * ub

    def group(gi_, h):
        r0 = pl.multiple_of(gi_ * SUBLANES, SUBLANES)
        a8 = a_s[pl.ds(r0, SUBLANES), :]
        b8 = b_s[pl.ds(r0, SUBLANES), :]
        rows = []
        for j in range(SUBLANES):
            h = a8[j:j + 1, :] * h + b8[j:j + 1, :]
            rows.append(h)
        b_s[pl.ds(r0, SUBLANES), :] = jnp.concatenate(rows, axis=0)
        return h

    h_last = lax.fori_loop(0, tt // SUBLANES, group, hcar_s[...])
    hcar_s[...] = h_last
    hl_ref[0] = h_last
    yv = b_s[...] * gate
    out = jnp.dot(yv.astype(BF16), wout_ref[...], preferred_element_type=F32)
    y_ref[0] = x + g1_ref[0] * out


def _lru_layer(x, sh, sc, g1, ng, w_in, conv_w, conv_b, gate_w, gate_b, lam, w_out, conv_state, h0):
    B, T, D = x.shape
    W = LRU_WIDTH
    tt = min(T, 256)
    cs0 = jnp.pad(conv_state, ((0, 0), (SUBLANES - (CONV_W - 1), 0), (0, 0)))
    tok = lambda b, t: (b, t, 0)
    per_b = lambda b, t: (b, 0, 0)
    c2 = lambda b, t: (0, 0)
    y, cs, hl = pl.pallas_call(
        _lru_layer_kernel,
        grid=(B, T // tt),
        in_specs=[
            pl.BlockSpec((1, tt, D), tok),
            pl.BlockSpec((1, 1, D), per_b), pl.BlockSpec((1, 1, D), per_b), pl.BlockSpec((1, 1, D), per_b),
            pl.BlockSpec((1, D), c2),
            pl.BlockSpec(w_in.shape, c2),
            pl.BlockSpec((CONV_W, W), c2), pl.BlockSpec((1, W), c2),
            pl.BlockSpec(gate_w.shape, lambda b, t: (0, 0, 0, 0)),
            pl.BlockSpec((2, W), c2), pl.BlockSpec((1, W), c2),
            pl.BlockSpec(w_out.shape, c2),
            pl.BlockSpec((1, SUBLANES, W), per_b), pl.BlockSpec((1, 1, W), per_b),
        ],
        out_specs=[pl.BlockSpec((1, tt, D), tok), pl.BlockSpec((1, SUBLANES, W), per_b),
                   pl.BlockSpec((1, 1, W), per_b)],
        out_shape=[jax.ShapeDtypeStruct((B, T, D), F32), jax.ShapeDtypeStruct((B, SUBLANES, W), F32),
                   jax.ShapeDtypeStruct((B, 1, W), F32)],
        scratch_shapes=[
            pltpu.VMEM((tt + SUBLANES, W), F32),
            pltpu.VMEM((tt, W), F32),
            pltpu.VMEM((tt, W), F32),
            pltpu.VMEM((1, W), F32),
        ],
        compiler_params=pltpu.CompilerParams(
            dimension_semantics=("parallel", "arbitrary"), vmem_limit_bytes=48 * 1024 * 1024),
        name="lru_layer",
    )(x, sh, sc, g1, ng, w_in, conv_w, conv_b, gate_w, gate_b, lam, w_out, cs0, h0[:, None, :])
    return y, cs[:, SUBLANES - (CONV_W - 1):], hl[:, 0]


def _ada_kernel(c_ref, w_ref, b_ref, m_ref):
    c = c_ref[...]
    s = c * jax.nn.sigmoid(c)
    m_ref[0] = jnp.dot(s, w_ref[0], preferred_element_type=F32, precision=lax.Precision.HIGHEST) + b_ref[0]


def _ada_all(c_all, ada_w, ada_b):
    L, D, E = ada_w.shape
    R = c_all.shape[0]
    tn = 1536
    return pl.pallas_call(
        _ada_kernel,
        grid=(L, E // tn),
        in_specs=[pl.BlockSpec((R, D), lambda l, j: (0, 0)),
                  pl.BlockSpec((1, D, tn), lambda l, j: (l, 0, j)),
                  pl.BlockSpec((1, 1, tn), lambda l, j: (l, 0, j))],
        out_specs=pl.BlockSpec((1, R, tn), lambda l, j: (l, 0, j)),
        out_shape=jax.ShapeDtypeStruct((L, R, E), F32),
        compiler_params=pltpu.CompilerParams(
            dimension_semantics=("parallel", "parallel"), vmem_limit_bytes=32 * 1024 * 1024),
        name="ada_mod",
    )(c_all, ada_w, ada_b[:, None, :])


class _Chain:
    def __init__(self, x, mods, conv_st, lru_st, k_cache, v_cache, w, prompt):
        self.x, self.mods, self.w, self.prompt = x, mods, w, prompt
        self.conv_st, self.lru_st, self.k_cache, self.v_cache = conv_st, lru_st, k_cache, v_cache
        self.new_conv, self.new_h, self.new_k, self.new_v = [], [], [], []
        self.layer = 0

    def step(self):
        i, w, x = self.layer, self.w, self.x
        B, T, D = x.shape
        sh1, sc1, g1, sh2, sc2, g2 = self.mods[i]
        j = i // 2
        if i % 2 == 0:
            if self.prompt:
                cs0 = jnp.zeros((B, CONV_W - 1, LRU_WIDTH), x.dtype)
                h0 = jnp.zeros((B, LRU_WIDTH), F32)
            else:
                cs0, h0 = self.conv_st[j], self.lru_st[j]
            x, cnew, hnew = _lru_layer(x, sh1, sc1, g1, w['norm_g'][i, 0][None, :], w['lru_w_in'][j],
                                       w['lru_conv_w'][j], w['lru_conv_b'][j][None, :], w['lru_gate_w'][j],
                                       w['lru_gate_b'][j], w['lru_lambda'][j][None, :], w['lru_w_out'][j], cs0, h0)
            self.new_conv.append(cnew)
            self.new_h.append(hnew)
        else:
            if self.prompt:
                kc = vc = None
            else:
                kc = self.k_cache[j].reshape(B, LEFT_ROWS, D)
                vc = self.v_cache[j].reshape(B, LEFT_ROWS, D)
            x, kn, vn = _attn_layer(x, sh1, sc1, g1, w['norm_g'][i, 0][None, :], w['att_w_qkv'][j],
                                    w['att_q_gain'][j][None, :], w['att_k_gain'][j][None, :], w['att_bias'][j],
                                    w['att_w_o'][j], kc, vc)
            self.new_k.append(kn.reshape(B, kn.shape[1], N_HEADS, HEAD_DIM))
            self.new_v.append(vn.reshape(B, vn.shape[1], N_HEADS, HEAD_DIM))
        self.x = _peer_layer(x, sh2, sc2, g2, w['norm_g'][i, 1][None, :], w['peer_wqt'][i], w['peer_sk'][i],
                             w['peer_uv'][i])
        self.layer += 1

    def results(self):
        return (self.x, jnp.stack(self.new_conv), jnp.stack(self.new_h), jnp.stack(self.new_k),
                jnp.stack(self.new_v))


def _run_chains(chains, depth):
    for wave in range(depth + len(chains) - 1):
        for ci, chain in enumerate(chains):
            if 0 <= wave - ci < depth:
                chain.step()


def kernel(x_prompt, x_sample, c_prompt, c_sample, state_conv, state_lru_h, cache_k, cache_v, norm_g, ada_w, ada_b, lru_w_in, lru_conv_w, lru_conv_b, lru_gate_w, lru_gate_b, lru_lambda, lru_w_out, att_w_qkv, att_q_gain, att_k_gain, att_rel_bias, att_w_o, peer_w_query, peer_sub_keys, peer_u, peer_v):
    depth = norm_g.shape[0]
    bp, bs = c_prompt.shape[0], c_sample.shape[0]
    rows = -(-(bp + bs) // SUBLANES) * SUBLANES
    c_all = jnp.concatenate([c_prompt, c_sample, jnp.zeros((rows - bp - bs, D_MODEL), F32)], axis=0)
    m_all = _ada_all(c_all, ada_w, ada_b)

    def mods(lo, n):
        return [[m_all[i, lo:lo + n, None, k * D_MODEL:(k + 1) * D_MODEL] for k in range(6)] for i in range(depth)]

    w = {
        'norm_g': norm_g,
        'lru_w_in': lru_w_in.astype(BF16), 'lru_conv_w': lru_conv_w, 'lru_conv_b': lru_conv_b,
        'lru_gate_w': lru_gate_w.astype(BF16), 'lru_gate_b': lru_gate_b, 'lru_lambda': lru_lambda,
        'lru_w_out': lru_w_out.astype(BF16),
        'att_w_qkv': att_w_qkv.astype(BF16), 'att_q_gain': att_q_gain, 'att_k_gain': att_k_gain,
        'att_bias': jax.vmap(_rel_bias_tile)(att_rel_bias), 'att_w_o': att_w_o.astype(BF16),
        'peer_wqt': jnp.swapaxes(peer_w_query, 1, 2).astype(BF16),
        'peer_sk': peer_sub_keys.astype(BF16).reshape(depth, 2 * PEER_HEADS, N_KEYS, PEER_HALF),
        'peer_uv': [_pack_expert_tables(peer_u[i], peer_v[i]) for i in range(depth)],
    }
    prompt_chains = [_Chain(x_prompt[b:b + 1], mods(b, 1), None, None, None, None, w, True) for b in range(bp)]
    sample_chain = _Chain(x_sample, mods(bp, bs), state_conv, state_lru_h, cache_k, cache_v, w, False)
    _run_chains(prompt_chains + [sample_chain], depth)
    y_prompt, p_conv, p_h, p_k, p_v = (
        jnp.concatenate(parts, axis=ax)
        for parts, ax in zip(zip(*(c.results() for c in prompt_chains)), (0, 1, 1, 1, 1)))
    y_sample, s_conv, s_h, s_k, s_v = sample_chain.results()
    return (y_prompt, y_sample, p_conv, p_h, p_k, p_v, s_conv, s_h, s_k, s_v)
```

```python
import functools

import jax
import jax.numpy as jnp
import numpy as np
from jax import lax
from jax.experimental import pallas as pl
from jax.experimental.pallas import tpu as pltpu
from jax.experimental.pallas import tpu_sc as plsc

D_MODEL = 1024
CHUNK = 64
LRU_WIDTH = D_MODEL
LRU_BLOCKS = 4
LRU_BLOCK_W = LRU_WIDTH // LRU_BLOCKS
CONV_W = 4
RG_C = 8.0
N_HEADS = 8
HEAD_DIM = D_MODEL // N_HEADS
LEFT_CHUNKS = 8
LEFT_ROWS = LEFT_CHUNKS * CHUNK
BAND = (LEFT_CHUNKS + 1) * CHUNK
REL_CLIP = 128
N_REL = 2 * REL_CLIP + 1
PEER_HEADS = 8
N_KEYS = 128
N_EXPERTS = N_KEYS * N_KEYS
PEER_TOPK = 16
PEER_QDIM = 256
PEER_HALF = PEER_QDIM // 2
PEER_SEL = PEER_HEADS * PEER_TOPK
EPS = 1e-6
NEG_INF = -1e30
PAST_LEN = 2048

LANES = 128
SUBLANES = 8
BF16 = jnp.bfloat16
F32 = jnp.float32


_CAND_RUNS = [(a, PEER_TOPK // (a + 1)) for a in range(PEER_TOPK)]
_NCAND = sum(n for _, n in _CAND_RUNS)
_NCAND_PAD = -(-_NCAND // SUBLANES) * SUBLANES
_CAND_FLAT = np.full((_NCAND_PAD,), PEER_TOPK * PEER_TOPK, np.int32)
_off = 0
for _a, _n in _CAND_RUNS:
    _CAND_FLAT[_off:_off + _n] = _a * PEER_TOPK + np.arange(_n)
    _off += _n


def _peer_route_kernel(x_ref, sh_ref, sc_ref, ng_ref, wqt_ref, sk_ref, cand_ref,
                       hn_ref, idx_ref, g_ref,
                       q_s, cs_s, ci_s):
    tt = x_ref.shape[1]
    x = x_ref[0]
    ms = jnp.mean(x * x, axis=-1, keepdims=True)
    hn = x * lax.rsqrt(ms + EPS) * ng_ref[...]
    hn = hn * (1.0 + sc_ref[0]) + sh_ref[0]
    hn_ref[0] = hn
    qt = lax.dot_general(wqt_ref[...], hn.astype(BF16), (((1,), (1,)), ((), ())),
                         preferred_element_type=F32)
    for gidx in range(2 * PEER_HEADS):
        q_s[gidx] = qt[gidx * PEER_HALF:(gidx + 1) * PEER_HALF, :]

    key_id = lax.broadcasted_iota(jnp.int32, (N_KEYS, tt), 0)
    slot_id = lax.broadcasted_iota(jnp.int32, (PEER_TOPK, tt), 0)
    cand = cand_ref[:, 0:tt]
    big = PEER_TOPK * PEER_TOPK
    cs_s[_NCAND:_NCAND_PAD, :] = jnp.full((_NCAND_PAD - _NCAND, tt), -jnp.inf, F32)
    ci_s[_NCAND:_NCAND_PAD, :] = jnp.zeros((_NCAND_PAD - _NCAND, tt), jnp.int32)

    def head_body(h, _):
        s = [jnp.dot(sk_ref[2 * h + p], q_s[2 * h + p].astype(BF16), preferred_element_type=F32)
             for p in range(2)]
        ts = [jnp.zeros((PEER_TOPK, tt), F32) for _ in range(2)]
        ti = [jnp.zeros((PEER_TOPK, tt), jnp.int32) for _ in range(2)]
        for j in range(PEER_TOPK):
            for p in range(2):
                m = jnp.max(s[p], axis=0, keepdims=True)
                am = jnp.min(jnp.where(s[p] == m, key_id, N_KEYS), axis=0, keepdims=True)
                ts[p] = jnp.where(slot_id == j, m, ts[p])
                ti[p] = jnp.where(slot_id == j, am, ti[p])
                s[p] = jnp.where(key_id == am, -jnp.inf, s[p])
        off = 0
        for a, n in _CAND_RUNS:
            cs_s[off:off + n, :] = ts[0][a:a + 1, :] + ts[1][0:n, :]
            ci_s[off:off + n, :] = ti[0][a:a + 1, :] * N_KEYS + ti[1][0:n, :]
            off += n
        cs = cs_s[...]
        ci = ci_s[...]
        fs = jnp.zeros((PEER_TOPK, tt), F32)
        ei = jnp.zeros((PEER_TOPK, tt), jnp.int32)
        for k in range(PEER_TOPK):
            m = jnp.max(cs, axis=0, keepdims=True)
            cmin = jnp.min(jnp.where(cs == m, cand, big), axis=0, keepdims=True)
            sel = cand == cmin
            e = jnp.sum(jnp.where(sel, ci, 0), axis=0, keepdims=True)
            fs = jnp.where(slot_id == k, m, fs)
            ei = jnp.where(slot_id == k, e, ei)
            cs = jnp.where(sel, -jnp.inf, cs)
        ex = jnp.exp(fs - fs[0:1, :])
        g = ex / jnp.sum(ex, axis=0, keepdims=True)
        row0 = pl.multiple_of(h * PEER_TOPK, PEER_TOPK)
        g_ref[0, pl.ds(row0, PEER_TOPK), :] = g
        idx_ref[0, pl.ds(row0, PEER_TOPK), :] = ei
        return 0

    lax.fori_loop(0, PEER_HEADS, head_body, 0)


def _peer_route(x, sh, sc, ng, wqt, sk):
    B, T, D = x.shape
    tt = min(T, 128)
    grid = (B, T // tt)
    tok = lambda b, t: (b, t, 0)
    tok_t = lambda b, t: (b, 0, t)
    per_b = lambda b, t: (b, 0, 0)
    cand_tab = jnp.asarray(np.broadcast_to(_CAND_FLAT[:, None], (_NCAND_PAD, LANES)))
    hn, idx_t, g_t = pl.pallas_call(
        _peer_route_kernel,
        grid=grid,
        in_specs=[
            pl.BlockSpec((1, tt, D), tok),
            pl.BlockSpec((1, 1, D), per_b),
            pl.BlockSpec((1, 1, D), per_b),
            pl.BlockSpec((1, D), lambda b, t: (0, 0)),
            pl.BlockSpec(wqt.shape, lambda b, t: (0, 0)),
            pl.BlockSpec(sk.shape, lambda b, t: (0, 0, 0)),
            pl.BlockSpec(cand_tab.shape, lambda b, t: (0, 0)),
        ],
        out_specs=[
            pl.BlockSpec((1, tt, D), tok),
            pl.BlockSpec((1, PEER_SEL, tt), tok_t),
            pl.BlockSpec((1, PEER_SEL, tt), tok_t),
        ],
        out_shape=[
            jax.ShapeDtypeStruct((B, T, D), F32),
            jax.ShapeDtypeStruct((B, PEER_SEL, T), jnp.int32),
            jax.ShapeDtypeStruct((B, PEER_SEL, T), F32),
        ],
        scratch_shapes=[
            pltpu.VMEM((2 * PEER_HEADS, PEER_HALF, tt), F32),
            pltpu.VMEM((_NCAND_PAD, tt), F32),
            pltpu.VMEM((_NCAND_PAD, tt), jnp.int32),
        ],
        compiler_params=pltpu.CompilerParams(
            dimension_semantics=("parallel", "parallel"), vmem_limit_bytes=48 * 1024 * 1024),
        name="peer_route",
    )(x, sh, sc, ng, wqt, sk, cand_tab)
    return hn, jnp.swapaxes(idx_t, 1, 2), jnp.swapaxes(g_t, 1, 2)


SC_CORES = 2
SC_SUBCORES = 16
SC_LANES = 16
SC_WORKERS = SC_CORES * SC_SUBCORES
NCHUNK = D_MODEL // SC_LANES
HALF_D = D_MODEL // 2
NWCHUNK = HALF_D // SC_LANES
HI_MASK = -65536
TOK_BLOCK = 8
NSLOT = 3


def _pack_expert_tables(u_tab, v_tab):
    def pack(tab):
        bits = lax.bitcast_convert_type(tab.astype(BF16), jnp.uint16).astype(jnp.uint32)
        return bits[:, :HALF_D] | (bits[:, HALF_D:] << 16)

    return lax.bitcast_convert_type(jnp.concatenate([pack(u_tab), pack(v_tab)], axis=1), jnp.int32)


def _unpack_words(words):
    lo = lax.bitcast_convert_type(lax.shift_left(words, jnp.full_like(words, 16)), F32)
    hi = lax.bitcast_convert_type(words & HI_MASK, F32)
    return lo, hi


def _peer_expert_sc(x, hn, eidx, g, g2, uv_tab, tokens_per_seq, n_tokens):
    N, D = n_tokens, x.shape[1]
    tpw = N // SC_WORKERS
    assert tpw * SC_WORKERS == N and tpw % TOK_BLOCK == 0
    assert g2.shape[0] == 1 or tokens_per_seq % tpw == 0
    workers_per_seq = max(tokens_per_seq // tpw, 1) if g2.shape[0] > 1 else SC_WORKERS
    nblocks = tpw // TOK_BLOCK
    steps = TOK_BLOCK * PEER_HEADS
    mesh = plsc.VectorSubcoreMesh(core_axis_name="c", subcore_axis_name="s",
                                  num_cores=SC_CORES, num_subcores=SC_SUBCORES)

    @functools.partial(
        pl.kernel, out_type=jax.ShapeDtypeStruct((N, D), F32), mesh=mesh,
        scratch_types=[
            pltpu.VMEM((TOK_BLOCK, D), F32),
            pltpu.VMEM((TOK_BLOCK, D), F32),
            pltpu.VMEM((TOK_BLOCK, PEER_SEL), jnp.int32),
            pltpu.VMEM((TOK_BLOCK, PEER_SEL), F32),
            pltpu.VMEM((D,), F32),
            pltpu.VMEM((NSLOT, PEER_TOPK, D), jnp.int32),
            pltpu.VMEM((D,), F32),
            pltpu.SemaphoreType.DMA((NSLOT,)),
        ],
        compiler_params=pltpu.CompilerParams(needs_layout_passes=False),
        name="peer_expert_sc",
    )
    def k(x_hbm, hn_hbm, idx_hbm, g_hbm, g2_hbm, uv_hbm, out_hbm,
          hn_b, x_b, idx_b, g_b, g2_v, uvbuf, o_v, sem):
        wid = lax.axis_index("s") * SC_CORES + lax.axis_index("c")
        tok0 = wid * tpw
        pltpu.sync_copy(g2_hbm.at[wid // workers_per_seq], g2_v)
        lane = lax.iota(jnp.int32, SC_LANES)

        def gather_desc(s, slot):
            t = s // PEER_HEADS
            h0 = pl.multiple_of((s % PEER_HEADS) * PEER_TOPK, PEER_TOPK)
            ids = idx_b.at[t, pl.ds(h0, PEER_TOPK)]
            return pltpu.make_async_copy(uv_hbm.at[ids], uvbuf.at[slot], sem.at[slot])

        def issue(s, slot):
            gather_desc(s, slot).start()

        def wait(s, slot):
            gather_desc(s, slot).wait()

        def compute(s, slot):
            t = s // PEER_HEADS
            h = s % PEER_HEADS
            h0 = pl.multiple_of(h * PEER_TOPK, PEER_TOPK)

            @pl.when(h == 0)
            def _():
                @pl.loop(0, NCHUNK)
                def _(c):
                    o_v[pl.ds(pl.multiple_of(c * SC_LANES, SC_LANES), SC_LANES)] = jnp.zeros((SC_LANES,), F32)

            def ubody(c, accs):
                c0 = pl.multiple_of(c * SC_LANES, SC_LANES)
                x_lo = hn_b[t, pl.ds(c0, SC_LANES)]
                x_hi = hn_b[t, pl.ds(HALF_D + c0, SC_LANES)]
                out = []
                for kk in range(PEER_TOPK):
                    lo, hi = _unpack_words(uvbuf[slot, kk, pl.ds(c0, SC_LANES)])
                    out.append(accs[kk] + (x_lo * lo + x_hi * hi))
                return tuple(out)

            accs = lax.fori_loop(0, NWCHUNK, ubody,
                                 tuple(jnp.zeros((SC_LANES,), F32) for _ in range(PEER_TOPK)))
            last = jnp.full((SC_LANES,), SC_LANES - 1, jnp.int32)
            r = jnp.zeros((SC_LANES,), F32)
            for kk in range(PEER_TOPK):
                tot = jnp.take_along_axis(plsc.cumsum(accs[kk]), last, axis=0)
                r = jnp.where(lane == kk, tot, r)
            gv = g_b[t, pl.ds(h0, PEER_TOPK)]
            z = 0.7978845608028654 * (r + 0.044715 * (r * r * r))
            w = gv * r / (1.0 + jnp.exp(-2.0 * z))
            wk = [jnp.take_along_axis(w, jnp.full((SC_LANES,), kk, jnp.int32), axis=0) for kk in range(PEER_TOPK)]

            @plsc.parallel_loop(0, NWCHUNK, unroll=2)
            def _(c):
                c0 = pl.multiple_of(c * SC_LANES, SC_LANES)
                los, his = [], []
                for kk in range(PEER_TOPK):
                    lo, hi = _unpack_words(uvbuf[slot, kk, pl.ds(HALF_D + c0, SC_LANES)])
                    los.append(wk[kk] * lo)
                    his.append(wk[kk] * hi)
                while len(los) > 1:
                    los = [los[i] + los[i + 1] for i in range(0, len(los), 2)]
                    his = [his[i] + his[i + 1] for i in range(0, len(his), 2)]
                plsc.addupdate(o_v.at[pl.ds(c0, SC_LANES)], los[0])
                plsc.addupdate(o_v.at[pl.ds(HALF_D + c0, SC_LANES)], his[0])

            @pl.when(h == PEER_HEADS - 1)
            def _():
                @pl.loop(0, NCHUNK)
                def _(c):
                    c0 = pl.multiple_of(c * SC_LANES, SC_LANES)
                    x_b[t, pl.ds(c0, SC_LANES)] = (x_b[t, pl.ds(c0, SC_LANES)]
                                                   + g2_v[pl.ds(c0, SC_LANES)] * o_v[pl.ds(c0, SC_LANES)])

        @pl.loop(0, nblocks)
        def _(blk):
            tok = pl.multiple_of(tok0 + blk * TOK_BLOCK, TOK_BLOCK)
            pltpu.sync_copy(idx_hbm.at[pl.ds(tok, TOK_BLOCK)], idx_b)
            pltpu.sync_copy(g_hbm.at[pl.ds(tok, TOK_BLOCK)], g_b)
            pltpu.sync_copy(hn_hbm.at[pl.ds(tok, TOK_BLOCK)], hn_b)
            pltpu.sync_copy(x_hbm.at[pl.ds(tok, TOK_BLOCK)], x_b)
            for s0 in range(NSLOT - 1):
                issue(s0, s0)

            @pl.loop(0, steps)
            def _(s):
                @pl.when(s + (NSLOT - 1) < steps)
                def _():
                    issue(s + (NSLOT - 1), (s + (NSLOT - 1)) % NSLOT)

                slot = s % NSLOT
                wait(s, slot)
                compute(s, slot)

            pltpu.sync_copy(x_b, out_hbm.at[pl.ds(tok, TOK_BLOCK)])

    return k(x, hn, eidx, g, g2, uv_tab)


TC_TOK_BLOCK = 64
TC_ISSUE_UNROLL = 32


def _peer_expert_tc_kernel(idx_hbm, x_ref, hn_ref, g_ref, g2_ref, uv_hbm, out_ref,
                           idx_s, uvbuf, sem, isem, *, first_block):
    tb = x_ref.shape[0]
    row0 = (pl.program_id(0) + first_block) * tb
    cp = pltpu.make_async_copy(idx_hbm.at[pl.ds(row0 * PEER_SEL, tb * PEER_SEL)], idx_s, isem)
    cp.start()
    cp.wait()

    def issue(t, slot):
        def body(jb, _):
            base = t * PEER_SEL + jb * TC_ISSUE_UNROLL
            for jj in range(TC_ISSUE_UNROLL):
                pltpu.make_async_copy(uv_hbm.at[pl.ds(idx_s[base + jj], 1)], uvbuf.at[slot, jb, pl.ds(jj, 1)],
                                      sem.at[slot]).start()
            return 0

        lax.fori_loop(0, PEER_SEL // TC_ISSUE_UNROLL, body, 0)

    def wait(slot):
        pltpu.make_async_copy(uv_hbm.at[pl.ds(0, PEER_SEL)], uvbuf.at[slot].reshape(PEER_SEL, D_MODEL),
                              sem.at[slot]).wait()

    issue(0, 0)
    r2 = lax.broadcasted_iota(jnp.int32, (PEER_SEL, LANES), 0)
    c2 = lax.broadcasted_iota(jnp.int32, (PEER_SEL, LANES), 1)
    diag = r2 == c2

    def tok(t, _):
        slot = t % 2

        @pl.when(t + 1 < tb)
        def _():
            issue(t + 1, 1 - slot)

        wait(slot)
        xr = hn_ref[pl.ds(t, 1), :]
        words = uvbuf[slot].reshape(PEER_SEL, D_MODEL)
        u_lo, u_hi = _unpack_words(words[:, :HALF_D])
        prod = u_lo * xr[:, :HALF_D] + u_hi * xr[:, HALF_D:]
        p = prod[:, 0:LANES]
        for c in range(1, HALF_D // LANES):
            p = p + prod[:, c * LANES:(c + 1) * LANES]
        act = jnp.sum(p, axis=-1, keepdims=True)
        grow = g_ref[pl.ds(t, 1), :]
        gcol = jnp.sum(jnp.where(diag, jnp.broadcast_to(grow, (PEER_SEL, LANES)), 0.0),
                       axis=-1, keepdims=True)
        wgt = gcol * _gelu_tanh(act)
        v_lo, v_hi = _unpack_words(words[:, HALF_D:])
        o = jnp.concatenate([jnp.sum(v_lo * wgt, axis=0, keepdims=True),
                             jnp.sum(v_hi * wgt, axis=0, keepdims=True)], axis=-1)
        out_ref[pl.ds(t, 1), :] = x_ref[pl.ds(t, 1), :] + g2_ref[...] * o
        return 0

    lax.fori_loop(0, tb, tok, 0)


def _peer_expert_tc(x, hn, eidx, g, g2, uv_tab, first_token):
    N, D = x.shape
    tb = TC_TOK_BLOCK
    n_out = N - first_token
    assert first_token % tb == 0 and n_out % tb == 0
    first_block = first_token // tb
    tok = lambda t: (t + first_block, 0)
    return pl.pallas_call(
        functools.partial(_peer_expert_tc_kernel, first_block=first_block),
        grid=(n_out // tb,),
        in_specs=[
            pl.BlockSpec(memory_space=pl.ANY),
            pl.BlockSpec((tb, D), tok),
            pl.BlockSpec((tb, D), tok),
            pl.BlockSpec((tb, PEER_SEL), tok),
            pl.BlockSpec((1, D), lambda t: (0, 0)),
            pl.BlockSpec(memory_space=pl.ANY),
        ],
        out_specs=pl.BlockSpec((tb, D), lambda t: (t, 0)),
        out_shape=jax.ShapeDtypeStruct((n_out, D), F32),
        scratch_shapes=[
            pltpu.SMEM((tb * PEER_SEL,), jnp.int32),
            pltpu.VMEM((2, PEER_SEL // TC_ISSUE_UNROLL, TC_ISSUE_UNROLL, D), jnp.int32),
            pltpu.SemaphoreType.DMA((2,)),
            pltpu.SemaphoreType.DMA,
        ],
        compiler_params=pltpu.CompilerParams(
            dimension_semantics=("arbitrary",), vmem_limit_bytes=32 * 1024 * 1024,
            disable_bounds_checks=True),
        name="peer_expert_tc",
    )(eidx.reshape(-1), x, hn, g, g2, uv_tab)


TC_SHARE_TOKENS = 1536


def _peer_layer(x, sh2, sc2, g2, ng, wqt, sk, uv_tab):
    B, T, D = x.shape
    hn, eidx, g = _peer_route(x, sh2, sc2, ng, wqt, sk)
    N = B * T
    flat = (x.reshape(N, D), hn.reshape(N, D), eidx.reshape(N, PEER_SEL), g.reshape(N, PEER_SEL),
            g2.reshape(B, D), uv_tab)
    n_tc = TC_SHARE_TOKENS if (B == 1 and N >= 4 * TC_SHARE_TOKENS) else 0
    y = _peer_expert_sc(*flat, tokens_per_seq=T, n_tokens=N - n_tc)
    if n_tc:
        y = jnp.concatenate([y, _peer_expert_tc(*flat, first_token=N - n_tc)], axis=0)
    return y.reshape(B, T, D)


def _attn_layer_kernel(*refs, tq, ch, has_cache):
    if has_cache:
        (x_ref, sh_ref, sc_ref, g1_ref, ng_ref, wqkv_ref, qg_ref, kg_ref, bias_ref, wo_ref, kc_ref, vc_ref,
         y_ref, ko_ref, vo_ref, kcat, vcat, qn_s, o_s) = refs
    else:
        (x_ref, sh_ref, sc_ref, g1_ref, ng_ref, wqkv_ref, qg_ref, kg_ref, bias_ref, wo_ref,
         y_ref, ko_ref, vo_ref, kcat, vcat, qn_s, o_s) = refs
    t = pl.program_id(1)
    band = LEFT_ROWS + ch

    if has_cache:
        kcat[0:LEFT_ROWS, :] = kc_ref[0].astype(BF16)
        vcat[0:LEFT_ROWS, :] = vc_ref[0].astype(BF16)
    else:
        @pl.when(t == 0)
        def _():
            kcat[0:LEFT_ROWS, :] = jnp.zeros((LEFT_ROWS, D_MODEL), BF16)
            vcat[0:LEFT_ROWS, :] = jnp.zeros((LEFT_ROWS, D_MODEL), BF16)

    x = x_ref[0]
    ms = jnp.mean(x * x, axis=-1, keepdims=True)
    hn = x * lax.rsqrt(ms + EPS) * ng_ref[...]
    hn = hn * (1.0 + sc_ref[0]) + sh_ref[0]
    qkv = jnp.dot(hn.astype(BF16), wqkv_ref[...], preferred_element_type=F32)
    for h in range(N_HEADS):
        lo = h * HEAD_DIM
        qh = qkv[:, lo:lo + HEAD_DIM]
        kh = qkv[:, D_MODEL + lo:D_MODEL + lo + HEAD_DIM]
        qh = qh * lax.rsqrt(jnp.mean(qh * qh, axis=-1, keepdims=True) + EPS) * qg_ref[...]
        kh = kh * lax.rsqrt(jnp.mean(kh * kh, axis=-1, keepdims=True) + EPS) * kg_ref[...]
        qn_s[:, lo:lo + HEAD_DIM] = qh.astype(BF16)
        ko_ref[0, :, lo:lo + HEAD_DIM] = kh
        kcat[LEFT_ROWS:LEFT_ROWS + tq, lo:lo + HEAD_DIM] = kh.astype(BF16)
    v = qkv[:, 2 * D_MODEL:]
    vo_ref[0] = v
    vcat[LEFT_ROWS:LEFT_ROWS + tq, :] = v.astype(BF16)

    scale = HEAD_DIM ** -0.5
    kpos_rel = lax.broadcasted_iota(jnp.int32, (ch, band), 1)

    def chunk_body(c, _):
        r0 = pl.multiple_of(c * ch, ch)
        if not has_cache:
            valid = kpos_rel + (t * tq + r0) >= LEFT_ROWS
        for h in range(N_HEADS):
            lo = h * HEAD_DIM
            q = qn_s[pl.ds(r0, ch), lo:lo + HEAD_DIM]
            kb = kcat[pl.ds(r0, band), lo:lo + HEAD_DIM]
            vb = vcat[pl.ds(r0, band), lo:lo + HEAD_DIM]
            s = lax.dot_general(q, kb, (((1,), (1,)), ((), ())), preferred_element_type=F32) * scale
            s = s + bias_ref[h, 0:ch, 0:band]
            if not has_cache:
                s = jnp.where(valid, s, NEG_INF)
            m = jnp.max(s, axis=-1, keepdims=True)
            e = jnp.exp(s - m)
            p = e / jnp.sum(e, axis=-1, keepdims=True)
            o_s[pl.ds(r0, ch), lo:lo + HEAD_DIM] = jnp.dot(p.astype(BF16), vb, preferred_element_type=F32).astype(BF16)
        return 0

    lax.fori_loop(0, tq // ch, chunk_body, 0)

    out = jnp.dot(o_s[...], wo_ref[...], preferred_element_type=F32)
    y_ref[0] = x + g1_ref[0] * out

    if not has_cache:
        kcat[0:LEFT_ROWS, :] = kcat[tq:tq + LEFT_ROWS, :]
        vcat[0:LEFT_ROWS, :] = vcat[tq:tq + LEFT_ROWS, :]


def _rel_bias_tile(rel_bias):
    i = jnp.arange(CHUNK)[:, None]
    r = jnp.arange(LEFT_ROWS + CHUNK)[None, :]
    rel = jnp.clip(i + LEFT_ROWS - r, -REL_CLIP, REL_CLIP) + REL_CLIP
    return rel_bias[:, rel]


def _attn_layer(x, sh, sc, g1, ng, wqkv, qg, kg, bias, wo, k_cache=None, v_cache=None):
    B, T, D = x.shape
    has_cache = k_cache is not None
    if has_cache:
        tq, ch = T, T
        keep_blocks = 1
    else:
        tq, ch = 256, CHUNK
        keep_blocks = LEFT_ROWS // tq
    nt = T // tq
    assert nt >= keep_blocks
    tok = lambda b, t: (b, t, 0)
    per_b = lambda b, t: (b, 0, 0)
    c2 = lambda b, t: (0, 0)
    c3 = lambda b, t: (0, 0, 0)
    keep = lambda b, t: (b, jnp.maximum(t - (nt - keep_blocks), 0), 0)
    in_specs = [
        pl.BlockSpec((1, tq, D), tok),
        pl.BlockSpec((1, 1, D), per_b), pl.BlockSpec((1, 1, D), per_b), pl.BlockSpec((1, 1, D), per_b),
        pl.BlockSpec((1, D), c2),
        pl.BlockSpec(wqkv.shape, c2),
        pl.BlockSpec((1, HEAD_DIM), c2), pl.BlockSpec((1, HEAD_DIM), c2),
        pl.BlockSpec(bias.shape, c3),
        pl.BlockSpec(wo.shape, c2),
    ]
    args = [x, sh, sc, g1, ng, wqkv, qg, kg, bias, wo]
    if has_cache:
        in_specs += [pl.BlockSpec((1, LEFT_ROWS, D), per_b), pl.BlockSpec((1, LEFT_ROWS, D), per_b)]
        args += [k_cache, v_cache]
    return pl.pallas_call(
        functools.partial(_attn_layer_kernel, tq=tq, ch=ch, has_cache=has_cache),
        grid=(B, nt),
        in_specs=in_specs,
        out_specs=[pl.BlockSpec((1, tq, D), tok), pl.BlockSpec((1, tq, D), keep), pl.BlockSpec((1, tq, D), keep)],
        out_shape=[jax.ShapeDtypeStruct((B, T, D), F32),
                   jax.ShapeDtypeStruct((B, keep_blocks * tq, D), F32),
                   jax.ShapeDtypeStruct((B, keep_blocks * tq, D), F32)],
        scratch_shapes=[
            pltpu.VMEM((LEFT_ROWS + tq, D), BF16),
            pltpu.VMEM((LEFT_ROWS + tq, D), BF16),
            pltpu.VMEM((tq, D), BF16),
            pltpu.VMEM((tq, D), BF16),
        ],
        compiler_params=pltpu.CompilerParams(
            dimension_semantics=("parallel", "arbitrary"), vmem_limit_bytes=48 * 1024 * 1024),
        name="attn_layer",
    )(*args)


def _gelu_tanh(x):
    return 0.5 * x * (1.0 + jnp.tanh(0.7978845608028654 * (x + 0.044715 * (x * x * x))))


def _lru_layer_kernel(x_ref, sh_ref, sc_ref, g1_ref, ng_ref, win_ref, cw_ref, cb_ref, gw_ref, gb_ref, lam_ref,
                      wout_ref, cs0_ref, h0_ref,
                      y_ref, cs_ref, hl_ref,
                      ext_s, a_s, b_s, hcar_s):
    tt = x_ref.shape[1]
    W = LRU_WIDTH
    t = pl.program_id(1)

    @pl.when(t == 0)
    def _():
        ext_s[0:SUBLANES, :] = cs0_ref[0]
        hcar_s[...] = h0_ref[0]

    x = x_ref[0]
    ms = jnp.mean(x * x, axis=-1, keepdims=True)
    hn = x * lax.rsqrt(ms + EPS) * ng_ref[...]
    hn = hn * (1.0 + sc_ref[0]) + sh_ref[0]
    proj = jnp.dot(hn.astype(BF16), win_ref[...], preferred_element_type=F32)
    gate = _gelu_tanh(proj[:, :W])
    rec = proj[:, W:]
    ext_s[SUBLANES:SUBLANES + tt, :] = rec
    u = cb_ref[...] + cw_ref[3:4, :] * rec
    for k in range(CONV_W - 1):
        u = u + cw_ref[k:k + 1, :] * ext_s[SUBLANES - 3 + k:SUBLANES - 3 + k + tt, :]
    tail = ext_s[tt:tt + SUBLANES, :]
    cs_ref[0] = tail
    ext_s[0:SUBLANES, :] = tail

    sp_in = -lam_ref[...]
    softplus = jnp.maximum(sp_in, 0.0) + jnp.log1p(jnp.exp(-jnp.abs(sp_in)))
    for n in range(LRU_BLOCKS):
        lo = n * LRU_BLOCK_W
        ub = u[:, lo:lo + LRU_BLOCK_W]
        ubb = ub.astype(BF16)
        gr = jnp.dot(ubb, gw_ref[0, n], preferred_element_type=F32) + gb_ref[0:1, lo:lo + LRU_BLOCK_W]
        gi = jnp.dot(ubb, gw_ref[1, n], preferred_element_type=F32) + gb_ref[1:2, lo:lo + LRU_BLOCK_W]
        r = jax.nn.sigmoid(gr)
        i = jax.nn.sigmoid(gi)
        log_a = -RG_C * r * softplus[:, lo:lo + LRU_BLOCK_W]
        a = jnp.exp(log_a)
        a_s[:, lo:lo + LRU_BLOCK_W] = a
        b_s[:, lo:lo + LRU_BLOCK_W] = jnp.sqrt(1.0 - a * a) * i * ub

    def group(gi_, h):
        r0 = pl.multiple_of(gi_ * SUBLANES, SUBLANES)
        a8 = a_s[pl.ds(r0, SUBLANES), :]
        b8 = b_s[pl.ds(r0, SUBLANES), :]
        rows = []
        for j in range(SUBLANES):
            h = a8[j:j + 1, :] * h + b8[j:j + 1, :]
            rows.append(h)
        b_s[pl.ds(r0, SUBLANES), :] = jnp.concatenate(rows, axis=0)
        return h

    h_last = lax.fori_loop(0, tt // SUBLANES, group, hcar_s[...])
    hcar_s[...] = h_last
    hl_ref[0] = h_last
    yv = b_s[...] * gate
    out = jnp.dot(yv.astype(BF16), wout_ref[...], preferred_element_type=F32)
    y_ref[0] = x + g1_ref[0] * out


def _lru_layer(x, sh, sc, g1, ng, w_in, conv_w, conv_b, gate_w, gate_b, lam, w_out, conv_state, h0):
    B, T, D = x.shape
    W = LRU_WIDTH
    tt = min(T, 256)
    cs0 = jnp.pad(conv_state, ((0, 0), (SUBLANES - (CONV_W - 1), 0), (0, 0)))
    tok = lambda b, t: (b, t, 0)
    per_b = lambda b, t: (b, 0, 0)
    c2 = lambda b, t: (0, 0)
    y, cs, hl = pl.pallas_call(
        _lru_layer_kernel,
        grid=(B, T // tt),
        in_specs=[
            pl.BlockSpec((1, tt, D), tok),
            pl.BlockSpec((1, 1, D), per_b), pl.BlockSpec((1, 1, D), per_b), pl.BlockSpec((1, 1, D), per_b),
            pl.BlockSpec((1, D), c2),
            pl.BlockSpec(w_in.shape, c2),
            pl.BlockSpec((CONV_W, W), c2), pl.BlockSpec((1, W), c2),
            pl.BlockSpec(gate_w.shape, lambda b, t: (0, 0, 0, 0)),
            pl.BlockSpec((2, W), c2), pl.BlockSpec((1, W), c2),
            pl.BlockSpec(w_out.shape, c2),
            pl.BlockSpec((1, SUBLANES, W), per_b), pl.BlockSpec((1, 1, W), per_b),
        ],
        out_specs=[pl.BlockSpec((1, tt, D), tok), pl.BlockSpec((1, SUBLANES, W), per_b),
                   pl.BlockSpec((1, 1, W), per_b)],
        out_shape=[jax.ShapeDtypeStruct((B, T, D), F32), jax.ShapeDtypeStruct((B, SUBLANES, W), F32),
                   jax.ShapeDtypeStruct((B, 1, W), F32)],
        scratch_shapes=[
            pltpu.VMEM((tt + SUBLANES, W), F32),
            pltpu.VMEM((tt, W), F32),
            pltpu.VMEM((tt, W), F32),
            pltpu.VMEM((1, W), F32),
        ],
        compiler_params=pltpu.CompilerParams(
            dimension_semantics=("parallel", "arbitrary"), vmem_limit_bytes=48 * 1024 * 1024),
        name="lru_layer",
    )(x, sh, sc, g1, ng, w_in, conv_w, conv_b, gate_w, gate_b, lam, w_out, cs0, h0[:, None, :])
    return y, cs[:, SUBLANES - (CONV_W - 1):], hl[:, 0]


def _ada_kernel(c_ref, w_ref, b_ref, m_ref):
    c = c_ref[...]
    s = c * jax.nn.sigmoid(c)
    m_ref[0] = jnp.dot(s, w_ref[0], preferred_element_type=F32, precision=lax.Precision.HIGHEST) + b_ref[0]


def _ada_all(c_all, ada_w, ada_b):
    L, D, E = ada_w.shape
    R = c_all.shape[0]
    tn = 1536
    return pl.pallas_call(
        _ada_kernel,
        grid=(L, E // tn),
        in_specs=[pl.BlockSpec((R, D), lambda l, j: (0, 0)),
                  pl.BlockSpec((1, D, tn), lambda l, j: (l, 0, j)),
                  pl.BlockSpec((1, 1, tn), lambda l, j: (l, 0, j))],
        out_specs=pl.BlockSpec((1, R, tn), lambda l, j: (l, 0, j)),
        out_shape=jax.ShapeDtypeStruct((L, R, E), F32),
        compiler_params=pltpu.CompilerParams(
            dimension_semantics=("parallel", "parallel"), vmem_limit_bytes=32 * 1024 * 1024),
        name="ada_mod",
    )(c_all, ada_w, ada_b[:, None, :])


class _Chain:
    def __init__(self, x, mods, conv_st, lru_st, k_cache, v_cache, w, prompt):
        self.x, self.mods, self.w, self.prompt = x, mods, w, prompt
        self.conv_st, self.lru_st, self.k_cache, self.v_cache = conv_st, lru_st, k_cache, v_cache
        self.new_conv, self.new_h, self.new_k, self.new_v = [], [], [], []
        self.layer = 0

    def step(self):
        i, w, x = self.layer, self.w, self.x
        B, T, D = x.shape
        sh1, sc1, g1, sh2, sc2, g2 = self.mods[i]
        j = i // 2
        if i % 2 == 0:
            if self.prompt:
                cs0 = jnp.zeros((B, CONV_W - 1, LRU_WIDTH), x.dtype)
                h0 = jnp.zeros((B, LRU_WIDTH), F32)
            else:
                cs0, h0 = self.conv_st[j], self.lru_st[j]
            x, cnew, hnew = _lru_layer(x, sh1, sc1, g1, w['norm_g'][i, 0][None, :], w['lru_w_in'][j],
                                       w['lru_conv_w'][j], w['lru_conv_b'][j][None, :], w['lru_gate_w'][j],
                                       w['lru_gate_b'][j], w['lru_lambda'][j][None, :], w['lru_w_out'][j], cs0, h0)
            self.new_conv.append(cnew)
            self.new_h.append(hnew)
        else:
            if self.prompt:
                kc = vc = None
            else:
                kc = self.k_cache[j].reshape(B, LEFT_ROWS, D)
                vc = self.v_cache[j].reshape(B, LEFT_ROWS, D)
            x, kn, vn = _attn_layer(x, sh1, sc1, g1, w['norm_g'][i, 0][None, :], w['att_w_qkv'][j],
                                    w['att_q_gain'][j][None, :], w['att_k_gain'][j][None, :], w['att_bias'][j],
                                    w['att_w_o'][j], kc, vc)
            self.new_k.append(kn.reshape(B, kn.shape[1], N_HEADS, HEAD_DIM))
            self.new_v.append(vn.reshape(B, vn.shape[1], N_HEADS, HEAD_DIM))
        self.x = _peer_layer(x, sh2, sc2, g2, w['norm_g'][i, 1][None, :], w['peer_wqt'][i], w['peer_sk'][i],
                             w['peer_uv'][i])
        self.layer += 1

    def results(self):
        return (self.x, jnp.stack(self.new_conv), jnp.stack(self.new_h), jnp.stack(self.new_k),
                jnp.stack(self.new_v))


def _run_chains(chains, depth):
    for wave in range(depth + len(chains) - 1):
        for ci, chain in enumerate(chains):
            if 0 <= wave - ci < depth:
                chain.step()


def kernel(x_prompt, x_sample, c_prompt, c_sample, state_conv, state_lru_h, cache_k, cache_v, norm_g, ada_w, ada_b, lru_w_in, lru_conv_w, lru_conv_b, lru_gate_w, lru_gate_b, lru_lambda, lru_w_out, att_w_qkv, att_q_gain, att_k_gain, att_rel_bias, att_w_o, peer_w_query, peer_sub_keys, peer_u, peer_v):
    depth = norm_g.shape[0]
    bp, bs = c_prompt.shape[0], c_sample.shape[0]
    rows = -(-(bp + bs) // SUBLANES) * SUBLANES
    c_all = jnp.concatenate([c_prompt, c_sample, jnp.zeros((rows - bp - bs, D_MODEL), F32)], axis=0)
    m_all = _ada_all(c_all, ada_w, ada_b)

    def mods(lo, n):
        return [[m_all[i, lo:lo + n, None, k * D_MODEL:(k + 1) * D_MODEL] for k in range(6)] for i in range(depth)]

    w = {
        'norm_g': norm_g,
        'lru_w_in': lru_w_in.astype(BF16), 'lru_conv_w': lru_conv_w, 'lru_conv_b': lru_conv_b,
        'lru_gate_w': lru_gate_w.astype(BF16), 'lru_gate_b': lru_gate_b, 'lru_lambda': lru_lambda,
        'lru_w_out': lru_w_out.astype(BF16),
        'att_w_qkv': att_w_qkv.astype(BF16), 'att_q_gain': att_q_gain, 'att_k_gain': att_k_gain,
        'att_bias': jax.vmap(_rel_bias_tile)(att_rel_bias), 'att_w_o': att_w_o.astype(BF16),
        'peer_wqt': jnp.swapaxes(peer_w_query, 1, 2).astype(BF16),
        'peer_sk': peer_sub_keys.astype(BF16).reshape(depth, 2 * PEER_HEADS, N_KEYS, PEER_HALF),
        'peer_uv': [_pack_expert_tables(peer_u[i], peer_v[i]) for i in range(depth)],
    }
    prompt_chains = [_Chain(x_prompt[b:b + 1], mods(b, 1), None, None, None, None, w, True) for b in range(bp)]
    sample_chain = _Chain(x_sample, mods(bp, bs), state_conv, state_lru_h, cache_k, cache_v, w, False)
    _run_chains(prompt_chains + [sample_chain], depth)
    y_prompt, p_conv, p_h, p_k, p_v = (
        jnp.concatenate(parts, axis=ax)
        for parts, ax in zip(zip(*(c.results() for c in prompt_chains)), (0, 1, 1, 1, 1)))
    y_sample, s_conv, s_h, s_k, s_v = sample_chain.results()
    return (y_prompt, y_sample, p_conv, p_h, p_k, p_v, s_conv, s_h, s_k, s_v)
```

```python
import functools

import jax
import jax.numpy as jnp
import numpy as np
from jax import lax
from jax.experimental import pallas as pl
from jax.experimental.pallas import tpu as pltpu
from jax.experimental.pallas import tpu_sc as plsc

D_MODEL = 1024
CHUNK = 64
LRU_WIDTH = D_MODEL
LRU_BLOCKS = 4
LRU_BLOCK_W = LRU_WIDTH // LRU_BLOCKS
CONV_W = 4
RG_C = 8.0
N_HEADS = 8
HEAD_DIM = D_MODEL // N_HEADS
LEFT_CHUNKS = 8
LEFT_ROWS = LEFT_CHUNKS * CHUNK
REL_CLIP = 128
PEER_HEADS = 8
N_KEYS = 128
PEER_TOPK = 16
PEER_QDIM = 256
PEER_HALF = PEER_QDIM // 2
PEER_SEL = PEER_HEADS * PEER_TOPK
EPS = 1e-6
NEG_INF = -1e30

LANES = 128
SUBLANES = 8
V7X_VMEM_BYTES = 64 * 1024 * 1024
VMEM_LIMIT_LAYER = (3 * V7X_VMEM_BYTES) // 4
VMEM_LIMIT_SMALL = V7X_VMEM_BYTES // 2
ROUTE_TILE = 512
LAYER_TILE = 256
ADA_TILE = 1536
BF16 = jnp.bfloat16
F32 = jnp.float32


_CAND_RUNS = [(a, PEER_TOPK // (a + 1)) for a in range(PEER_TOPK)]
_NCAND = sum(n for _, n in _CAND_RUNS)
_NCAND_PAD = -(-_NCAND // SUBLANES) * SUBLANES
_CAND_FLAT = np.full((_NCAND_PAD,), PEER_TOPK * PEER_TOPK, np.int32)
_off = 0
for _a, _n in _CAND_RUNS:
    _CAND_FLAT[_off:_off + _n] = _a * PEER_TOPK + np.arange(_n)
    _off += _n


def _peer_route_kernel(x_ref, sh_ref, sc_ref, ng_ref, wqt_ref, sk_ref, cand_ref,
                       hn_ref, idx_ref, g_ref,
                       q_s, cs_s, ci_s):
    tt = x_ref.shape[1]
    x = x_ref[0]
    ms = jnp.mean(x * x, axis=-1, keepdims=True)
    hn = x * lax.rsqrt(ms + EPS) * ng_ref[...]
    hn = hn * (1.0 + sc_ref[0]) + sh_ref[0]
    hn_ref[0] = hn
    qt = lax.dot_general(wqt_ref[...], hn.astype(BF16), (((1,), (1,)), ((), ())),
                         preferred_element_type=F32)
    for gidx in range(2 * PEER_HEADS):
        q_s[gidx] = qt[gidx * PEER_HALF:(gidx + 1) * PEER_HALF, :]

    key_id = lax.broadcasted_iota(jnp.int32, (N_KEYS, tt), 0)
    slot_id = lax.broadcasted_iota(jnp.int32, (PEER_TOPK, tt), 0)
    cand = cand_ref[:, 0:tt]
    big = PEER_TOPK * PEER_TOPK
    cs_s[_NCAND:_NCAND_PAD, :] = jnp.full((_NCAND_PAD - _NCAND, tt), -jnp.inf, F32)
    ci_s[_NCAND:_NCAND_PAD, :] = jnp.zeros((_NCAND_PAD - _NCAND, tt), jnp.int32)

    def head_body(h, _):
        s = [jnp.dot(sk_ref[2 * h + p], q_s[2 * h + p].astype(BF16), preferred_element_type=F32)
             for p in range(2)]
        ts = [jnp.zeros((PEER_TOPK, tt), F32) for _ in range(2)]
        ti = [jnp.zeros((PEER_TOPK, tt), jnp.int32) for _ in range(2)]
        for j in range(PEER_TOPK):
            for p in range(2):
                m = jnp.max(s[p], axis=0, keepdims=True)
                am = jnp.min(jnp.where(s[p] == m, key_id, N_KEYS), axis=0, keepdims=True)
                ts[p] = jnp.where(slot_id == j, m, ts[p])
                ti[p] = jnp.where(slot_id == j, am, ti[p])
                s[p] = jnp.where(key_id == am, -jnp.inf, s[p])
        off = 0
        for a, n in _CAND_RUNS:
            cs_s[off:off + n, :] = ts[0][a:a + 1, :] + ts[1][0:n, :]
            ci_s[off:off + n, :] = ti[0][a:a + 1, :] * N_KEYS + ti[1][0:n, :]
            off += n
        cs = cs_s[...]
        ci = ci_s[...]
        fs = jnp.zeros((PEER_TOPK, tt), F32)
        ei = jnp.zeros((PEER_TOPK, tt), jnp.int32)
        for k in range(PEER_TOPK):
            m = jnp.max(cs, axis=0, keepdims=True)
            cmin = jnp.min(jnp.where(cs == m, cand, big), axis=0, keepdims=True)
            sel = cand == cmin
            e = jnp.sum(jnp.where(sel, ci, 0), axis=0, keepdims=True)
            fs = jnp.where(slot_id == k, m, fs)
            ei = jnp.where(slot_id == k, e, ei)
            cs = jnp.where(sel, -jnp.inf, cs)
        ex = jnp.exp(fs - fs[0:1, :])
        g = ex / jnp.sum(ex, axis=0, keepdims=True)
        row0 = pl.multiple_of(h * PEER_TOPK, PEER_TOPK)
        g_ref[0, pl.ds(row0, PEER_TOPK), :] = g
        idx_ref[0, pl.ds(row0, PEER_TOPK), :] = ei
        return 0

    lax.fori_loop(0, PEER_HEADS, head_body, 0)


def _peer_route(x, sh, sc, ng, wqt, sk):
    B, T, D = x.shape
    tt = min(T, ROUTE_TILE)
    grid = (B, T // tt)
    tok = lambda b, t: (b, t, 0)
    tok_t = lambda b, t: (b, 0, t)
    per_b = lambda b, t: (b, 0, 0)
    cand_tab = jnp.asarray(np.broadcast_to(_CAND_FLAT[:, None], (_NCAND_PAD, max(tt, LANES))))
    hn, idx_t, g_t = pl.pallas_call(
        _peer_route_kernel,
        grid=grid,
        in_specs=[
            pl.BlockSpec((1, tt, D), tok),
            pl.BlockSpec((1, 1, D), per_b),
            pl.BlockSpec((1, 1, D), per_b),
            pl.BlockSpec((1, D), lambda b, t: (0, 0)),
            pl.BlockSpec(wqt.shape, lambda b, t: (0, 0)),
            pl.BlockSpec(sk.shape, lambda b, t: (0, 0, 0)),
            pl.BlockSpec(cand_tab.shape, lambda b, t: (0, 0)),
        ],
        out_specs=[
            pl.BlockSpec((1, tt, D), tok),
            pl.BlockSpec((1, PEER_SEL, tt), tok_t),
            pl.BlockSpec((1, PEER_SEL, tt), tok_t),
        ],
        out_shape=[
            jax.ShapeDtypeStruct((B, T, D), F32),
            jax.ShapeDtypeStruct((B, PEER_SEL, T), jnp.int32),
            jax.ShapeDtypeStruct((B, PEER_SEL, T), F32),
        ],
        scratch_shapes=[
            pltpu.VMEM((2 * PEER_HEADS, PEER_HALF, tt), F32),
            pltpu.VMEM((_NCAND_PAD, tt), F32),
            pltpu.VMEM((_NCAND_PAD, tt), jnp.int32),
        ],
        compiler_params=pltpu.CompilerParams(
            dimension_semantics=("parallel", "parallel"), vmem_limit_bytes=VMEM_LIMIT_LAYER),
        name="peer_route",
    )(x, sh, sc, ng, wqt, sk, cand_tab)
    return hn, jnp.swapaxes(idx_t, 1, 2), jnp.swapaxes(g_t, 1, 2)


SC_CORES = 2
SC_SUBCORES = 16
SC_LANES = 16
SC_WORKERS = SC_CORES * SC_SUBCORES
NCHUNK = D_MODEL // SC_LANES
HALF_D = D_MODEL // 2
NWCHUNK = HALF_D // SC_LANES
HI_MASK = -65536
TOK_BLOCK = 8
NSLOT = 3


def _pack_expert_tables(u_tab, v_tab):
    def pack(tab):
        bits = lax.bitcast_convert_type(tab.astype(BF16), jnp.uint16).astype(jnp.uint32)
        return bits[:, :HALF_D] | (bits[:, HALF_D:] << 16)

    return lax.bitcast_convert_type(jnp.concatenate([pack(u_tab), pack(v_tab)], axis=1), jnp.int32)


def _unpack_words(words):
    lo = lax.bitcast_convert_type(lax.shift_left(words, jnp.full_like(words, 16)), F32)
    hi = lax.bitcast_convert_type(words & HI_MASK, F32)
    return lo, hi


def _peer_expert_sc(x, hn, eidx, g, g2, uv_tab, tokens_per_seq, n_tokens):
    N, D = n_tokens, x.shape[1]
    tpw = N // SC_WORKERS
    assert tpw * SC_WORKERS == N and tpw % TOK_BLOCK == 0
    assert g2.shape[0] == 1 or tokens_per_seq % tpw == 0
    workers_per_seq = max(tokens_per_seq // tpw, 1) if g2.shape[0] > 1 else SC_WORKERS
    nblocks = tpw // TOK_BLOCK
    steps = TOK_BLOCK * PEER_HEADS
    mesh = plsc.VectorSubcoreMesh(core_axis_name="c", subcore_axis_name="s",
                                  num_cores=SC_CORES, num_subcores=SC_SUBCORES)

    @functools.partial(
        pl.kernel, out_type=jax.ShapeDtypeStruct((N, D), F32), mesh=mesh,
        scratch_types=[
            pltpu.VMEM((TOK_BLOCK, D), F32),
            pltpu.VMEM((TOK_BLOCK, D), F32),
            pltpu.VMEM((TOK_BLOCK, PEER_SEL), jnp.int32),
            pltpu.VMEM((TOK_BLOCK, PEER_SEL), F32),
            pltpu.VMEM((D,), F32),
            pltpu.VMEM((NSLOT, PEER_TOPK, D), jnp.int32),
            pltpu.VMEM((D,), F32),
            pltpu.SemaphoreType.DMA((NSLOT,)),
        ],
        compiler_params=pltpu.CompilerParams(needs_layout_passes=False),
        name="peer_expert_sc",
    )
    def k(x_hbm, hn_hbm, idx_hbm, g_hbm, g2_hbm, uv_hbm, out_hbm,
          hn_b, x_b, idx_b, g_b, g2_v, uvbuf, o_v, sem):
        wid = lax.axis_index("s") * SC_CORES + lax.axis_index("c")
        tok0 = wid * tpw
        pltpu.sync_copy(g2_hbm.at[wid // workers_per_seq], g2_v)
        lane = lax.iota(jnp.int32, SC_LANES)

        def gather_desc(s, slot):
            t = s // PEER_HEADS
            h0 = pl.multiple_of((s % PEER_HEADS) * PEER_TOPK, PEER_TOPK)
            ids = idx_b.at[t, pl.ds(h0, PEER_TOPK)]
            return pltpu.make_async_copy(uv_hbm.at[ids], uvbuf.at[slot], sem.at[slot])

        def issue(s, slot):
            gather_desc(s, slot).start()

        def wait(s, slot):
            gather_desc(s, slot).wait()

        def compute(s, slot):
            t = s // PEER_HEADS
            h = s % PEER_HEADS
            h0 = pl.multiple_of(h * PEER_TOPK, PEER_TOPK)

            @pl.when(h == 0)
            def _():
                @pl.loop(0, NCHUNK)
                def _(c):
                    o_v[pl.ds(pl.multiple_of(c * SC_LANES, SC_LANES), SC_LANES)] = jnp.zeros((SC_LANES,), F32)

            def ubody(c, accs):
                c0 = pl.multiple_of(c * SC_LANES, SC_LANES)
                x_lo = hn_b[t, pl.ds(c0, SC_LANES)]
                x_hi = hn_b[t, pl.ds(HALF_D + c0, SC_LANES)]
                out = []
                for kk in range(PEER_TOPK):
                    lo, hi = _unpack_words(uvbuf[slot, kk, pl.ds(c0, SC_LANES)])
                    out.append(accs[kk] + (x_lo * lo + x_hi * hi))
                return tuple(out)

            accs = lax.fori_loop(0, NWCHUNK, ubody,
                                 tuple(jnp.zeros((SC_LANES,), F32) for _ in range(PEER_TOPK)))
            last = jnp.full((SC_LANES,), SC_LANES - 1, jnp.int32)
            r = jnp.zeros((SC_LANES,), F32)
            for kk in range(PEER_TOPK):
                tot = jnp.take_along_axis(plsc.cumsum(accs[kk]), last, axis=0)
                r = jnp.where(lane == kk, tot, r)
            gv = g_b[t, pl.ds(h0, PEER_TOPK)]
            z = 0.7978845608028654 * (r + 0.044715 * (r * r * r))
            w = gv * r / (1.0 + jnp.exp(-2.0 * z))
            wk = [jnp.take_along_axis(w, jnp.full((SC_LANES,), kk, jnp.int32), axis=0) for kk in range(PEER_TOPK)]

            @plsc.parallel_loop(0, NWCHUNK, unroll=2)
            def _(c):
                c0 = pl.multiple_of(c * SC_LANES, SC_LANES)
                los, his = [], []
                for kk in range(PEER_TOPK):
                    lo, hi = _unpack_words(uvbuf[slot, kk, pl.ds(HALF_D + c0, SC_LANES)])
                    los.append(wk[kk] * lo)
                    his.append(wk[kk] * hi)
                while len(los) > 1:
                    los = [los[i] + los[i + 1] for i in range(0, len(los), 2)]
                    his = [his[i] + his[i + 1] for i in range(0, len(his), 2)]
                plsc.addupdate(o_v.at[pl.ds(c0, SC_LANES)], los[0])
                plsc.addupdate(o_v.at[pl.ds(HALF_D + c0, SC_LANES)], his[0])

            @pl.when(h == PEER_HEADS - 1)
            def _():
                @pl.loop(0, NCHUNK)
                def _(c):
                    c0 = pl.multiple_of(c * SC_LANES, SC_LANES)
                    x_b[t, pl.ds(c0, SC_LANES)] = (x_b[t, pl.ds(c0, SC_LANES)]
                                                   + g2_v[pl.ds(c0, SC_LANES)] * o_v[pl.ds(c0, SC_LANES)])

        @pl.loop(0, nblocks)
        def _(blk):
            tok = pl.multiple_of(tok0 + blk * TOK_BLOCK, TOK_BLOCK)
            pltpu.sync_copy(idx_hbm.at[pl.ds(tok, TOK_BLOCK)], idx_b)
            pltpu.sync_copy(g_hbm.at[pl.ds(tok, TOK_BLOCK)], g_b)
            pltpu.sync_copy(hn_hbm.at[pl.ds(tok, TOK_BLOCK)], hn_b)
            pltpu.sync_copy(x_hbm.at[pl.ds(tok, TOK_BLOCK)], x_b)
            for s0 in range(NSLOT - 1):
                issue(s0, s0)

            @pl.loop(0, steps)
            def _(s):
                @pl.when(s + (NSLOT - 1) < steps)
                def _():
                    issue(s + (NSLOT - 1), (s + (NSLOT - 1)) % NSLOT)

                slot = s % NSLOT
                wait(s, slot)
                compute(s, slot)

            pltpu.sync_copy(x_b, out_hbm.at[pl.ds(tok, TOK_BLOCK)])

    return k(x, hn, eidx, g, g2, uv_tab)


TC_TOK_BLOCK = 64
TC_ISSUE_UNROLL = 32


def _peer_expert_tc_kernel(idx_hbm, x_ref, hn_ref, g_ref, g2_ref, uv_hbm, out_ref,
                           idx_s, uvbuf, sem, isem, *, first_block):
    tb = x_ref.shape[0]
    row0 = (pl.program_id(0) + first_block) * tb
    cp = pltpu.make_async_copy(idx_hbm.at[pl.ds(row0 * PEER_SEL, tb * PEER_SEL)], idx_s, isem)
    cp.start()
    cp.wait()

    def issue(t, slot):
        def body(jb, _):
            base = t * PEER_SEL + jb * TC_ISSUE_UNROLL
            for jj in range(TC_ISSUE_UNROLL):
                pltpu.make_async_copy(uv_hbm.at[pl.ds(idx_s[base + jj], 1)], uvbuf.at[slot, jb, pl.ds(jj, 1)],
                                      sem.at[slot]).start()
            return 0

        lax.fori_loop(0, PEER_SEL // TC_ISSUE_UNROLL, body, 0)

    def wait(slot):
        pltpu.make_async_copy(uv_hbm.at[pl.ds(0, PEER_SEL)], uvbuf.at[slot].reshape(PEER_SEL, D_MODEL),
                              sem.at[slot]).wait()

    issue(0, 0)
    r2 = lax.broadcasted_iota(jnp.int32, (PEER_SEL, LANES), 0)
    c2 = lax.broadcasted_iota(jnp.int32, (PEER_SEL, LANES), 1)
    diag = r2 == c2

    def tok(t, _):
        slot = t % 2

        @pl.when(t + 1 < tb)
        def _():
            issue(t + 1, 1 - slot)

        wait(slot)
        xr = hn_ref[pl.ds(t, 1), :]
        words = uvbuf[slot].reshape(PEER_SEL, D_MODEL)
        u_lo, u_hi = _unpack_words(words[:, :HALF_D])
        prod = u_lo * xr[:, :HALF_D] + u_hi * xr[:, HALF_D:]
        p = prod[:, 0:LANES]
        for c in range(1, HALF_D // LANES):
            p = p + prod[:, c * LANES:(c + 1) * LANES]
        act = jnp.sum(p, axis=-1, keepdims=True)
        grow = g_ref[pl.ds(t, 1), :]
        gcol = jnp.sum(jnp.where(diag, jnp.broadcast_to(grow, (PEER_SEL, LANES)), 0.0),
                       axis=-1, keepdims=True)
        wgt = gcol * _gelu_tanh(act)
        v_lo, v_hi = _unpack_words(words[:, HALF_D:])
        o = jnp.concatenate([jnp.sum(v_lo * wgt, axis=0, keepdims=True),
                             jnp.sum(v_hi * wgt, axis=0, keepdims=True)], axis=-1)
        out_ref[pl.ds(t, 1), :] = x_ref[pl.ds(t, 1), :] + g2_ref[...] * o
        return 0

    lax.fori_loop(0, tb, tok, 0)


def _peer_expert_tc(x, hn, eidx, g, g2, uv_tab, first_token):
    N, D = x.shape
    tb = TC_TOK_BLOCK
    n_out = N - first_token
    assert first_token % tb == 0 and n_out % tb == 0
    first_block = first_token // tb
    tok = lambda t: (t + first_block, 0)
    return pl.pallas_call(
        functools.partial(_peer_expert_tc_kernel, first_block=first_block),
        grid=(n_out // tb,),
        in_specs=[
            pl.BlockSpec(memory_space=pl.ANY),
            pl.BlockSpec((tb, D), tok),
            pl.BlockSpec((tb, D), tok),
            pl.BlockSpec((tb, PEER_SEL), tok),
            pl.BlockSpec((1, D), lambda t: (0, 0)),
            pl.BlockSpec(memory_space=pl.ANY),
        ],
        out_specs=pl.BlockSpec((tb, D), lambda t: (t, 0)),
        out_shape=jax.ShapeDtypeStruct((n_out, D), F32),
        scratch_shapes=[
            pltpu.SMEM((tb * PEER_SEL,), jnp.int32),
            pltpu.VMEM((2, PEER_SEL // TC_ISSUE_UNROLL, TC_ISSUE_UNROLL, D), jnp.int32),
            pltpu.SemaphoreType.DMA((2,)),
            pltpu.SemaphoreType.DMA,
        ],
        compiler_params=pltpu.CompilerParams(
            dimension_semantics=("arbitrary",), vmem_limit_bytes=VMEM_LIMIT_SMALL,
            disable_bounds_checks=True),
        name="peer_expert_tc",
    )(eidx.reshape(-1), x, hn, g, g2, uv_tab)


TC_SHARE_TOKENS = 1792


def _peer_layer(x, sh2, sc2, g2, ng, wqt, sk, uv_tab):
    B, T, D = x.shape
    hn, eidx, g = _peer_route(x, sh2, sc2, ng, wqt, sk)
    N = B * T
    flat = (x.reshape(N, D), hn.reshape(N, D), eidx.reshape(N, PEER_SEL), g.reshape(N, PEER_SEL),
            g2.reshape(B, D), uv_tab)
    n_tc = TC_SHARE_TOKENS if (B == 1 and N >= 4 * TC_SHARE_TOKENS) else 0
    y = _peer_expert_sc(*flat, tokens_per_seq=T, n_tokens=N - n_tc)
    if n_tc:
        y = jnp.concatenate([y, _peer_expert_tc(*flat, first_token=N - n_tc)], axis=0)
    return y.reshape(B, T, D)


def _attn_layer_kernel(*refs, tq, ch, has_cache):
    if has_cache:
        (x_ref, sh_ref, sc_ref, g1_ref, ng_ref, wqkv_ref, qg_ref, kg_ref, bias_ref, wo_ref, kc_ref, vc_ref,
         y_ref, ko_ref, vo_ref, kcat, vcat, qn_s, o_s) = refs
    else:
        (x_ref, sh_ref, sc_ref, g1_ref, ng_ref, wqkv_ref, qg_ref, kg_ref, bias_ref, wo_ref,
         y_ref, ko_ref, vo_ref, kcat, vcat, qn_s, o_s) = refs
    t = pl.program_id(1)
    band = LEFT_ROWS + ch

    if has_cache:
        kcat[0:LEFT_ROWS, :] = kc_ref[0].astype(BF16)
        vcat[0:LEFT_ROWS, :] = vc_ref[0].astype(BF16)
    else:
        @pl.when(t == 0)
        def _():
            kcat[0:LEFT_ROWS, :] = jnp.zeros((LEFT_ROWS, D_MODEL), BF16)
            vcat[0:LEFT_ROWS, :] = jnp.zeros((LEFT_ROWS, D_MODEL), BF16)

    x = x_ref[0]
    ms = jnp.mean(x * x, axis=-1, keepdims=True)
    hn = x * lax.rsqrt(ms + EPS) * ng_ref[...]
    hn = hn * (1.0 + sc_ref[0]) + sh_ref[0]
    qkv = jnp.dot(hn.astype(BF16), wqkv_ref[...], preferred_element_type=F32)
    for h in range(N_HEADS):
        lo = h * HEAD_DIM
        qh = qkv[:, lo:lo + HEAD_DIM]
        kh = qkv[:, D_MODEL + lo:D_MODEL + lo + HEAD_DIM]
        qh = qh * lax.rsqrt(jnp.mean(qh * qh, axis=-1, keepdims=True) + EPS) * qg_ref[...]
        kh = kh * lax.rsqrt(jnp.mean(kh * kh, axis=-1, keepdims=True) + EPS) * kg_ref[...]
        qn_s[:, lo:lo + HEAD_DIM] = qh.astype(BF16)
        ko_ref[0, :, lo:lo + HEAD_DIM] = kh
        kcat[LEFT_ROWS:LEFT_ROWS + tq, lo:lo + HEAD_DIM] = kh.astype(BF16)
    v = qkv[:, 2 * D_MODEL:]
    vo_ref[0] = v
    vcat[LEFT_ROWS:LEFT_ROWS + tq, :] = v.astype(BF16)

    scale = HEAD_DIM ** -0.5
    kpos_rel = lax.broadcasted_iota(jnp.int32, (ch, band), 1)

    def chunk_body(c, _):
        r0 = pl.multiple_of(c * ch, ch)
        if not has_cache:
            valid = kpos_rel + (t * tq + r0) >= LEFT_ROWS
        for h in range(N_HEADS):
            lo = h * HEAD_DIM
            q = qn_s[pl.ds(r0, ch), lo:lo + HEAD_DIM]
            kb = kcat[pl.ds(r0, band), lo:lo + HEAD_DIM]
            vb = vcat[pl.ds(r0, band), lo:lo + HEAD_DIM]
            s = lax.dot_general(q, kb, (((1,), (1,)), ((), ())), preferred_element_type=F32) * scale
            s = s + bias_ref[h, 0:ch, 0:band]
            if not has_cache:
                s = jnp.where(valid, s, NEG_INF)
            m = jnp.max(s, axis=-1, keepdims=True)
            e = jnp.exp(s - m)
            p = e / jnp.sum(e, axis=-1, keepdims=True)
            o_s[pl.ds(r0, ch), lo:lo + HEAD_DIM] = jnp.dot(p.astype(BF16), vb, preferred_element_type=F32).astype(BF16)
        return 0

    lax.fori_loop(0, tq // ch, chunk_body, 0)

    out = jnp.dot(o_s[...], wo_ref[...], preferred_element_type=F32)
    y_ref[0] = x + g1_ref[0] * out

    if not has_cache:
        kcat[0:LEFT_ROWS, :] = kcat[tq:tq + LEFT_ROWS, :]
        vcat[0:LEFT_ROWS, :] = vcat[tq:tq + LEFT_ROWS, :]


def _rel_bias_tile(rel_bias):
    i = jnp.arange(CHUNK)[:, None]
    r = jnp.arange(LEFT_ROWS + CHUNK)[None, :]
    rel = jnp.clip(i + LEFT_ROWS - r, -REL_CLIP, REL_CLIP) + REL_CLIP
    return rel_bias[:, rel]


def _attn_layer(x, sh, sc, g1, ng, wqkv, qg, kg, bias, wo, k_cache=None, v_cache=None):
    B, T, D = x.shape
    has_cache = k_cache is not None
    if has_cache:
        tq, ch = T, T
        keep_blocks = 1
    else:
        tq, ch = LAYER_TILE, CHUNK
        keep_blocks = LEFT_ROWS // tq
    nt = T // tq
    assert nt >= keep_blocks
    tok = lambda b, t: (b, t, 0)
    per_b = lambda b, t: (b, 0, 0)
    c2 = lambda b, t: (0, 0)
    c3 = lambda b, t: (0, 0, 0)
    keep = lambda b, t: (b, jnp.maximum(t - (nt - keep_blocks), 0), 0)
    in_specs = [
        pl.BlockSpec((1, tq, D), tok),
        pl.BlockSpec((1, 1, D), per_b), pl.BlockSpec((1, 1, D), per_b), pl.BlockSpec((1, 1, D), per_b),
        pl.BlockSpec((1, D), c2),
        pl.BlockSpec(wqkv.shape, c2),
        pl.BlockSpec((1, HEAD_DIM), c2), pl.BlockSpec((1, HEAD_DIM), c2),
        pl.BlockSpec(bias.shape, c3),
        pl.BlockSpec(wo.shape, c2),
    ]
    args = [x, sh, sc, g1, ng, wqkv, qg, kg, bias, wo]
    if has_cache:
        in_specs += [pl.BlockSpec((1, LEFT_ROWS, D), per_b), pl.BlockSpec((1, LEFT_ROWS, D), per_b)]
        args += [k_cache, v_cache]
    return pl.pallas_call(
        functools.partial(_attn_layer_kernel, tq=tq, ch=ch, has_cache=has_cache),
        grid=(B, nt),
        in_specs=in_specs,
        out_specs=[pl.BlockSpec((1, tq, D), tok), pl.BlockSpec((1, tq, D), keep), pl.BlockSpec((1, tq, D), keep)],
        out_shape=[jax.ShapeDtypeStruct((B, T, D), F32),
                   jax.ShapeDtypeStruct((B, keep_blocks * tq, D), F32),
                   jax.ShapeDtypeStruct((B, keep_blocks * tq, D), F32)],
        scratch_shapes=[
            pltpu.VMEM((LEFT_ROWS + tq, D), BF16),
            pltpu.VMEM((LEFT_ROWS + tq, D), BF16),
            pltpu.VMEM((tq, D), BF16),
            pltpu.VMEM((tq, D), BF16),
        ],
        compiler_params=pltpu.CompilerParams(
            dimension_semantics=("parallel", "arbitrary"), vmem_limit_bytes=VMEM_LIMIT_LAYER),
        name="attn_layer",
    )(*args)


def _gelu_tanh(x):
    return 0.5 * x * (1.0 + jnp.tanh(0.7978845608028654 * (x + 0.044715 * (x * x * x))))


def _lru_layer_kernel(x_ref, sh_ref, sc_ref, g1_ref, ng_ref, win_ref, cw_ref, cb_ref, gw_ref, gb_ref, lam_ref,
                      wout_ref, cs0_ref, h0_ref,
                      y_ref, cs_ref, hl_ref,
                      ext_s, a_s, b_s, hcar_s):
    tt = x_ref.shape[1]
    W = LRU_WIDTH
    t = pl.program_id(1)

    @pl.when(t == 0)
    def _():
        ext_s[0:SUBLANES, :] = cs0_ref[0]
        hcar_s[...] = h0_ref[0]

    x = x_ref[0]
    ms = jnp.mean(x * x, axis=-1, keepdims=True)
    hn = x * lax.rsqrt(ms + EPS) * ng_ref[...]
    hn = hn * (1.0 + sc_ref[0]) + sh_ref[0]
    proj = jnp.dot(hn.astype(BF16), win_ref[...], preferred_element_type=F32)
    gate = _gelu_tanh(proj[:, :W])
    rec = proj[:, W:]
    ext_s[SUBLANES:SUBLANES + tt, :] = rec
    u = cb_ref[...] + cw_ref[3:4, :] * rec
    for k in range(CONV_W - 1):
        u = u + cw_ref[k:k + 1, :] * ext_s[SUBLANES - 3 + k:SUBLANES - 3 + k + tt, :]
    tail = ext_s[tt:tt + SUBLANES, :]
    cs_ref[0] = tail
    ext_s[0:SUBLANES, :] = tail

    sp_in = -lam_ref[...]
    softplus = jnp.maximum(sp_in, 0.0) + jnp.log1p(jnp.exp(-jnp.abs(sp_in)))
    for n in range(LRU_BLOCKS):
        lo = n * LRU_BLOCK_W
        ub = u[:, lo:lo + LRU_BLOCK_W]
        ubb = ub.astype(BF16)
        gr = jnp.dot(ubb, gw_ref[0, n], preferred_element_type=F32) + gb_ref[0:1, lo:lo + LRU_BLOCK_W]
        gi = jnp.dot(ubb, gw_ref[1, n], preferred_element_type=F32) + gb_ref[1:2, lo:lo + LRU_BLOCK_W]
        r = jax.nn.sigmoid(gr)
        i = jax.nn.sigmoid(gi)
        log_a = -RG_C * r * softplus[:, lo:lo + LRU_BLOCK_W]
        a = jnp.exp(log_a)
        a_s[:, lo:lo + LRU_BLOCK_W] = a
        b_s[:, lo:lo + LRU_BLOCK_W] = jnp.sqrt(1.0 - a * a) * i * ub

    def group(gi_, h):
        r0 = pl.multiple_of(gi_ * SUBLANES, SUBLANES)
        a8 = a_s[pl.ds(r0, SUBLANES), :]
        b8 = b_s[pl.ds(r0, SUBLANES), :]
        rows = []
        for j in range(SUBLANES):
            h = a8[j:j + 1, :] * h + b8[j:j + 1, :]
            rows.append(h)
        b_s[pl.ds(r0, SUBLANES), :] = jnp.concatenate(rows, axis=0)
        return h

    h_last = lax.fori_loop(0, tt // SUBLANES, group, hcar_s[...])
    hcar_s[...] = h_last
    hl_ref[0] = h_last
    yv = b_s[...] * gate
    out = jnp.dot(yv.astype(BF16), wout_ref[...], preferred_element_type=F32)
    y_ref[0] = x + g1_ref[0] * out


def _lru_layer(x, sh, sc, g1, ng, w_in, conv_w, conv_b, gate_w, gate_b, lam, w_out, conv_state, h0):
    B, T, D = x.shape
    W = LRU_WIDTH
    tt = min(T, LAYER_TILE)
    cs0 = jnp.pad(conv_state, ((0, 0), (SUBLANES - (CONV_W - 1), 0), (0, 0)))
    tok = lambda b, t: (b, t, 0)
    per_b = lambda b, t: (b, 0, 0)
    c2 = lambda b, t: (0, 0)
    y, cs, hl = pl.pallas_call(
        _lru_layer_kernel,
        grid=(B, T // tt),
        in_specs=[
            pl.BlockSpec((1, tt, D), tok),
            pl.BlockSpec((1, 1, D), per_b), pl.BlockSpec((1, 1, D), per_b), pl.BlockSpec((1, 1, D), per_b),
            pl.BlockSpec((1, D), c2),
            pl.BlockSpec(w_in.shape, c2),
            pl.BlockSpec((CONV_W, W), c2), pl.BlockSpec((1, W), c2),
            pl.BlockSpec(gate_w.shape, lambda b, t: (0, 0, 0, 0)),
            pl.BlockSpec((2, W), c2), pl.BlockSpec((1, W), c2),
            pl.BlockSpec(w_out.shape, c2),
            pl.BlockSpec((1, SUBLANES, W), per_b), pl.BlockSpec((1, 1, W), per_b),
        ],
        out_specs=[pl.BlockSpec((1, tt, D), tok), pl.BlockSpec((1, SUBLANES, W), per_b),
                   pl.BlockSpec((1, 1, W), per_b)],
        out_shape=[jax.ShapeDtypeStruct((B, T, D), F32), jax.ShapeDtypeStruct((B, SUBLANES, W), F32),
                   jax.ShapeDtypeStruct((B, 1, W), F32)],
        scratch_shapes=[
            pltpu.VMEM((tt + SUBLANES, W), F32),
            pltpu.VMEM((tt, W), F32),
            pltpu.VMEM((tt, W), F32),
            pltpu.VMEM((1, W), F32),
        ],
        compiler_params=pltpu.CompilerParams(
            dimension_semantics=("parallel", "arbitrary"), vmem_limit_bytes=VMEM_LIMIT_LAYER),
        name="lru_layer",
    )(x, sh, sc, g1, ng, w_in, conv_w, conv_b, gate_w, gate_b, lam, w_out, cs0, h0[:, None, :])
    return y, cs[:, SUBLANES - (CONV_W - 1):], hl[:, 0]


def _ada_kernel(c_ref, w_ref, b_ref, m_ref):
    c = c_ref[...]
    s = c * jax.nn.sigmoid(c)
    m_ref[0] = jnp.dot(s, w_ref[0], preferred_element_type=F32, precision=lax.Precision.HIGHEST) + b_ref[0]


def _ada_all(c_all, ada_w, ada_b):
    L, D, E = ada_w.shape
    R = c_all.shape[0]
    tn = ADA_TILE
    return pl.pallas_call(
        _ada_kernel,
        grid=(L, E // tn),
        in_specs=[pl.BlockSpec((R, D), lambda l, j: (0, 0)),
                  pl.BlockSpec((1, D, tn), lambda l, j: (l, 0, j)),
                  pl.BlockSpec((1, 1, tn), lambda l, j: (l, 0, j))],
        out_specs=pl.BlockSpec((1, R, tn), lambda l, j: (l, 0, j)),
        out_shape=jax.ShapeDtypeStruct((L, R, E), F32),
        compiler_params=pltpu.CompilerParams(
            dimension_semantics=("parallel", "parallel"), vmem_limit_bytes=VMEM_LIMIT_SMALL),
        name="ada_mod",
    )(c_all, ada_w, ada_b[:, None, :])


class _Chain:
    def __init__(self, x, mods, conv_st, lru_st, k_cache, v_cache, w, prompt):
        self.x, self.mods, self.w, self.prompt = x, mods, w, prompt
        self.conv_st, self.lru_st, self.k_cache, self.v_cache = conv_st, lru_st, k_cache, v_cache
        self.new_conv, self.new_h, self.new_k, self.new_v = [], [], [], []
        self.layer = 0

    def step(self):
        i, w, x = self.layer, self.w, self.x
        B, T, D = x.shape
        sh1, sc1, g1, sh2, sc2, g2 = self.mods[i]
        j = i // 2
        if i % 2 == 0:
            if self.prompt:
                cs0 = jnp.zeros((B, CONV_W - 1, LRU_WIDTH), x.dtype)
                h0 = jnp.zeros((B, LRU_WIDTH), F32)
            else:
                cs0, h0 = self.conv_st[j], self.lru_st[j]
            x, cnew, hnew = _lru_layer(x, sh1, sc1, g1, w['norm_g'][i, 0][None, :], w['lru_w_in'][j],
                                       w['lru_conv_w'][j], w['lru_conv_b'][j][None, :], w['lru_gate_w'][j],
                                       w['lru_gate_b'][j], w['lru_lambda'][j][None, :], w['lru_w_out'][j], cs0, h0)
            self.new_conv.append(cnew)
            self.new_h.append(hnew)
        else:
            if self.prompt:
                kc = vc = None
            else:
                kc = self.k_cache[j].reshape(B, LEFT_ROWS, D)
                vc = self.v_cache[j].reshape(B, LEFT_ROWS, D)
            x, kn, vn = _attn_layer(x, sh1, sc1, g1, w['norm_g'][i, 0][None, :], w['att_w_qkv'][j],
                                    w['att_q_gain'][j][None, :], w['att_k_gain'][j][None, :], w['att_bias'][j],
                                    w['att_w_o'][j], kc, vc)
            self.new_k.append(kn.reshape(B, kn.shape[1], N_HEADS, HEAD_DIM))
            self.new_v.append(vn.reshape(B, vn.shape[1], N_HEADS, HEAD_DIM))
        self.x = _peer_layer(x, sh2, sc2, g2, w['norm_g'][i, 1][None, :], w['peer_wqt'][i], w['peer_sk'][i],
                             w['peer_uv'][i])
        self.layer += 1

    def results(self):
        return (self.x, jnp.stack(self.new_conv), jnp.stack(self.new_h), jnp.stack(self.new_k),
                jnp.stack(self.new_v))


def _run_chains(chains, depth):
    for wave in range(depth + len(chains) - 1):
        for ci, chain in enumerate(chains):
            if 0 <= wave - ci < depth:
                chain.step()


def kernel(x_prompt, x_sample, c_prompt, c_sample, state_conv, state_lru_h, cache_k, cache_v, norm_g, ada_w, ada_b, lru_w_in, lru_conv_w, lru_conv_b, lru_gate_w, lru_gate_b, lru_lambda, lru_w_out, att_w_qkv, att_q_gain, att_k_gain, att_rel_bias, att_w_o, peer_w_query, peer_sub_keys, peer_u, peer_v):
    depth = norm_g.shape[0]
    bp, bs = c_prompt.shape[0], c_sample.shape[0]
    rows = -(-(bp + bs) // SUBLANES) * SUBLANES
    c_all = jnp.concatenate([c_prompt, c_sample, jnp.zeros((rows - bp - bs, D_MODEL), F32)], axis=0)
    m_all = _ada_all(c_all, ada_w, ada_b)

    def mods(lo, n):
        return [[m_all[i, lo:lo + n, None, k * D_MODEL:(k + 1) * D_MODEL] for k in range(6)] for i in range(depth)]

    w = {
        'norm_g': norm_g,
        'lru_w_in': lru_w_in.astype(BF16), 'lru_conv_w': lru_conv_w, 'lru_conv_b': lru_conv_b,
        'lru_gate_w': lru_gate_w.astype(BF16), 'lru_gate_b': lru_gate_b, 'lru_lambda': lru_lambda,
        'lru_w_out': lru_w_out.astype(BF16),
        'att_w_qkv': att_w_qkv.astype(BF16), 'att_q_gain': att_q_gain, 'att_k_gain': att_k_gain,
        'att_bias': jax.vmap(_rel_bias_tile)(att_rel_bias), 'att_w_o': att_w_o.astype(BF16),
        'peer_wqt': jnp.swapaxes(peer_w_query, 1, 2).astype(BF16),
        'peer_sk': peer_sub_keys.astype(BF16).reshape(depth, 2 * PEER_HEADS, N_KEYS, PEER_HALF),
        'peer_uv': [_pack_expert_tables(peer_u[i], peer_v[i]) for i in range(depth)],
    }
    prompt_chains = [_Chain(x_prompt[b:b + 1], mods(b, 1), None, None, None, None, w, True) for b in range(bp)]
    sample_chain = _Chain(x_sample, mods(bp, bs), state_conv, state_lru_h, cache_k, cache_v, w, False)
    _run_chains(prompt_chains + [sample_chain], depth)
    y_prompt, p_conv, p_h, p_k, p_v = (
        jnp.concatenate(parts, axis=ax)
        for parts, ax in zip(zip(*(c.results() for c in prompt_chains)), (0, 1, 1, 1, 1)))
    y_sample, s_conv, s_h, s_k, s_v = sample_chain.results()
    return (y_prompt, y_sample, p_conv, p_h, p_k, p_v, s_conv, s_h, s_k, s_v)
```

```python
import functools

import jax
import jax.numpy as jnp
import numpy as np
from jax import lax
from jax.experimental import pallas as pl
from jax.experimental.pallas import tpu as pltpu
from jax.experimental.pallas import tpu_sc as plsc

D_MODEL = 1024
CHUNK = 64
LRU_WIDTH = D_MODEL
LRU_BLOCKS = 4
LRU_BLOCK_W = LRU_WIDTH // LRU_BLOCKS
CONV_W = 4
RG_C = 8.0
N_HEADS = 8
HEAD_DIM = D_MODEL // N_HEADS
LEFT_CHUNKS = 8
LEFT_ROWS = LEFT_CHUNKS * CHUNK
REL_CLIP = 128
PEER_HEADS = 8
N_KEYS = 128
PEER_TOPK = 16
PEER_QDIM = 256
PEER_HALF = PEER_QDIM // 2
PEER_SEL = PEER_HEADS * PEER_TOPK
EPS = 1e-6
NEG_INF = -1e30

LANES = 128
SUBLANES = 8
V7X_VMEM_BYTES = 64 * 1024 * 1024
VMEM_LIMIT_LAYER = (3 * V7X_VMEM_BYTES) // 4
VMEM_LIMIT_SMALL = V7X_VMEM_BYTES // 2
ROUTE_TILE = 512
LAYER_TILE = 256
ADA_TILE = 1536
BF16 = jnp.bfloat16
F32 = jnp.float32


_CAND_RUNS = [(a, PEER_TOPK // (a + 1)) for a in range(PEER_TOPK)]
_NCAND = sum(n for _, n in _CAND_RUNS)
_NCAND_PAD = -(-_NCAND // SUBLANES) * SUBLANES
_CAND_FLAT = np.full((_NCAND_PAD,), PEER_TOPK * PEER_TOPK, np.int32)
_off = 0
for _a, _n in _CAND_RUNS:
    _CAND_FLAT[_off:_off + _n] = _a * PEER_TOPK + np.arange(_n)
    _off += _n


def _peer_route_kernel(x_ref, sh_ref, sc_ref, ng_ref, wqt_ref, sk_ref, cand_ref,
                       hn_ref, idx_ref, g_ref,
                       q_s, cs_s, ci_s):
    tt = x_ref.shape[1]
    x = x_ref[0]
    ms = jnp.mean(x * x, axis=-1, keepdims=True)
    hn = x * lax.rsqrt(ms + EPS) * ng_ref[...]
    hn = hn * (1.0 + sc_ref[0]) + sh_ref[0]
    hn_ref[0] = hn
    qt = lax.dot_general(wqt_ref[...], hn.astype(BF16), (((1,), (1,)), ((), ())),
                         preferred_element_type=F32)
    for gidx in range(2 * PEER_HEADS):
        q_s[gidx] = qt[gidx * PEER_HALF:(gidx + 1) * PEER_HALF, :]

    key_id = lax.broadcasted_iota(jnp.int32, (N_KEYS, tt), 0)
    slot_id = lax.broadcasted_iota(jnp.int32, (PEER_TOPK, tt), 0)
    cand = cand_ref[:, 0:tt]
    big = PEER_TOPK * PEER_TOPK
    cs_s[_NCAND:_NCAND_PAD, :] = jnp.full((_NCAND_PAD - _NCAND, tt), -jnp.inf, F32)
    ci_s[_NCAND:_NCAND_PAD, :] = jnp.zeros((_NCAND_PAD - _NCAND, tt), jnp.int32)

    def head_body(h, _):
        s = [jnp.dot(sk_ref[2 * h + p], q_s[2 * h + p].astype(BF16), preferred_element_type=F32)
             for p in range(2)]
        ts = [jnp.zeros((PEER_TOPK, tt), F32) for _ in range(2)]
        ti = [jnp.zeros((PEER_TOPK, tt), jnp.int32) for _ in range(2)]
        for j in range(PEER_TOPK):
            for p in range(2):
                m = jnp.max(s[p], axis=0, keepdims=True)
                am = jnp.min(jnp.where(s[p] == m, key_id, N_KEYS), axis=0, keepdims=True)
                ts[p] = jnp.where(slot_id == j, m, ts[p])
                ti[p] = jnp.where(slot_id == j, am, ti[p])
                s[p] = jnp.where(key_id == am, -jnp.inf, s[p])
        off = 0
        for a, n in _CAND_RUNS:
            cs_s[off:off + n, :] = ts[0][a:a + 1, :] + ts[1][0:n, :]
            ci_s[off:off + n, :] = ti[0][a:a + 1, :] * N_KEYS + ti[1][0:n, :]
            off += n
        cs = cs_s[...]
        ci = ci_s[...]
        fs = jnp.zeros((PEER_TOPK, tt), F32)
        ei = jnp.zeros((PEER_TOPK, tt), jnp.int32)
        for k in range(PEER_TOPK):
            m = jnp.max(cs, axis=0, keepdims=True)
            cmin = jnp.min(jnp.where(cs == m, cand, big), axis=0, keepdims=True)
            sel = cand == cmin
            e = jnp.sum(jnp.where(sel, ci, 0), axis=0, keepdims=True)
            fs = jnp.where(slot_id == k, m, fs)
            ei = jnp.where(slot_id == k, e, ei)
            cs = jnp.where(sel, -jnp.inf, cs)
        ex = jnp.exp(fs - fs[0:1, :])
        g = ex / jnp.sum(ex, axis=0, keepdims=True)
        row0 = pl.multiple_of(h * PEER_TOPK, PEER_TOPK)
        g_ref[0, pl.ds(row0, PEER_TOPK), :] = g
        idx_ref[0, pl.ds(row0, PEER_TOPK), :] = ei
        return 0

    lax.fori_loop(0, PEER_HEADS, head_body, 0)


def _peer_route(x, sh, sc, ng, wqt, sk):
    B, T, D = x.shape
    tt = min(T, ROUTE_TILE)
    grid = (B, T // tt)
    tok = lambda b, t: (b, t, 0)
    tok_t = lambda b, t: (b, 0, t)
    per_b = lambda b, t: (b, 0, 0)
    cand_tab = jnp.asarray(np.broadcast_to(_CAND_FLAT[:, None], (_NCAND_PAD, max(tt, LANES))))
    hn, idx_t, g_t = pl.pallas_call(
        _peer_route_kernel,
        grid=grid,
        in_specs=[
            pl.BlockSpec((1, tt, D), tok),
            pl.BlockSpec((1, 1, D), per_b),
            pl.BlockSpec((1, 1, D), per_b),
            pl.BlockSpec((1, D), lambda b, t: (0, 0)),
            pl.BlockSpec(wqt.shape, lambda b, t: (0, 0)),
            pl.BlockSpec(sk.shape, lambda b, t: (0, 0, 0)),
            pl.BlockSpec(cand_tab.shape, lambda b, t: (0, 0)),
        ],
        out_specs=[
            pl.BlockSpec((1, tt, D), tok),
            pl.BlockSpec((1, PEER_SEL, tt), tok_t),
            pl.BlockSpec((1, PEER_SEL, tt), tok_t),
        ],
        out_shape=[
            jax.ShapeDtypeStruct((B, T, D), F32),
            jax.ShapeDtypeStruct((B, PEER_SEL, T), jnp.int32),
            jax.ShapeDtypeStruct((B, PEER_SEL, T), F32),
        ],
        scratch_shapes=[
            pltpu.VMEM((2 * PEER_HEADS, PEER_HALF, tt), F32),
            pltpu.VMEM((_NCAND_PAD, tt), F32),
            pltpu.VMEM((_NCAND_PAD, tt), jnp.int32),
        ],
        compiler_params=pltpu.CompilerParams(
            dimension_semantics=("parallel", "parallel"), vmem_limit_bytes=VMEM_LIMIT_LAYER),
        name="peer_route",
    )(x, sh, sc, ng, wqt, sk, cand_tab)
    return hn, jnp.swapaxes(idx_t, 1, 2), jnp.swapaxes(g_t, 1, 2)


SC_CORES = 2
SC_SUBCORES = 16
SC_LANES = 16
SC_WORKERS = SC_CORES * SC_SUBCORES
NCHUNK = D_MODEL // SC_LANES
HALF_D = D_MODEL // 2
NWCHUNK = HALF_D // SC_LANES
HI_MASK = -65536
TOK_BLOCKS = (16, 8)
NSLOT = 3


def _pack_expert_tables(u_tab, v_tab):
    def pack(tab):
        bits = lax.bitcast_convert_type(tab.astype(BF16), jnp.uint16).astype(jnp.uint32)
        return bits[:, :HALF_D] | (bits[:, HALF_D:] << 16)

    return lax.bitcast_convert_type(jnp.concatenate([pack(u_tab), pack(v_tab)], axis=1), jnp.int32)


def _unpack_words(words):
    lo = lax.bitcast_convert_type(lax.shift_left(words, jnp.full_like(words, 16)), F32)
    hi = lax.bitcast_convert_type(words & HI_MASK, F32)
    return lo, hi


def _peer_expert_sc(x, hn, eidx, g, g2, uv_tab, tokens_per_seq, n_tokens):
    N, D = n_tokens, x.shape[1]
    tpw = N // SC_WORKERS
    assert tpw * SC_WORKERS == N and tpw % min(TOK_BLOCKS) == 0
    tok_block = next(b for b in TOK_BLOCKS if tpw % b == 0)
    assert g2.shape[0] == 1 or tokens_per_seq % tpw == 0
    workers_per_seq = max(tokens_per_seq // tpw, 1) if g2.shape[0] > 1 else SC_WORKERS
    nblocks = tpw // tok_block
    steps = tok_block * PEER_HEADS
    mesh = plsc.VectorSubcoreMesh(core_axis_name="c", subcore_axis_name="s",
                                  num_cores=SC_CORES, num_subcores=SC_SUBCORES)

    @functools.partial(
        pl.kernel, out_type=jax.ShapeDtypeStruct((N, D), F32), mesh=mesh,
        scratch_types=[
            pltpu.VMEM((tok_block, D), F32),
            pltpu.VMEM((tok_block, D), F32),
            pltpu.VMEM((tok_block, PEER_SEL), jnp.int32),
            pltpu.VMEM((tok_block, PEER_SEL), F32),
            pltpu.VMEM((D,), F32),
            pltpu.VMEM((NSLOT, PEER_TOPK, D), jnp.int32),
            pltpu.VMEM((D,), F32),
            pltpu.SemaphoreType.DMA((NSLOT,)),
        ],
        compiler_params=pltpu.CompilerParams(needs_layout_passes=False),
        name="peer_expert_sc",
    )
    def k(x_hbm, hn_hbm, idx_hbm, g_hbm, g2_hbm, uv_hbm, out_hbm,
          hn_b, x_b, idx_b, g_b, g2_v, uvbuf, o_v, sem):
        wid = lax.axis_index("s") * SC_CORES + lax.axis_index("c")
        tok0 = wid * tpw
        pltpu.sync_copy(g2_hbm.at[wid // workers_per_seq], g2_v)
        lane = lax.iota(jnp.int32, SC_LANES)

        def gather_desc(s, slot):
            t = s // PEER_HEADS
            h0 = pl.multiple_of((s % PEER_HEADS) * PEER_TOPK, PEER_TOPK)
            ids = idx_b.at[t, pl.ds(h0, PEER_TOPK)]
            return pltpu.make_async_copy(uv_hbm.at[ids], uvbuf.at[slot], sem.at[slot])

        def issue(s, slot):
            gather_desc(s, slot).start()

        def wait(s, slot):
            gather_desc(s, slot).wait()

        def compute(s, slot):
            t = s // PEER_HEADS
            h = s % PEER_HEADS
            h0 = pl.multiple_of(h * PEER_TOPK, PEER_TOPK)

            @pl.when(h == 0)
            def _():
                @pl.loop(0, NCHUNK)
                def _(c):
                    o_v[pl.ds(pl.multiple_of(c * SC_LANES, SC_LANES), SC_LANES)] = jnp.zeros((SC_LANES,), F32)

            def ubody(c, accs):
                c0 = pl.multiple_of(c * SC_LANES, SC_LANES)
                x_lo = hn_b[t, pl.ds(c0, SC_LANES)]
                x_hi = hn_b[t, pl.ds(HALF_D + c0, SC_LANES)]
                out = []
                for kk in range(PEER_TOPK):
                    lo, hi = _unpack_words(uvbuf[slot, kk, pl.ds(c0, SC_LANES)])
                    out.append(accs[kk] + (x_lo * lo + x_hi * hi))
                return tuple(out)

            accs = lax.fori_loop(0, NWCHUNK, ubody,
                                 tuple(jnp.zeros((SC_LANES,), F32) for _ in range(PEER_TOPK)))
            last = jnp.full((SC_LANES,), SC_LANES - 1, jnp.int32)
            r = jnp.zeros((SC_LANES,), F32)
            for kk in range(PEER_TOPK):
                tot = jnp.take_along_axis(plsc.cumsum(accs[kk]), last, axis=0)
                r = jnp.where(lane == kk, tot, r)
            gv = g_b[t, pl.ds(h0, PEER_TOPK)]
            z = 0.7978845608028654 * (r + 0.044715 * (r * r * r))
            w = gv * r / (1.0 + jnp.exp(-2.0 * z))
            wk = [jnp.take_along_axis(w, jnp.full((SC_LANES,), kk, jnp.int32), axis=0) for kk in range(PEER_TOPK)]

            @plsc.parallel_loop(0, NWCHUNK, unroll=2)
            def _(c):
                c0 = pl.multiple_of(c * SC_LANES, SC_LANES)
                los, his = [], []
                for kk in range(PEER_TOPK):
                    lo, hi = _unpack_words(uvbuf[slot, kk, pl.ds(HALF_D + c0, SC_LANES)])
                    los.append(wk[kk] * lo)
                    his.append(wk[kk] * hi)
                while len(los) > 1:
                    los = [los[i] + los[i + 1] for i in range(0, len(los), 2)]
                    his = [his[i] + his[i + 1] for i in range(0, len(his), 2)]
                plsc.addupdate(o_v.at[pl.ds(c0, SC_LANES)], los[0])
                plsc.addupdate(o_v.at[pl.ds(HALF_D + c0, SC_LANES)], his[0])

            @pl.when(h == PEER_HEADS - 1)
            def _():
                @pl.loop(0, NCHUNK)
                def _(c):
                    c0 = pl.multiple_of(c * SC_LANES, SC_LANES)
                    x_b[t, pl.ds(c0, SC_LANES)] = (x_b[t, pl.ds(c0, SC_LANES)]
                                                   + g2_v[pl.ds(c0, SC_LANES)] * o_v[pl.ds(c0, SC_LANES)])

        @pl.loop(0, nblocks)
        def _(blk):
            tok = pl.multiple_of(tok0 + blk * tok_block, tok_block)
            pltpu.sync_copy(idx_hbm.at[pl.ds(tok, tok_block)], idx_b)
            pltpu.sync_copy(g_hbm.at[pl.ds(tok, tok_block)], g_b)
            pltpu.sync_copy(hn_hbm.at[pl.ds(tok, tok_block)], hn_b)
            pltpu.sync_copy(x_hbm.at[pl.ds(tok, tok_block)], x_b)
            for s0 in range(NSLOT - 1):
                issue(s0, s0)

            @pl.loop(0, steps)
            def _(s):
                @pl.when(s + (NSLOT - 1) < steps)
                def _():
                    issue(s + (NSLOT - 1), (s + (NSLOT - 1)) % NSLOT)

                slot = s % NSLOT
                wait(s, slot)
                compute(s, slot)

            pltpu.sync_copy(x_b, out_hbm.at[pl.ds(tok, tok_block)])

    return k(x, hn, eidx, g, g2, uv_tab)


TC_TOK_BLOCK = 64
TC_ISSUE_UNROLL = 32


def _peer_expert_tc_kernel(idx_hbm, x_ref, hn_ref, g_ref, g2_ref, uv_hbm, out_ref,
                           idx_s, uvbuf, sem, isem, *, first_block):
    tb = x_ref.shape[0]
    row0 = (pl.program_id(0) + first_block) * tb
    cp = pltpu.make_async_copy(idx_hbm.at[pl.ds(row0 * PEER_SEL, tb * PEER_SEL)], idx_s, isem)
    cp.start()
    cp.wait()

    def issue(t, slot):
        def body(jb, _):
            base = t * PEER_SEL + jb * TC_ISSUE_UNROLL
            for jj in range(TC_ISSUE_UNROLL):
                pltpu.make_async_copy(uv_hbm.at[pl.ds(idx_s[base + jj], 1)], uvbuf.at[slot, jb, pl.ds(jj, 1)],
                                      sem.at[slot]).start()
            return 0

        lax.fori_loop(0, PEER_SEL // TC_ISSUE_UNROLL, body, 0)

    def wait(slot):
        pltpu.make_async_copy(uv_hbm.at[pl.ds(0, PEER_SEL)], uvbuf.at[slot].reshape(PEER_SEL, D_MODEL),
                              sem.at[slot]).wait()

    issue(0, 0)
    r2 = lax.broadcasted_iota(jnp.int32, (PEER_SEL, LANES), 0)
    c2 = lax.broadcasted_iota(jnp.int32, (PEER_SEL, LANES), 1)
    diag = r2 == c2

    def tok(t, _):
        slot = t % 2

        @pl.when(t + 1 < tb)
        def _():
            issue(t + 1, 1 - slot)

        wait(slot)
        xr = hn_ref[pl.ds(t, 1), :]
        words = uvbuf[slot].reshape(PEER_SEL, D_MODEL)
        u_lo, u_hi = _unpack_words(words[:, :HALF_D])
        prod = u_lo * xr[:, :HALF_D] + u_hi * xr[:, HALF_D:]
        p = prod[:, 0:LANES]
        for c in range(1, HALF_D // LANES):
            p = p + prod[:, c * LANES:(c + 1) * LANES]
        act = jnp.sum(p, axis=-1, keepdims=True)
        grow = g_ref[pl.ds(t, 1), :]
        gcol = jnp.sum(jnp.where(diag, jnp.broadcast_to(grow, (PEER_SEL, LANES)), 0.0),
                       axis=-1, keepdims=True)
        wgt = gcol * _gelu_tanh(act)
        v_lo, v_hi = _unpack_words(words[:, HALF_D:])
        o = jnp.concatenate([jnp.sum(v_lo * wgt, axis=0, keepdims=True),
                             jnp.sum(v_hi * wgt, axis=0, keepdims=True)], axis=-1)
        out_ref[pl.ds(t, 1), :] = x_ref[pl.ds(t, 1), :] + g2_ref[...] * o
        return 0

    lax.fori_loop(0, tb, tok, 0)


def _peer_expert_tc(x, hn, eidx, g, g2, uv_tab, first_token):
    N, D = x.shape
    tb = TC_TOK_BLOCK
    n_out = N - first_token
    assert first_token % tb == 0 and n_out % tb == 0
    first_block = first_token // tb
    tok = lambda t: (t + first_block, 0)
    return pl.pallas_call(
        functools.partial(_peer_expert_tc_kernel, first_block=first_block),
        grid=(n_out // tb,),
        in_specs=[
            pl.BlockSpec(memory_space=pl.ANY),
            pl.BlockSpec((tb, D), tok),
            pl.BlockSpec((tb, D), tok),
            pl.BlockSpec((tb, PEER_SEL), tok),
            pl.BlockSpec((1, D), lambda t: (0, 0)),
            pl.BlockSpec(memory_space=pl.ANY),
        ],
        out_specs=pl.BlockSpec((tb, D), lambda t: (t, 0)),
        out_shape=jax.ShapeDtypeStruct((n_out, D), F32),
        scratch_shapes=[
            pltpu.SMEM((tb * PEER_SEL,), jnp.int32),
            pltpu.VMEM((2, PEER_SEL // TC_ISSUE_UNROLL, TC_ISSUE_UNROLL, D), jnp.int32),
            pltpu.SemaphoreType.DMA((2,)),
            pltpu.SemaphoreType.DMA,
        ],
        compiler_params=pltpu.CompilerParams(
            dimension_semantics=("arbitrary",), vmem_limit_bytes=VMEM_LIMIT_SMALL,
            disable_bounds_checks=True),
        name="peer_expert_tc",
    )(eidx.reshape(-1), x, hn, g, g2, uv_tab)


TC_SHARE_TOKENS = 1536


def _peer_layer(x, sh2, sc2, g2, ng, wqt, sk, uv_tab):
    B, T, D = x.shape
    hn, eidx, g = _peer_route(x, sh2, sc2, ng, wqt, sk)
    N = B * T
    flat = (x.reshape(N, D), hn.reshape(N, D), eidx.reshape(N, PEER_SEL), g.reshape(N, PEER_SEL),
            g2.reshape(B, D), uv_tab)
    n_tc = TC_SHARE_TOKENS if (B == 1 and N >= 4 * TC_SHARE_TOKENS) else 0
    y = _peer_expert_sc(*flat, tokens_per_seq=T, n_tokens=N - n_tc)
    if n_tc:
        y = jnp.concatenate([y, _peer_expert_tc(*flat, first_token=N - n_tc)], axis=0)
    return y.reshape(B, T, D)


def _attn_layer_kernel(*refs, tq, ch, has_cache):
    if has_cache:
        (x_ref, sh_ref, sc_ref, g1_ref, ng_ref, wqkv_ref, qg_ref, kg_ref, bias_ref, wo_ref, kc_ref, vc_ref,
         y_ref, ko_ref, vo_ref, kcat, vcat, qn_s, o_s) = refs
    else:
        (x_ref, sh_ref, sc_ref, g1_ref, ng_ref, wqkv_ref, qg_ref, kg_ref, bias_ref, wo_ref,
         y_ref, ko_ref, vo_ref, kcat, vcat, qn_s, o_s) = refs
    t = pl.program_id(1)
    band = LEFT_ROWS + ch

    if has_cache:
        kcat[0:LEFT_ROWS, :] = kc_ref[0].astype(BF16)
        vcat[0:LEFT_ROWS, :] = vc_ref[0].astype(BF16)
    else:
        @pl.when(t == 0)
        def _():
            kcat[0:LEFT_ROWS, :] = jnp.zeros((LEFT_ROWS, D_MODEL), BF16)
            vcat[0:LEFT_ROWS, :] = jnp.zeros((LEFT_ROWS, D_MODEL), BF16)

    x = x_ref[0]
    ms = jnp.mean(x * x, axis=-1, keepdims=True)
    hn = x * lax.rsqrt(ms + EPS) * ng_ref[...]
    hn = hn * (1.0 + sc_ref[0]) + sh_ref[0]
    qkv = jnp.dot(hn.astype(BF16), wqkv_ref[...], preferred_element_type=F32)
    for h in range(N_HEADS):
        lo = h * HEAD_DIM
        qh = qkv[:, lo:lo + HEAD_DIM]
        kh = qkv[:, D_MODEL + lo:D_MODEL + lo + HEAD_DIM]
        qh = qh * lax.rsqrt(jnp.mean(qh * qh, axis=-1, keepdims=True) + EPS) * qg_ref[...]
        kh = kh * lax.rsqrt(jnp.mean(kh * kh, axis=-1, keepdims=True) + EPS) * kg_ref[...]
        qn_s[:, lo:lo + HEAD_DIM] = qh.astype(BF16)
        ko_ref[0, :, lo:lo + HEAD_DIM] = kh
        kcat[LEFT_ROWS:LEFT_ROWS + tq, lo:lo + HEAD_DIM] = kh.astype(BF16)
    v = qkv[:, 2 * D_MODEL:]
    vo_ref[0] = v
    vcat[LEFT_ROWS:LEFT_ROWS + tq, :] = v.astype(BF16)

    scale = HEAD_DIM ** -0.5
    kpos_rel = lax.broadcasted_iota(jnp.int32, (ch, band), 1)

    def chunk_body(c, _):
        r0 = pl.multiple_of(c * ch, ch)
        if not has_cache:
            valid = kpos_rel + (t * tq + r0) >= LEFT_ROWS
        for h in range(N_HEADS):
            lo = h * HEAD_DIM
            q = qn_s[pl.ds(r0, ch), lo:lo + HEAD_DIM]
            kb = kcat[pl.ds(r0, band), lo:lo + HEAD_DIM]
            vb = vcat[pl.ds(r0, band), lo:lo + HEAD_DIM]
            s = lax.dot_general(q, kb, (((1,), (1,)), ((), ())), preferred_element_type=F32) * scale
            s = s + bias_ref[h, 0:ch, 0:band]
            if not has_cache:
                s = jnp.where(valid, s, NEG_INF)
            m = jnp.max(s, axis=-1, keepdims=True)
            e = jnp.exp(s - m)
            p = e / jnp.sum(e, axis=-1, keepdims=True)
            o_s[pl.ds(r0, ch), lo:lo + HEAD_DIM] = jnp.dot(p.astype(BF16), vb, preferred_element_type=F32).astype(BF16)
        return 0

    lax.fori_loop(0, tq // ch, chunk_body, 0)

    out = jnp.dot(o_s[...], wo_ref[...], preferred_element_type=F32)
    y_ref[0] = x + g1_ref[0] * out

    if not has_cache:
        kcat[0:LEFT_ROWS, :] = kcat[tq:tq + LEFT_ROWS, :]
        vcat[0:LEFT_ROWS, :] = vcat[tq:tq + LEFT_ROWS, :]


def _rel_bias_tile(rel_bias):
    i = jnp.arange(CHUNK)[:, None]
    r = jnp.arange(LEFT_ROWS + CHUNK)[None, :]
    rel = jnp.clip(i + LEFT_ROWS - r, -REL_CLIP, REL_CLIP) + REL_CLIP
    return rel_bias[:, rel]


def _attn_layer(x, sh, sc, g1, ng, wqkv, qg, kg, bias, wo, k_cache=None, v_cache=None):
    B, T, D = x.shape
    has_cache = k_cache is not None
    if has_cache:
        tq, ch = T, T
        keep_blocks = 1
    else:
        tq, ch = LAYER_TILE, CHUNK
        keep_blocks = LEFT_ROWS // tq
    nt = T // tq
    assert nt >= keep_blocks
    tok = lambda b, t: (b, t, 0)
    per_b = lambda b, t: (b, 0, 0)
    c2 = lambda b, t: (0, 0)
    c3 = lambda b, t: (0, 0, 0)
    keep = lambda b, t: (b, jnp.maximum(t - (nt - keep_blocks), 0), 0)
    in_specs = [
        pl.BlockSpec((1, tq, D), tok),
        pl.BlockSpec((1, 1, D), per_b), pl.BlockSpec((1, 1, D), per_b), pl.BlockSpec((1, 1, D), per_b),
        pl.BlockSpec((1, D), c2),
        pl.BlockSpec(wqkv.shape, c2),
        pl.BlockSpec((1, HEAD_DIM), c2), pl.BlockSpec((1, HEAD_DIM), c2),
        pl.BlockSpec(bias.shape, c3),
        pl.BlockSpec(wo.shape, c2),
    ]
    args = [x, sh, sc, g1, ng, wqkv, qg, kg, bias, wo]
    if has_cache:
        in_specs += [pl.BlockSpec((1, LEFT_ROWS, D), per_b), pl.BlockSpec((1, LEFT_ROWS, D), per_b)]
        args += [k_cache, v_cache]
    return pl.pallas_call(
        functools.partial(_attn_layer_kernel, tq=tq, ch=ch, has_cache=has_cache),
        grid=(B, nt),
        in_specs=in_specs,
        out_specs=[pl.BlockSpec((1, tq, D), tok), pl.BlockSpec((1, tq, D), keep), pl.BlockSpec((1, tq, D), keep)],
        out_shape=[jax.ShapeDtypeStruct((B, T, D), F32),
                   jax.ShapeDtypeStruct((B, keep_blocks * tq, D), F32),
                   jax.ShapeDtypeStruct((B, keep_blocks * tq, D), F32)],
        scratch_shapes=[
            pltpu.VMEM((LEFT_ROWS + tq, D), BF16),
            pltpu.VMEM((LEFT_ROWS + tq, D), BF16),
            pltpu.VMEM((tq, D), BF16),
            pltpu.VMEM((tq, D), BF16),
        ],
        compiler_params=pltpu.CompilerParams(
            dimension_semantics=("parallel", "arbitrary"), vmem_limit_bytes=VMEM_LIMIT_LAYER),
        name="attn_layer",
    )(*args)


def _gelu_tanh(x):
    return 0.5 * x * (1.0 + jnp.tanh(0.7978845608028654 * (x + 0.044715 * (x * x * x))))


def _lru_layer_kernel(x_ref, sh_ref, sc_ref, g1_ref, ng_ref, win_ref, cw_ref, cb_ref, gw_ref, gb_ref, lam_ref,
                      wout_ref, cs0_ref, h0_ref,
                      y_ref, cs_ref, hl_ref,
                      ext_s, a_s, b_s, hcar_s):
    tt = x_ref.shape[1]
    W = LRU_WIDTH
    t = pl.program_id(1)

    @pl.when(t == 0)
    def _():
        ext_s[0:SUBLANES, :] = cs0_ref[0]
        hcar_s[...] = h0_ref[0]

    x = x_ref[0]
    ms = jnp.mean(x * x, axis=-1, keepdims=True)
    hn = x * lax.rsqrt(ms + EPS) * ng_ref[...]
    hn = hn * (1.0 + sc_ref[0]) + sh_ref[0]
    proj = jnp.dot(hn.astype(BF16), win_ref[...], preferred_element_type=F32)
    gate = _gelu_tanh(proj[:, :W])
    rec = proj[:, W:]
    ext_s[SUBLANES:SUBLANES + tt, :] = rec
    u = cb_ref[...] + cw_ref[3:4, :] * rec
    for k in range(CONV_W - 1):
        u = u + cw_ref[k:k + 1, :] * ext_s[SUBLANES - 3 + k:SUBLANES - 3 + k + tt, :]
    tail = ext_s[tt:tt + SUBLANES, :]
    cs_ref[0] = tail
    ext_s[0:SUBLANES, :] = tail

    sp_in = -lam_ref[...]
    softplus = jnp.maximum(sp_in, 0.0) + jnp.log1p(jnp.exp(-jnp.abs(sp_in)))
    for n in range(LRU_BLOCKS):
        lo = n * LRU_BLOCK_W
        ub = u[:, lo:lo + LRU_BLOCK_W]
        ubb = ub.astype(BF16)
        gr = jnp.dot(ubb, gw_ref[0, n], preferred_element_type=F32) + gb_ref[0:1, lo:lo + LRU_BLOCK_W]
        gi = jnp.dot(ubb, gw_ref[1, n], preferred_element_type=F32) + gb_ref[1:2, lo:lo + LRU_BLOCK_W]
        r = jax.nn.sigmoid(gr)
        i = jax.nn.sigmoid(gi)
        log_a = -RG_C * r * softplus[:, lo:lo + LRU_BLOCK_W]
        a = jnp.exp(log_a)
        a_s[:, lo:lo + LRU_BLOCK_W] = a
        b_s[:, lo:lo + LRU_BLOCK_W] = jnp.sqrt(1.0 - a * a) * i * ub

    def group(gi_, h):
        r0 = pl.multiple_of(gi_ * SUBLANES, SUBLANES)
        a8 = a_s[pl.ds(r0, SUBLANES), :]
        b8 = b_s[pl.ds(r0, SUBLANES), :]
        rows = []
        for j in range(SUBLANES):
            h = a8[j:j + 1, :] * h + b8[j:j + 1, :]
            rows.append(h)
        b_s[pl.ds(r0, SUBLANES), :] = jnp.concatenate(rows, axis=0)
        return h

    h_last = lax.fori_loop(0, tt // SUBLANES, group, hcar_s[...])
    hcar_s[...] = h_last
    hl_ref[0] = h_last
    yv = b_s[...] * gate
    out = jnp.dot(yv.astype(BF16), wout_ref[...], preferred_element_type=F32)
    y_ref[0] = x + g1_ref[0] * out


def _lru_layer(x, sh, sc, g1, ng, w_in, conv_w, conv_b, gate_w, gate_b, lam, w_out, conv_state, h0):
    B, T, D = x.shape
    W = LRU_WIDTH
    tt = min(T, LAYER_TILE)
    cs0 = jnp.pad(conv_state, ((0, 0), (SUBLANES - (CONV_W - 1), 0), (0, 0)))
    tok = lambda b, t: (b, t, 0)
    per_b = lambda b, t: (b, 0, 0)
    c2 = lambda b, t: (0, 0)
    y, cs, hl = pl.pallas_call(
        _lru_layer_kernel,
        grid=(B, T // tt),
        in_specs=[
            pl.BlockSpec((1, tt, D), tok),
            pl.BlockSpec((1, 1, D), per_b), pl.BlockSpec((1, 1, D), per_b), pl.BlockSpec((1, 1, D), per_b),
            pl.BlockSpec((1, D), c2),
            pl.BlockSpec(w_in.shape, c2),
            pl.BlockSpec((CONV_W, W), c2), pl.BlockSpec((1, W), c2),
            pl.BlockSpec(gate_w.shape, lambda b, t: (0, 0, 0, 0)),
            pl.BlockSpec((2, W), c2), pl.BlockSpec((1, W), c2),
            pl.BlockSpec(w_out.shape, c2),
            pl.BlockSpec((1, SUBLANES, W), per_b), pl.BlockSpec((1, 1, W), per_b),
        ],
        out_specs=[pl.BlockSpec((1, tt, D), tok), pl.BlockSpec((1, SUBLANES, W), per_b),
                   pl.BlockSpec((1, 1, W), per_b)],
        out_shape=[jax.ShapeDtypeStruct((B, T, D), F32), jax.ShapeDtypeStruct((B, SUBLANES, W), F32),
                   jax.ShapeDtypeStruct((B, 1, W), F32)],
        scratch_shapes=[
            pltpu.VMEM((tt + SUBLANES, W), F32),
            pltpu.VMEM((tt, W), F32),
            pltpu.VMEM((tt, W), F32),
            pltpu.VMEM((1, W), F32),
        ],
        compiler_params=pltpu.CompilerParams(
            dimension_semantics=("parallel", "arbitrary"), vmem_limit_bytes=VMEM_LIMIT_LAYER),
        name="lru_layer",
    )(x, sh, sc, g1, ng, w_in, conv_w, conv_b, gate_w, gate_b, lam, w_out, cs0, h0[:, None, :])
    return y, cs[:, SUBLANES - (CONV_W - 1):], hl[:, 0]


def _ada_kernel(c_ref, w_ref, b_ref, m_ref):
    c = c_ref[...]
    s = c * jax.nn.sigmoid(c)
    m_ref[0] = jnp.dot(s, w_ref[0], preferred_element_type=F32, precision=lax.Precision.HIGHEST) + b_ref[0]


def _ada_all(c_all, ada_w, ada_b):
    L, D, E = ada_w.shape
    R = c_all.shape[0]
    tn = ADA_TILE
    return pl.pallas_call(
        _ada_kernel,
        grid=(L, E // tn),
        in_specs=[pl.BlockSpec((R, D), lambda l, j: (0, 0)),
                  pl.BlockSpec((1, D, tn), lambda l, j: (l, 0, j)),
                  pl.BlockSpec((1, 1, tn), lambda l, j: (l, 0, j))],
        out_specs=pl.BlockSpec((1, R, tn), lambda l, j: (l, 0, j)),
        out_shape=jax.ShapeDtypeStruct((L, R, E), F32),
        compiler_params=pltpu.CompilerParams(
            dimension_semantics=("parallel", "parallel"), vmem_limit_bytes=VMEM_LIMIT_SMALL),
        name="ada_mod",
    )(c_all, ada_w, ada_b[:, None, :])


class _Chain:
    def __init__(self, x, mods, conv_st, lru_st, k_cache, v_cache, w, prompt):
        self.x, self.mods, self.w, self.prompt = x, mods, w, prompt
        self.conv_st, self.lru_st, self.k_cache, self.v_cache = conv_st, lru_st, k_cache, v_cache
        self.new_conv, self.new_h, self.new_k, self.new_v = [], [], [], []
        self.layer = 0

    def step(self):
        i, w, x = self.layer, self.w, self.x
        B, T, D = x.shape
        sh1, sc1, g1, sh2, sc2, g2 = self.mods[i]
        j = i // 2
        if i % 2 == 0:
            if self.prompt:
                cs0 = jnp.zeros((B, CONV_W - 1, LRU_WIDTH), x.dtype)
                h0 = jnp.zeros((B, LRU_WIDTH), F32)
            else:
                cs0, h0 = self.conv_st[j], self.lru_st[j]
            x, cnew, hnew = _lru_layer(x, sh1, sc1, g1, w['norm_g'][i, 0][None, :], w['lru_w_in'][j],
                                       w['lru_conv_w'][j], w['lru_conv_b'][j][None, :], w['lru_gate_w'][j],
                                       w['lru_gate_b'][j], w['lru_lambda'][j][None, :], w['lru_w_out'][j], cs0, h0)
            self.new_conv.append(cnew)
            self.new_h.append(hnew)
        else:
            if self.prompt:
                kc = vc = None
            else:
                kc = self.k_cache[j].reshape(B, LEFT_ROWS, D)
                vc = self.v_cache[j].reshape(B, LEFT_ROWS, D)
            x, kn, vn = _attn_layer(x, sh1, sc1, g1, w['norm_g'][i, 0][None, :], w['att_w_qkv'][j],
                                    w['att_q_gain'][j][None, :], w['att_k_gain'][j][None, :], w['att_bias'][j],
                                    w['att_w_o'][j], kc, vc)
            self.new_k.append(kn.reshape(B, kn.shape[1], N_HEADS, HEAD_DIM))
            self.new_v.append(vn.reshape(B, vn.shape[1], N_HEADS, HEAD_DIM))
        self.x = _peer_layer(x, sh2, sc2, g2, w['norm_g'][i, 1][None, :], w['peer_wqt'][i], w['peer_sk'][i],
                             w['peer_uv'][i])
        self.layer += 1

    def results(self):
        return (self.x, jnp.stack(self.new_conv), jnp.stack(self.new_h), jnp.stack(self.new_k),
                jnp.stack(self.new_v))


def _run_chains(chains, depth):
    for wave in range(depth + len(chains) - 1):
        for ci, chain in enumerate(chains):
            if 0 <= wave - ci < depth:
                chain.step()


def kernel(x_prompt, x_sample, c_prompt, c_sample, state_conv, state_lru_h, cache_k, cache_v, norm_g, ada_w, ada_b, lru_w_in, lru_conv_w, lru_conv_b, lru_gate_w, lru_gate_b, lru_lambda, lru_w_out, att_w_qkv, att_q_gain, att_k_gain, att_rel_bias, att_w_o, peer_w_query, peer_sub_keys, peer_u, peer_v):
    depth = norm_g.shape[0]
    bp, bs = c_prompt.shape[0], c_sample.shape[0]
    rows = -(-(bp + bs) // SUBLANES) * SUBLANES
    c_all = jnp.concatenate([c_prompt, c_sample, jnp.zeros((rows - bp - bs, D_MODEL), F32)], axis=0)
    m_all = _ada_all(c_all, ada_w, ada_b)

    def mods(lo, n):
        return [[m_all[i, lo:lo + n, None, k * D_MODEL:(k + 1) * D_MODEL] for k in range(6)] for i in range(depth)]

    w = {
        'norm_g': norm_g,
        'lru_w_in': lru_w_in.astype(BF16), 'lru_conv_w': lru_conv_w, 'lru_conv_b': lru_conv_b,
        'lru_gate_w': lru_gate_w.astype(BF16), 'lru_gate_b': lru_gate_b, 'lru_lambda': lru_lambda,
        'lru_w_out': lru_w_out.astype(BF16),
        'att_w_qkv': att_w_qkv.astype(BF16), 'att_q_gain': att_q_gain, 'att_k_gain': att_k_gain,
        'att_bias': jax.vmap(_rel_bias_tile)(att_rel_bias), 'att_w_o': att_w_o.astype(BF16),
        'peer_wqt': jnp.swapaxes(peer_w_query, 1, 2).astype(BF16),
        'peer_sk': peer_sub_keys.astype(BF16).reshape(depth, 2 * PEER_HEADS, N_KEYS, PEER_HALF),
        'peer_uv': [_pack_expert_tables(peer_u[i], peer_v[i]) for i in range(depth)],
    }
    prompt_chains = [_Chain(x_prompt[b:b + 1], mods(b, 1), None, None, None, None, w, True) for b in range(bp)]
    sample_chain = _Chain(x_sample, mods(bp, bs), state_conv, state_lru_h, cache_k, cache_v, w, False)
    _run_chains(prompt_chains + [sample_chain], depth)
    y_prompt, p_conv, p_h, p_k, p_v = (
        jnp.concatenate(parts, axis=ax)
        for parts, ax in zip(zip(*(c.results() for c in prompt_chains)), (0, 1, 1, 1, 1)))
    y_sample, s_conv, s_h, s_k, s_v = sample_chain.results()
    return (y_prompt, y_sample, p_conv, p_h, p_k, p_v, s_conv, s_h, s_k, s_v)
```

```python
import functools

import jax
import jax.numpy as jnp
import numpy as np
from jax import lax
from jax.experimental import pallas as pl
from jax.experimental.pallas import tpu as pltpu
from jax.experimental.pallas import tpu_sc as plsc

D_MODEL = 1024
CHUNK = 64
LRU_WIDTH = D_MODEL
LRU_BLOCKS = 4
LRU_BLOCK_W = LRU_WIDTH // LRU_BLOCKS
CONV_W = 4
RG_C = 8.0
N_HEADS = 8
HEAD_DIM = D_MODEL // N_HEADS
LEFT_CHUNKS = 8
LEFT_ROWS = LEFT_CHUNKS * CHUNK
REL_CLIP = 128
PEER_HEADS = 8
N_KEYS = 128
PEER_TOPK = 16
PEER_QDIM = 256
PEER_HALF = PEER_QDIM // 2
PEER_SEL = PEER_HEADS * PEER_TOPK
EPS = 1e-6
NEG_INF = -1e30

LANES = 128
SUBLANES = 8
V7X_VMEM_BYTES = 64 * 1024 * 1024
VMEM_LIMIT_LAYER = (3 * V7X_VMEM_BYTES) // 4
VMEM_LIMIT_SMALL = V7X_VMEM_BYTES // 2
ROUTE_TILE = 512
LAYER_TILE = 256
ADA_TILE = 1536
BF16 = jnp.bfloat16
F32 = jnp.float32


_CAND_RUNS = [(a, PEER_TOPK // (a + 1)) for a in range(PEER_TOPK)]
_NCAND = sum(n for _, n in _CAND_RUNS)
_NCAND_PAD = -(-_NCAND // SUBLANES) * SUBLANES
_CAND_FLAT = np.full((_NCAND_PAD,), PEER_TOPK * PEER_TOPK, np.int32)
_off = 0
for _a, _n in _CAND_RUNS:
    _CAND_FLAT[_off:_off + _n] = _a * PEER_TOPK + np.arange(_n)
    _off += _n


def _peer_route_kernel(x_ref, sh_ref, sc_ref, ng_ref, wqt_ref, sk_ref, cand_ref,
                       hn_ref, idx_ref, g_ref,
                       q_s, cs_s, ci_s):
    tt = x_ref.shape[1]
    x = x_ref[0]
    ms = jnp.mean(x * x, axis=-1, keepdims=True)
    hn = x * lax.rsqrt(ms + EPS) * ng_ref[...]
    hn = hn * (1.0 + sc_ref[0]) + sh_ref[0]
    hn_ref[0] = hn
    qt = lax.dot_general(wqt_ref[...], hn.astype(BF16), (((1,), (1,)), ((), ())),
                         preferred_element_type=F32)
    for gidx in range(2 * PEER_HEADS):
        q_s[gidx] = qt[gidx * PEER_HALF:(gidx + 1) * PEER_HALF, :]

    key_id = lax.broadcasted_iota(jnp.int32, (N_KEYS, tt), 0)
    slot_id = lax.broadcasted_iota(jnp.int32, (PEER_TOPK, tt), 0)
    cand = cand_ref[:, 0:tt]
    big = PEER_TOPK * PEER_TOPK
    cs_s[_NCAND:_NCAND_PAD, :] = jnp.full((_NCAND_PAD - _NCAND, tt), -jnp.inf, F32)
    ci_s[_NCAND:_NCAND_PAD, :] = jnp.zeros((_NCAND_PAD - _NCAND, tt), jnp.int32)

    def head_body(h, _):
        s = [jnp.dot(sk_ref[2 * h + p], q_s[2 * h + p].astype(BF16), preferred_element_type=F32)
             for p in range(2)]
        ts = [jnp.zeros((PEER_TOPK, tt), F32) for _ in range(2)]
        ti = [jnp.zeros((PEER_TOPK, tt), jnp.int32) for _ in range(2)]
        for j in range(PEER_TOPK):
            for p in range(2):
                m = jnp.max(s[p], axis=0, keepdims=True)
                am = jnp.min(jnp.where(s[p] == m, key_id, N_KEYS), axis=0, keepdims=True)
                ts[p] = jnp.where(slot_id == j, m, ts[p])
                ti[p] = jnp.where(slot_id == j, am, ti[p])
                s[p] = jnp.where(key_id == am, -jnp.inf, s[p])
        off = 0
        for a, n in _CAND_RUNS:
            cs_s[off:off + n, :] = ts[0][a:a + 1, :] + ts[1][0:n, :]
            ci_s[off:off + n, :] = ti[0][a:a + 1, :] * N_KEYS + ti[1][0:n, :]
            off += n
        cs = cs_s[...]
        ci = ci_s[...]
        fs = jnp.zeros((PEER_TOPK, tt), F32)
        ei = jnp.zeros((PEER_TOPK, tt), jnp.int32)
        for k in range(PEER_TOPK):
            m = jnp.max(cs, axis=0, keepdims=True)
            cmin = jnp.min(jnp.where(cs == m, cand, big), axis=0, keepdims=True)
            sel = cand == cmin
            e = jnp.sum(jnp.where(sel, ci, 0), axis=0, keepdims=True)
            fs = jnp.where(slot_id == k, m, fs)
            ei = jnp.where(slot_id == k, e, ei)
            cs = jnp.where(sel, -jnp.inf, cs)
        ex = jnp.exp(fs - fs[0:1, :])
        g = ex / jnp.sum(ex, axis=0, keepdims=True)
        row0 = pl.multiple_of(h * PEER_TOPK, PEER_TOPK)
        g_ref[0, pl.ds(row0, PEER_TOPK), :] = g
        idx_ref[0, pl.ds(row0, PEER_TOPK), :] = ei
        return 0

    lax.fori_loop(0, PEER_HEADS, head_body, 0)


def _peer_route(x, sh, sc, ng, wqt, sk):
    B, T, D = x.shape
    tt = min(T, ROUTE_TILE)
    grid = (B, T // tt)
    tok = lambda b, t: (b, t, 0)
    tok_t = lambda b, t: (b, 0, t)
    per_b = lambda b, t: (b, 0, 0)
    cand_tab = jnp.asarray(np.broadcast_to(_CAND_FLAT[:, None], (_NCAND_PAD, max(tt, LANES))))
    hn, idx_t, g_t = pl.pallas_call(
        _peer_route_kernel,
        grid=grid,
        in_specs=[
            pl.BlockSpec((1, tt, D), tok),
            pl.BlockSpec((1, 1, D), per_b),
            pl.BlockSpec((1, 1, D), per_b),
            pl.BlockSpec((1, D), lambda b, t: (0, 0)),
            pl.BlockSpec(wqt.shape, lambda b, t: (0, 0)),
            pl.BlockSpec(sk.shape, lambda b, t: (0, 0, 0)),
            pl.BlockSpec(cand_tab.shape, lambda b, t: (0, 0)),
        ],
        out_specs=[
            pl.BlockSpec((1, tt, D), tok),
            pl.BlockSpec((1, PEER_SEL, tt), tok_t),
            pl.BlockSpec((1, PEER_SEL, tt), tok_t),
        ],
        out_shape=[
            jax.ShapeDtypeStruct((B, T, D), F32),
            jax.ShapeDtypeStruct((B, PEER_SEL, T), jnp.int32),
            jax.ShapeDtypeStruct((B, PEER_SEL, T), F32),
        ],
        scratch_shapes=[
            pltpu.VMEM((2 * PEER_HEADS, PEER_HALF, tt), F32),
            pltpu.VMEM((_NCAND_PAD, tt), F32),
            pltpu.VMEM((_NCAND_PAD, tt), jnp.int32),
        ],
        compiler_params=pltpu.CompilerParams(
            dimension_semantics=("parallel", "parallel"), vmem_limit_bytes=VMEM_LIMIT_LAYER),
        name="peer_route",
    )(x, sh, sc, ng, wqt, sk, cand_tab)
    return hn, jnp.swapaxes(idx_t, 1, 2), jnp.swapaxes(g_t, 1, 2)


SC_CORES = 2
SC_SUBCORES = 16
SC_LANES = 16
SC_WORKERS = SC_CORES * SC_SUBCORES
NCHUNK = D_MODEL // SC_LANES
HALF_D = D_MODEL // 2
NWCHUNK = HALF_D // SC_LANES
HI_MASK = -65536
TOK_BLOCKS = (16, 8)
NSLOT = 3


def _pack_expert_tables(u_tab, v_tab):
    def pack(tab):
        bits = lax.bitcast_convert_type(tab.astype(BF16), jnp.uint16).astype(jnp.uint32)
        return bits[:, :HALF_D] | (bits[:, HALF_D:] << 16)

    return lax.bitcast_convert_type(jnp.concatenate([pack(u_tab), pack(v_tab)], axis=1), jnp.int32)


def _unpack_words(words):
    lo = lax.bitcast_convert_type(lax.shift_left(words, jnp.full_like(words, 16)), F32)
    hi = lax.bitcast_convert_type(words & HI_MASK, F32)
    return lo, hi


def _peer_expert_sc(hn, eidx, g, uv_tab, n_tokens):
    N, D = n_tokens, hn.shape[1]
    tpw = N // SC_WORKERS
    assert tpw * SC_WORKERS == N and tpw % min(TOK_BLOCKS) == 0
    tok_block = next(b for b in TOK_BLOCKS if tpw % b == 0)
    nblocks = tpw // tok_block
    steps = tok_block * PEER_HEADS
    mesh = plsc.VectorSubcoreMesh(core_axis_name="c", subcore_axis_name="s",
                                  num_cores=SC_CORES, num_subcores=SC_SUBCORES)

    @functools.partial(
        pl.kernel, out_type=jax.ShapeDtypeStruct((N, D), F32), mesh=mesh,
        scratch_types=[
            pltpu.VMEM((tok_block, D), F32),
            pltpu.VMEM((tok_block, D), F32),
            pltpu.VMEM((tok_block, PEER_SEL), jnp.int32),
            pltpu.VMEM((tok_block, PEER_SEL), F32),
            pltpu.VMEM((NSLOT, PEER_TOPK, D), jnp.int32),
            pltpu.SemaphoreType.DMA((NSLOT,)),
        ],
        compiler_params=pltpu.CompilerParams(needs_layout_passes=False),
        name="peer_expert_sc",
    )
    def k(hn_hbm, idx_hbm, g_hbm, uv_hbm, out_hbm,
          hn_b, o_b, idx_b, g_b, uvbuf, sem):
        wid = lax.axis_index("s") * SC_CORES + lax.axis_index("c")
        tok0 = wid * tpw
        lane = lax.iota(jnp.int32, SC_LANES)

        def gather_desc(s, slot):
            t = s // PEER_HEADS
            h0 = pl.multiple_of((s % PEER_HEADS) * PEER_TOPK, PEER_TOPK)
            ids = idx_b.at[t, pl.ds(h0, PEER_TOPK)]
            return pltpu.make_async_copy(uv_hbm.at[ids], uvbuf.at[slot], sem.at[slot])

        def issue(s, slot):
            gather_desc(s, slot).start()

        def wait(s, slot):
            gather_desc(s, slot).wait()

        def compute(s, slot):
            t = s // PEER_HEADS
            h = s % PEER_HEADS
            h0 = pl.multiple_of(h * PEER_TOPK, PEER_TOPK)

            def ubody(c, accs):
                c0 = pl.multiple_of(c * SC_LANES, SC_LANES)
                x_lo = hn_b[t, pl.ds(c0, SC_LANES)]
                x_hi = hn_b[t, pl.ds(HALF_D + c0, SC_LANES)]
                out = []
                for kk in range(PEER_TOPK):
                    lo, hi = _unpack_words(uvbuf[slot, kk, pl.ds(c0, SC_LANES)])
                    out.append(accs[kk] + (x_lo * lo + x_hi * hi))
                return tuple(out)

            accs = lax.fori_loop(0, NWCHUNK, ubody,
                                 tuple(jnp.zeros((SC_LANES,), F32) for _ in range(PEER_TOPK)))
            last = jnp.full((SC_LANES,), SC_LANES - 1, jnp.int32)
            r = jnp.zeros((SC_LANES,), F32)
            for kk in range(PEER_TOPK):
                tot = jnp.take_along_axis(plsc.cumsum(accs[kk]), last, axis=0)
                r = jnp.where(lane == kk, tot, r)
            gv = g_b[t, pl.ds(h0, PEER_TOPK)]
            z = 0.7978845608028654 * (r + 0.044715 * (r * r * r))
            w = gv * r / (1.0 + jnp.exp(-2.0 * z))
            wk = [jnp.take_along_axis(w, jnp.full((SC_LANES,), kk, jnp.int32), axis=0) for kk in range(PEER_TOPK)]

            def vloop(first_head):
                @plsc.parallel_loop(0, NWCHUNK, unroll=2)
                def _(c):
                    c0 = pl.multiple_of(c * SC_LANES, SC_LANES)
                    los, his = [], []
                    for kk in range(PEER_TOPK):
                        lo, hi = _unpack_words(uvbuf[slot, kk, pl.ds(HALF_D + c0, SC_LANES)])
                        los.append(wk[kk] * lo)
                        his.append(wk[kk] * hi)
                    while len(los) > 1:
                        los = [los[i] + los[i + 1] for i in range(0, len(los), 2)]
                        his = [his[i] + his[i + 1] for i in range(0, len(his), 2)]
                    if first_head:
                        o_b[t, pl.ds(c0, SC_LANES)] = los[0]
                        o_b[t, pl.ds(HALF_D + c0, SC_LANES)] = his[0]
                    else:
                        plsc.addupdate(o_b.at[t, pl.ds(c0, SC_LANES)], los[0])
                        plsc.addupdate(o_b.at[t, pl.ds(HALF_D + c0, SC_LANES)], his[0])

            @pl.when(h == 0)
            def _():
                vloop(True)

            @pl.when(h != 0)
            def _():
                vloop(False)

        @pl.loop(0, nblocks)
        def _(blk):
            tok = pl.multiple_of(tok0 + blk * tok_block, tok_block)
            pltpu.sync_copy(idx_hbm.at[pl.ds(tok, tok_block)], idx_b)
            pltpu.sync_copy(g_hbm.at[pl.ds(tok, tok_block)], g_b)
            pltpu.sync_copy(hn_hbm.at[pl.ds(tok, tok_block)], hn_b)
            for s0 in range(NSLOT - 1):
                issue(s0, s0)

            @pl.loop(0, steps)
            def _(s):
                @pl.when(s + (NSLOT - 1) < steps)
                def _():
                    issue(s + (NSLOT - 1), (s + (NSLOT - 1)) % NSLOT)

                slot = s % NSLOT
                wait(s, slot)
                compute(s, slot)

            pltpu.sync_copy(o_b, out_hbm.at[pl.ds(tok, tok_block)])

    return k(hn, eidx, g, uv_tab)


TC_TOK_BLOCK = 64
TC_ISSUE_UNROLL = 32


def _peer_expert_tc_kernel(idx_hbm, x_ref, hn_ref, g_ref, g2_ref, uv_hbm, out_ref,
                           idx_s, uvbuf, sem, isem, *, first_block):
    tb = x_ref.shape[0]
    row0 = (pl.program_id(0) + first_block) * tb
    cp = pltpu.make_async_copy(idx_hbm.at[pl.ds(row0 * PEER_SEL, tb * PEER_SEL)], idx_s, isem)
    cp.start()
    cp.wait()

    def issue(t, slot):
        def body(jb, _):
            base = t * PEER_SEL + jb * TC_ISSUE_UNROLL
            for jj in range(TC_ISSUE_UNROLL):
                pltpu.make_async_copy(uv_hbm.at[pl.ds(idx_s[base + jj], 1)], uvbuf.at[slot, jb, pl.ds(jj, 1)],
                                      sem.at[slot]).start()
            return 0

        lax.fori_loop(0, PEER_SEL // TC_ISSUE_UNROLL, body, 0)

    def wait(slot):
        pltpu.make_async_copy(uv_hbm.at[pl.ds(0, PEER_SEL)], uvbuf.at[slot].reshape(PEER_SEL, D_MODEL),
                              sem.at[slot]).wait()

    issue(0, 0)
    r2 = lax.broadcasted_iota(jnp.int32, (PEER_SEL, LANES), 0)
    c2 = lax.broadcasted_iota(jnp.int32, (PEER_SEL, LANES), 1)
    diag = r2 == c2

    def tok(t, _):
        slot = t % 2

        @pl.when(t + 1 < tb)
        def _():
            issue(t + 1, 1 - slot)

        wait(slot)
        xr = hn_ref[pl.ds(t, 1), :]
        words = uvbuf[slot].reshape(PEER_SEL, D_MODEL)
        u_lo, u_hi = _unpack_words(words[:, :HALF_D])
        prod = u_lo * xr[:, :HALF_D] + u_hi * xr[:, HALF_D:]
        p = prod[:, 0:LANES]
        for c in range(1, HALF_D // LANES):
            p = p + prod[:, c * LANES:(c + 1) * LANES]
        act = jnp.sum(p, axis=-1, keepdims=True)
        grow = g_ref[pl.ds(t, 1), :]
        gcol = jnp.sum(jnp.where(diag, jnp.broadcast_to(grow, (PEER_SEL, LANES)), 0.0),
                       axis=-1, keepdims=True)
        wgt = gcol * _gelu_tanh(act)
        v_lo, v_hi = _unpack_words(words[:, HALF_D:])
        o = jnp.concatenate([jnp.sum(v_lo * wgt, axis=0, keepdims=True),
                             jnp.sum(v_hi * wgt, axis=0, keepdims=True)], axis=-1)
        out_ref[pl.ds(t, 1), :] = x_ref[pl.ds(t, 1), :] + g2_ref[...] * o
        return 0

    lax.fori_loop(0, tb, tok, 0)


def _peer_expert_tc(x, hn, eidx, g, g2, uv_tab, first_token):
    N, D = x.shape
    tb = TC_TOK_BLOCK
    n_out = N - first_token
    assert first_token % tb == 0 and n_out % tb == 0
    first_block = first_token // tb
    tok = lambda t: (t + first_block, 0)
    return pl.pallas_call(
        functools.partial(_peer_expert_tc_kernel, first_block=first_block),
        grid=(n_out // tb,),
        in_specs=[
            pl.BlockSpec(memory_space=pl.ANY),
            pl.BlockSpec((tb, D), tok),
            pl.BlockSpec((tb, D), tok),
            pl.BlockSpec((tb, PEER_SEL), tok),
            pl.BlockSpec((1, D), lambda t: (0, 0)),
            pl.BlockSpec(memory_space=pl.ANY),
        ],
        out_specs=pl.BlockSpec((tb, D), lambda t: (t, 0)),
        out_shape=jax.ShapeDtypeStruct((n_out, D), F32),
        scratch_shapes=[
            pltpu.SMEM((tb * PEER_SEL,), jnp.int32),
            pltpu.VMEM((2, PEER_SEL // TC_ISSUE_UNROLL, TC_ISSUE_UNROLL, D), jnp.int32),
            pltpu.SemaphoreType.DMA((2,)),
            pltpu.SemaphoreType.DMA,
        ],
        compiler_params=pltpu.CompilerParams(
            dimension_semantics=("arbitrary",), vmem_limit_bytes=VMEM_LIMIT_SMALL,
            disable_bounds_checks=True),
        name="peer_expert_tc",
    )(eidx.reshape(-1), x, hn, g, g2, uv_tab)


TC_SHARE_TOKENS = 1536


def _peer_combine_kernel(*refs, n_head_tiles, has_tail):
    if has_tail:
        x_ref, o_ref, g2_ref, tail_ref, y_ref = refs
    else:
        x_ref, o_ref, g2_ref, y_ref = refs
    i = pl.program_id(0)

    @pl.when(i < n_head_tiles)
    def _():
        y_ref[...] = x_ref[...] + g2_ref[0] * o_ref[...]

    if has_tail:
        @pl.when(i >= n_head_tiles)
        def _():
            y_ref[...] = tail_ref[...]


def _peer_combine(x, o_head, g2, y_tail, tokens_per_seq):
    N, D = x.shape
    nh = o_head.shape[0]
    rows = min(LAYER_TILE, tokens_per_seq)
    assert nh % rows == 0 and N % rows == 0 and tokens_per_seq % rows == 0
    n_head_tiles = nh // rows
    has_tail = y_tail is not None
    assert has_tail == (nh < N)
    in_specs = [
        pl.BlockSpec((rows, D), lambda i: (i, 0)),
        pl.BlockSpec((rows, D), lambda i: (jnp.minimum(i, n_head_tiles - 1), 0)),
        pl.BlockSpec((1, 1, D), lambda i: ((i * rows) // tokens_per_seq, 0, 0)),
    ]
    args = [x, o_head, g2]
    if has_tail:
        in_specs.append(pl.BlockSpec((rows, D), lambda i: (jnp.maximum(i - n_head_tiles, 0), 0)))
        args.append(y_tail)
    return pl.pallas_call(
        functools.partial(_peer_combine_kernel, n_head_tiles=n_head_tiles, has_tail=has_tail),
        grid=(N // rows,),
        in_specs=in_specs,
        out_specs=pl.BlockSpec((rows, D), lambda i: (i, 0)),
        out_shape=jax.ShapeDtypeStruct((N, D), F32),
        compiler_params=pltpu.CompilerParams(
            dimension_semantics=("parallel",), vmem_limit_bytes=VMEM_LIMIT_SMALL),
        name="peer_combine",
    )(*args)


def _peer_layer(x, sh2, sc2, g2, ng, wqt, sk, uv_tab):
    B, T, D = x.shape
    hn, eidx, g = _peer_route(x, sh2, sc2, ng, wqt, sk)
    N = B * T
    x2, hn2, eidx2, g_2 = x.reshape(N, D), hn.reshape(N, D), eidx.reshape(N, PEER_SEL), g.reshape(N, PEER_SEL)
    n_tc = TC_SHARE_TOKENS if (B == 1 and N >= 4 * TC_SHARE_TOKENS) else 0
    o_head = _peer_expert_sc(hn2, eidx2, g_2, uv_tab, n_tokens=N - n_tc)
    y_tail = _peer_expert_tc(x2, hn2, eidx2, g_2, g2.reshape(B, D), uv_tab, first_token=N - n_tc) if n_tc else None
    return _peer_combine(x2, o_head, g2, y_tail, tokens_per_seq=T).reshape(B, T, D)


def _attn_layer_kernel(*refs, tq, ch, has_cache):
    if has_cache:
        (x_ref, sh_ref, sc_ref, g1_ref, ng_ref, wqkv_ref, qg_ref, kg_ref, bias_ref, wo_ref, kc_ref, vc_ref,
         y_ref, ko_ref, vo_ref, kcat, vcat, qn_s, o_s) = refs
    else:
        (x_ref, sh_ref, sc_ref, g1_ref, ng_ref, wqkv_ref, qg_ref, kg_ref, bias_ref, wo_ref,
         y_ref, ko_ref, vo_ref, kcat, vcat, qn_s, o_s) = refs
    t = pl.program_id(1)
    band = LEFT_ROWS + ch

    if has_cache:
        kcat[0:LEFT_ROWS, :] = kc_ref[0].astype(BF16)
        vcat[0:LEFT_ROWS, :] = vc_ref[0].astype(BF16)
    else:
        @pl.when(t == 0)
        def _():
            kcat[0:LEFT_ROWS, :] = jnp.zeros((LEFT_ROWS, D_MODEL), BF16)
            vcat[0:LEFT_ROWS, :] = jnp.zeros((LEFT_ROWS, D_MODEL), BF16)

    x = x_ref[0]
    ms = jnp.mean(x * x, axis=-1, keepdims=True)
    hn = x * lax.rsqrt(ms + EPS) * ng_ref[...]
    hn = hn * (1.0 + sc_ref[0]) + sh_ref[0]
    qkv = jnp.dot(hn.astype(BF16), wqkv_ref[...], preferred_element_type=F32)
    for h in range(N_HEADS):
        lo = h * HEAD_DIM
        qh = qkv[:, lo:lo + HEAD_DIM]
        kh = qkv[:, D_MODEL + lo:D_MODEL + lo + HEAD_DIM]
        qh = qh * lax.rsqrt(jnp.mean(qh * qh, axis=-1, keepdims=True) + EPS) * qg_ref[...]
        kh = kh * lax.rsqrt(jnp.mean(kh * kh, axis=-1, keepdims=True) + EPS) * kg_ref[...]
        qn_s[:, lo:lo + HEAD_DIM] = qh.astype(BF16)
        ko_ref[0, :, lo:lo + HEAD_DIM] = kh
        kcat[LEFT_ROWS:LEFT_ROWS + tq, lo:lo + HEAD_DIM] = kh.astype(BF16)
    v = qkv[:, 2 * D_MODEL:]
    vo_ref[0] = v
    vcat[LEFT_ROWS:LEFT_ROWS + tq, :] = v.astype(BF16)

    scale = HEAD_DIM ** -0.5
    kpos_rel = lax.broadcasted_iota(jnp.int32, (ch, band), 1)

    def chunk_body(c, _):
        r0 = pl.multiple_of(c * ch, ch)
        if not has_cache:
            valid = kpos_rel + (t * tq + r0) >= LEFT_ROWS
        for h in range(N_HEADS):
            lo = h * HEAD_DIM
            q = qn_s[pl.ds(r0, ch), lo:lo + HEAD_DIM]
            kb = kcat[pl.ds(r0, band), lo:lo + HEAD_DIM]
            vb = vcat[pl.ds(r0, band), lo:lo + HEAD_DIM]
            s = lax.dot_general(q, kb, (((1,), (1,)), ((), ())), preferred_element_type=F32) * scale
            s = s + bias_ref[h, 0:ch, 0:band]
            if not has_cache:
                s = jnp.where(valid, s, NEG_INF)
            m = jnp.max(s, axis=-1, keepdims=True)
            e = jnp.exp(s - m)
            p = e / jnp.sum(e, axis=-1, keepdims=True)
            o_s[pl.ds(r0, ch), lo:lo + HEAD_DIM] = jnp.dot(p.astype(BF16), vb, preferred_element_type=F32).astype(BF16)
        return 0

    lax.fori_loop(0, tq // ch, chunk_body, 0)

    out = jnp.dot(o_s[...], wo_ref[...], preferred_element_type=F32)
    y_ref[0] = x + g1_ref[0] * out

    if not has_cache:
        kcat[0:LEFT_ROWS, :] = kcat[tq:tq + LEFT_ROWS, :]
        vcat[0:LEFT_ROWS, :] = vcat[tq:tq + LEFT_ROWS, :]


def _rel_bias_tile(rel_bias):
    i = jnp.arange(CHUNK)[:, None]
    r = jnp.arange(LEFT_ROWS + CHUNK)[None, :]
    rel = jnp.clip(i + LEFT_ROWS - r, -REL_CLIP, REL_CLIP) + REL_CLIP
    return rel_bias[:, rel]


def _attn_layer(x, sh, sc, g1, ng, wqkv, qg, kg, bias, wo, k_cache=None, v_cache=None):
    B, T, D = x.shape
    has_cache = k_cache is not None
    if has_cache:
        tq, ch = T, T
        keep_blocks = 1
    else:
        tq, ch = LAYER_TILE, CHUNK
        keep_blocks = LEFT_ROWS // tq
    nt = T // tq
    assert nt >= keep_blocks
    tok = lambda b, t: (b, t, 0)
    per_b = lambda b, t: (b, 0, 0)
    c2 = lambda b, t: (0, 0)
    c3 = lambda b, t: (0, 0, 0)
    keep = lambda b, t: (b, jnp.maximum(t - (nt - keep_blocks), 0), 0)
    in_specs = [
        pl.BlockSpec((1, tq, D), tok),
        pl.BlockSpec((1, 1, D), per_b), pl.BlockSpec((1, 1, D), per_b), pl.BlockSpec((1, 1, D), per_b),
        pl.BlockSpec((1, D), c2),
        pl.BlockSpec(wqkv.shape, c2),
        pl.BlockSpec((1, HEAD_DIM), c2), pl.BlockSpec((1, HEAD_DIM), c2),
        pl.BlockSpec(bias.shape, c3),
        pl.BlockSpec(wo.shape, c2),
    ]
    args = [x, sh, sc, g1, ng, wqkv, qg, kg, bias, wo]
    if has_cache:
        in_specs += [pl.BlockSpec((1, LEFT_ROWS, D), per_b), pl.BlockSpec((1, LEFT_ROWS, D), per_b)]
        args += [k_cache, v_cache]
    return pl.pallas_call(
        functools.partial(_attn_layer_kernel, tq=tq, ch=ch, has_cache=has_cache),
        grid=(B, nt),
        in_specs=in_specs,
        out_specs=[pl.BlockSpec((1, tq, D), tok), pl.BlockSpec((1, tq, D), keep), pl.BlockSpec((1, tq, D), keep)],
        out_shape=[jax.ShapeDtypeStruct((B, T, D), F32),
                   jax.ShapeDtypeStruct((B, keep_blocks * tq, D), F32),
                   jax.ShapeDtypeStruct((B, keep_blocks * tq, D), F32)],
        scratch_shapes=[
            pltpu.VMEM((LEFT_ROWS + tq, D), BF16),
            pltpu.VMEM((LEFT_ROWS + tq, D), BF16),
            pltpu.VMEM((tq, D), BF16),
            pltpu.VMEM((tq, D), BF16),
        ],
        compiler_params=pltpu.CompilerParams(
            dimension_semantics=("parallel", "arbitrary"), vmem_limit_bytes=VMEM_LIMIT_LAYER),
        name="attn_layer",
    )(*args)


def _gelu_tanh(x):
    return 0.5 * x * (1.0 + jnp.tanh(0.7978845608028654 * (x + 0.044715 * (x * x * x))))


def _lru_layer_kernel(x_ref, sh_ref, sc_ref, g1_ref, ng_ref, win_ref, cw_ref, cb_ref, gw_ref, gb_ref, lam_ref,
                      wout_ref, cs0_ref, h0_ref,
                      y_ref, cs_ref, hl_ref,
                      ext_s, a_s, b_s, hcar_s):
    tt = x_ref.shape[1]
    W = LRU_WIDTH
    t = pl.program_id(1)

    @pl.when(t == 0)
    def _():
        ext_s[0:SUBLANES, :] = cs0_ref[0]
        hcar_s[...] = h0_ref[0]

    x = x_ref[0]
    ms = jnp.mean(x * x, axis=-1, keepdims=True)
    hn = x * lax.rsqrt(ms + EPS) * ng_ref[...]
    hn = hn * (1.0 + sc_ref[0]) + sh_ref[0]
    proj = jnp.dot(hn.astype(BF16), win_ref[...], preferred_element_type=F32)
    gate = _gelu_tanh(proj[:, :W])
    rec = proj[:, W:]
    ext_s[SUBLANES:SUBLANES + tt, :] = rec
    u = cb_ref[...] + cw_ref[3:4, :] * rec
    for k in range(CONV_W - 1):
        u = u + cw_ref[k:k + 1, :] * ext_s[SUBLANES - 3 + k:SUBLANES - 3 + k + tt, :]
    tail = ext_s[tt:tt + SUBLANES, :]
    cs_ref[0] = tail
    ext_s[0:SUBLANES, :] = tail

    sp_in = -lam_ref[...]
    softplus = jnp.maximum(sp_in, 0.0) + jnp.log1p(jnp.exp(-jnp.abs(sp_in)))
    for n in range(LRU_BLOCKS):
        lo = n * LRU_BLOCK_W
        ub = u[:, lo:lo + LRU_BLOCK_W]
        ubb = ub.astype(BF16)
        gr = jnp.dot(ubb, gw_ref[0, n], preferred_element_type=F32) + gb_ref[0:1, lo:lo + LRU_BLOCK_W]
        gi = jnp.dot(ubb, gw_ref[1, n], preferred_element_type=F32) + gb_ref[1:2, lo:lo + LRU_BLOCK_W]
        r = jax.nn.sigmoid(gr)
        i = jax.nn.sigmoid(gi)
        log_a = -RG_C * r * softplus[:, lo:lo + LRU_BLOCK_W]
        a = jnp.exp(log_a)
        a_s[:, lo:lo + LRU_BLOCK_W] = a
        b_s[:, lo:lo + LRU_BLOCK_W] = jnp.sqrt(1.0 - a * a) * i * ub

    def group(gi_, h):
        r0 = pl.multiple_of(gi_ * SUBLANES, SUBLANES)
        a8 = a_s[pl.ds(r0, SUBLANES), :]
        b8 = b_s[pl.ds(r0, SUBLANES), :]
        rows = []
        for j in range(SUBLANES):
            h = a8[j:j + 1, :] * h + b8[j:j + 1, :]
            rows.append(h)
        b_s[pl.ds(r0, SUBLANES), :] = jnp.concatenate(rows, axis=0)
        return h

    h_last = lax.fori_loop(0, tt // SUBLANES, group, hcar_s[...])
    hcar_s[...] = h_last
    hl_ref[0] = h_last
    yv = b_s[...] * gate
    out = jnp.dot(yv.astype(BF16), wout_ref[...], preferred_element_type=F32)
    y_ref[0] = x + g1_ref[0] * out


def _lru_layer(x, sh, sc, g1, ng, w_in, conv_w, conv_b, gate_w, gate_b, lam, w_out, conv_state, h0):
    B, T, D = x.shape
    W = LRU_WIDTH
    tt = min(T, LAYER_TILE)
    cs0 = jnp.pad(conv_state, ((0, 0), (SUBLANES - (CONV_W - 1), 0), (0, 0)))
    tok = lambda b, t: (b, t, 0)
    per_b = lambda b, t: (b, 0, 0)
    c2 = lambda b, t: (0, 0)
    y, cs, hl = pl.pallas_call(
        _lru_layer_kernel,
        grid=(B, T // tt),
        in_specs=[
            pl.BlockSpec((1, tt, D), tok),
            pl.BlockSpec((1, 1, D), per_b), pl.BlockSpec((1, 1, D), per_b), pl.BlockSpec((1, 1, D), per_b),
            pl.BlockSpec((1, D), c2),
            pl.BlockSpec(w_in.shape, c2),
            pl.BlockSpec((CONV_W, W), c2), pl.BlockSpec((1, W), c2),
            pl.BlockSpec(gate_w.shape, lambda b, t: (0, 0, 0, 0)),
            pl.BlockSpec((2, W), c2), pl.BlockSpec((1, W), c2),
            pl.BlockSpec(w_out.shape, c2),
            pl.BlockSpec((1, SUBLANES, W), per_b), pl.BlockSpec((1, 1, W), per_b),
        ],
        out_specs=[pl.BlockSpec((1, tt, D), tok), pl.BlockSpec((1, SUBLANES, W), per_b),
                   pl.BlockSpec((1, 1, W), per_b)],
        out_shape=[jax.ShapeDtypeStruct((B, T, D), F32), jax.ShapeDtypeStruct((B, SUBLANES, W), F32),
                   jax.ShapeDtypeStruct((B, 1, W), F32)],
        scratch_shapes=[
            pltpu.VMEM((tt + SUBLANES, W), F32),
            pltpu.VMEM((tt, W), F32),
            pltpu.VMEM((tt, W), F32),
            pltpu.VMEM((1, W), F32),
        ],
        compiler_params=pltpu.CompilerParams(
            dimension_semantics=("parallel", "arbitrary"), vmem_limit_bytes=VMEM_LIMIT_LAYER),
        name="lru_layer",
    )(x, sh, sc, g1, ng, w_in, conv_w, conv_b, gate_w, gate_b, lam, w_out, cs0, h0[:, None, :])
    return y, cs[:, SUBLANES - (CONV_W - 1):], hl[:, 0]


def _ada_kernel(c_ref, w_ref, b_ref, m_ref):
    c = c_ref[...]
    s = c * jax.nn.sigmoid(c)
    m_ref[0] = jnp.dot(s, w_ref[0], preferred_element_type=F32, precision=lax.Precision.HIGHEST) + b_ref[0]


def _ada_all(c_all, ada_w, ada_b):
    L, D, E = ada_w.shape
    R = c_all.shape[0]
    tn = ADA_TILE
    return pl.pallas_call(
        _ada_kernel,
        grid=(L, E // tn),
        in_specs=[pl.BlockSpec((R, D), lambda l, j: (0, 0)),
                  pl.BlockSpec((1, D, tn), lambda l, j: (l, 0, j)),
                  pl.BlockSpec((1, 1, tn), lambda l, j: (l, 0, j))],
        out_specs=pl.BlockSpec((1, R, tn), lambda l, j: (l, 0, j)),
        out_shape=jax.ShapeDtypeStruct((L, R, E), F32),
        compiler_params=pltpu.CompilerParams(
            dimension_semantics=("parallel", "parallel"), vmem_limit_bytes=VMEM_LIMIT_SMALL),
        name="ada_mod",
    )(c_all, ada_w, ada_b[:, None, :])


class _Chain:
    def __init__(self, x, mods, conv_st, lru_st, k_cache, v_cache, w, prompt):
        self.x, self.mods, self.w, self.prompt = x, mods, w, prompt
        self.conv_st, self.lru_st, self.k_cache, self.v_cache = conv_st, lru_st, k_cache, v_cache
        self.new_conv, self.new_h, self.new_k, self.new_v = [], [], [], []
        self.layer = 0

    def step(self):
        i, w, x = self.layer, self.w, self.x
        B, T, D = x.shape
        sh1, sc1, g1, sh2, sc2, g2 = self.mods[i]
        j = i // 2
        if i % 2 == 0:
            if self.prompt:
                cs0 = jnp.zeros((B, CONV_W - 1, LRU_WIDTH), x.dtype)
                h0 = jnp.zeros((B, LRU_WIDTH), F32)
            else:
                cs0, h0 = self.conv_st[j], self.lru_st[j]
            x, cnew, hnew = _lru_layer(x, sh1, sc1, g1, w['norm_g'][i, 0][None, :], w['lru_w_in'][j],
                                       w['lru_conv_w'][j], w['lru_conv_b'][j][None, :], w['lru_gate_w'][j],
                                       w['lru_gate_b'][j], w['lru_lambda'][j][None, :], w['lru_w_out'][j], cs0, h0)
            self.new_conv.append(cnew)
            self.new_h.append(hnew)
        else:
            if self.prompt:
                kc = vc = None
            else:
                kc = self.k_cache[j].reshape(B, LEFT_ROWS, D)
                vc = self.v_cache[j].reshape(B, LEFT_ROWS, D)
            x, kn, vn = _attn_layer(x, sh1, sc1, g1, w['norm_g'][i, 0][None, :], w['att_w_qkv'][j],
                                    w['att_q_gain'][j][None, :], w['att_k_gain'][j][None, :], w['att_bias'][j],
                                    w['att_w_o'][j], kc, vc)
            self.new_k.append(kn.reshape(B, kn.shape[1], N_HEADS, HEAD_DIM))
            self.new_v.append(vn.reshape(B, vn.shape[1], N_HEADS, HEAD_DIM))
        self.x = _peer_layer(x, sh2, sc2, g2, w['norm_g'][i, 1][None, :], w['peer_wqt'][i], w['peer_sk'][i],
                             w['peer_uv'][i])
        self.layer += 1

    def results(self):
        return (self.x, jnp.stack(self.new_conv), jnp.stack(self.new_h), jnp.stack(self.new_k),
                jnp.stack(self.new_v))


def _run_chains(chains, depth):
    for wave in range(depth + len(chains) - 1):
        for ci, chain in enumerate(chains):
            if 0 <= wave - ci < depth:
                chain.step()


def kernel(x_prompt, x_sample, c_prompt, c_sample, state_conv, state_lru_h, cache_k, cache_v, norm_g, ada_w, ada_b, lru_w_in, lru_conv_w, lru_conv_b, lru_gate_w, lru_gate_b, lru_lambda, lru_w_out, att_w_qkv, att_q_gain, att_k_gain, att_rel_bias, att_w_o, peer_w_query, peer_sub_keys, peer_u, peer_v):
    depth = norm_g.shape[0]
    bp, bs = c_prompt.shape[0], c_sample.shape[0]
    rows = -(-(bp + bs) // SUBLANES) * SUBLANES
    c_all = jnp.concatenate([c_prompt, c_sample, jnp.zeros((rows - bp - bs, D_MODEL), F32)], axis=0)
    m_all = _ada_all(c_all, ada_w, ada_b)

    def mods(lo, n):
        return [[m_all[i, lo:lo + n, None, k * D_MODEL:(k + 1) * D_MODEL] for k in range(6)] for i in range(depth)]

    w = {
        'norm_g': norm_g,
        'lru_w_in': lru_w_in.astype(BF16), 'lru_conv_w': lru_conv_w, 'lru_conv_b': lru_conv_b,
        'lru_gate_w': lru_gate_w.astype(BF16), 'lru_gate_b': lru_gate_b, 'lru_lambda': lru_lambda,
        'lru_w_out': lru_w_out.astype(BF16),
        'att_w_qkv': att_w_qkv.astype(BF16), 'att_q_gain': att_q_gain, 'att_k_gain': att_k_gain,
        'att_bias': jax.vmap(_rel_bias_tile)(att_rel_bias), 'att_w_o': att_w_o.astype(BF16),
        'peer_wqt': jnp.swapaxes(peer_w_query, 1, 2).astype(BF16),
        'peer_sk': peer_sub_keys.astype(BF16).reshape(depth, 2 * PEER_HEADS, N_KEYS, PEER_HALF),
        'peer_uv': [_pack_expert_tables(peer_u[i], peer_v[i]) for i in range(depth)],
    }
    prompt_chains = [_Chain(x_prompt[b:b + 1], mods(b, 1), None, None, None, None, w, True) for b in range(bp)]
    sample_chain = _Chain(x_sample, mods(bp, bs), state_conv, state_lru_h, cache_k, cache_v, w, False)
    _run_chains(prompt_chains + [sample_chain], depth)
    y_prompt, p_conv, p_h, p_k, p_v = (
        jnp.concatenate(parts, axis=ax)
        for parts, ax in zip(zip(*(c.results() for c in prompt_chains)), (0, 1, 1, 1, 1)))
    y_sample, s_conv, s_h, s_k, s_v = sample_chain.results()
    return (y_prompt, y_sample, p_conv, p_h, p_k, p_v, s_conv, s_h, s_k, s_v)
```

```python
import functools

import jax
import jax.numpy as jnp
import numpy as np
from jax import lax
from jax.experimental import pallas as pl
from jax.experimental.pallas import tpu as pltpu
from jax.experimental.pallas import tpu_sc as plsc

D_MODEL = 1024
CHUNK = 64
LRU_WIDTH = D_MODEL
LRU_BLOCKS = 4
LRU_BLOCK_W = LRU_WIDTH // LRU_BLOCKS
CONV_W = 4
RG_C = 8.0
N_HEADS = 8
HEAD_DIM = D_MODEL // N_HEADS
LEFT_CHUNKS = 8
LEFT_ROWS = LEFT_CHUNKS * CHUNK
REL_CLIP = 128
PEER_HEADS = 8
N_KEYS = 128
PEER_TOPK = 16
PEER_QDIM = 256
PEER_HALF = PEER_QDIM // 2
PEER_SEL = PEER_HEADS * PEER_TOPK
EPS = 1e-6
NEG_INF = -1e30

LANES = 128
SUBLANES = 8
V7X_VMEM_BYTES = 64 * 1024 * 1024
VMEM_LIMIT_LAYER = (3 * V7X_VMEM_BYTES) // 4
VMEM_LIMIT_SMALL = V7X_VMEM_BYTES // 2
ROUTE_TILE = 512
LAYER_TILE = 256
ADA_TILE = 1536
BF16 = jnp.bfloat16
F32 = jnp.float32


_CAND_RUNS = [(a, PEER_TOPK // (a + 1)) for a in range(PEER_TOPK)]
_NCAND = sum(n for _, n in _CAND_RUNS)
_NCAND_PAD = -(-_NCAND // SUBLANES) * SUBLANES
_CAND_FLAT = np.full((_NCAND_PAD,), PEER_TOPK * PEER_TOPK, np.int32)
_off = 0
for _a, _n in _CAND_RUNS:
    _CAND_FLAT[_off:_off + _n] = _a * PEER_TOPK + np.arange(_n)
    _off += _n


def _peer_route_kernel(x_ref, sh_ref, sc_ref, ng_ref, wqt_ref, sk_ref, cand_ref,
                       hn_ref, idx_ref, g_ref,
                       q_s, cs_s, ci_s):
    tt = x_ref.shape[1]
    x = x_ref[0]
    ms = jnp.mean(x * x, axis=-1, keepdims=True)
    hn = x * lax.rsqrt(ms + EPS) * ng_ref[...]
    hn = hn * (1.0 + sc_ref[0]) + sh_ref[0]
    hn_ref[0] = hn
    qt = lax.dot_general(wqt_ref[...], hn.astype(BF16), (((1,), (1,)), ((), ())),
                         preferred_element_type=F32)
    for gidx in range(2 * PEER_HEADS):
        q_s[gidx] = qt[gidx * PEER_HALF:(gidx + 1) * PEER_HALF, :]

    key_id = lax.broadcasted_iota(jnp.int32, (N_KEYS, tt), 0)
    slot_id = lax.broadcasted_iota(jnp.int32, (PEER_TOPK, tt), 0)
    cand = cand_ref[:, 0:tt]
    big = PEER_TOPK * PEER_TOPK
    cs_s[_NCAND:_NCAND_PAD, :] = jnp.full((_NCAND_PAD - _NCAND, tt), -jnp.inf, F32)
    ci_s[_NCAND:_NCAND_PAD, :] = jnp.zeros((_NCAND_PAD - _NCAND, tt), jnp.int32)

    def head_body(h, _):
        s = [jnp.dot(sk_ref[2 * h + p], q_s[2 * h + p].astype(BF16), preferred_element_type=F32)
             for p in range(2)]
        ts = [jnp.zeros((PEER_TOPK, tt), F32) for _ in range(2)]
        ti = [jnp.zeros((PEER_TOPK, tt), jnp.int32) for _ in range(2)]
        for j in range(PEER_TOPK):
            for p in range(2):
                m = jnp.max(s[p], axis=0, keepdims=True)
                am = jnp.min(jnp.where(s[p] == m, key_id, N_KEYS), axis=0, keepdims=True)
                ts[p] = jnp.where(slot_id == j, m, ts[p])
                ti[p] = jnp.where(slot_id == j, am, ti[p])
                s[p] = jnp.where(key_id == am, -jnp.inf, s[p])
        off = 0
        for a, n in _CAND_RUNS:
            cs_s[off:off + n, :] = ts[0][a:a + 1, :] + ts[1][0:n, :]
            ci_s[off:off + n, :] = ti[0][a:a + 1, :] * N_KEYS + ti[1][0:n, :]
            off += n
        cs = cs_s[...]
        ci = ci_s[...]
        fs = jnp.zeros((PEER_TOPK, tt), F32)
        ei = jnp.zeros((PEER_TOPK, tt), jnp.int32)
        for k in range(PEER_TOPK):
            m = jnp.max(cs, axis=0, keepdims=True)
            cmin = jnp.min(jnp.where(cs == m, cand, big), axis=0, keepdims=True)
            sel = cand == cmin
            e = jnp.sum(jnp.where(sel, ci, 0), axis=0, keepdims=True)
            fs = jnp.where(slot_id == k, m, fs)
            ei = jnp.where(slot_id == k, e, ei)
            cs = jnp.where(sel, -jnp.inf, cs)
        ex = jnp.exp(fs - fs[0:1, :])
        g = ex / jnp.sum(ex, axis=0, keepdims=True)
        row0 = pl.multiple_of(h * PEER_TOPK, PEER_TOPK)
        g_ref[0, pl.ds(row0, PEER_TOPK), :] = g
        idx_ref[0, pl.ds(row0, PEER_TOPK), :] = ei
        return 0

    lax.fori_loop(0, PEER_HEADS, head_body, 0)


def _peer_route(x, sh, sc, ng, wqt, sk):
    B, T, D = x.shape
    tt = min(T, ROUTE_TILE)
    grid = (B, T // tt)
    tok = lambda b, t: (b, t, 0)
    tok_t = lambda b, t: (b, 0, t)
    per_b = lambda b, t: (b, 0, 0)
    cand_tab = jnp.asarray(np.broadcast_to(_CAND_FLAT[:, None], (_NCAND_PAD, max(tt, LANES))))
    hn, idx_t, g_t = pl.pallas_call(
        _peer_route_kernel,
        grid=grid,
        in_specs=[
            pl.BlockSpec((1, tt, D), tok),
            pl.BlockSpec((1, 1, D), per_b),
            pl.BlockSpec((1, 1, D), per_b),
            pl.BlockSpec((1, D), lambda b, t: (0, 0)),
            pl.BlockSpec(wqt.shape, lambda b, t: (0, 0)),
            pl.BlockSpec(sk.shape, lambda b, t: (0, 0, 0)),
            pl.BlockSpec(cand_tab.shape, lambda b, t: (0, 0)),
        ],
        out_specs=[
            pl.BlockSpec((1, tt, D), tok),
            pl.BlockSpec((1, PEER_SEL, tt), tok_t),
            pl.BlockSpec((1, PEER_SEL, tt), tok_t),
        ],
        out_shape=[
            jax.ShapeDtypeStruct((B, T, D), F32),
            jax.ShapeDtypeStruct((B, PEER_SEL, T), jnp.int32),
            jax.ShapeDtypeStruct((B, PEER_SEL, T), F32),
        ],
        scratch_shapes=[
            pltpu.VMEM((2 * PEER_HEADS, PEER_HALF, tt), F32),
            pltpu.VMEM((_NCAND_PAD, tt), F32),
            pltpu.VMEM((_NCAND_PAD, tt), jnp.int32),
        ],
        compiler_params=pltpu.CompilerParams(
            dimension_semantics=("parallel", "parallel"), vmem_limit_bytes=VMEM_LIMIT_LAYER),
        name="peer_route",
    )(x, sh, sc, ng, wqt, sk, cand_tab)
    return hn, jnp.swapaxes(idx_t, 1, 2), jnp.swapaxes(g_t, 1, 2)


SC_CORES = 2
SC_SUBCORES = 16
SC_LANES = 16
SC_WORKERS = SC_CORES * SC_SUBCORES
HALF_D = D_MODEL // 2
NWCHUNK = HALF_D // SC_LANES
HI_MASK = -65536
TOK_BLOCKS = (16, 8)
NSLOT = 3


def _pack_expert_tables(u_tab, v_tab):
    def pack(tab):
        bits = lax.bitcast_convert_type(tab, jnp.uint32)
        rne = bits + (jnp.uint32(0x7FFF) + ((bits >> 16) & jnp.uint32(1)))
        rne = jnp.where(tab != tab, jnp.uint32(0x7FC00000), rne)
        return (rne[:, :HALF_D] >> 16) | (rne[:, HALF_D:] & jnp.uint32(0xFFFF0000))

    return lax.bitcast_convert_type(jnp.concatenate([pack(u_tab), pack(v_tab)], axis=1), jnp.int32)


def _unpack_words(words):
    lo = lax.bitcast_convert_type(lax.shift_left(words, jnp.full_like(words, 16)), F32)
    hi = lax.bitcast_convert_type(words & HI_MASK, F32)
    return lo, hi


def _peer_expert_sc(hn, eidx, g, uv_tab, n_tokens):
    N, D = n_tokens, hn.shape[1]
    tpw = N // SC_WORKERS
    assert tpw * SC_WORKERS == N and tpw % min(TOK_BLOCKS) == 0
    tok_block = next(b for b in TOK_BLOCKS if tpw % b == 0)
    nblocks = tpw // tok_block
    steps = tok_block * PEER_HEADS
    mesh = plsc.VectorSubcoreMesh(core_axis_name="c", subcore_axis_name="s",
                                  num_cores=SC_CORES, num_subcores=SC_SUBCORES)

    @functools.partial(
        pl.kernel, out_type=jax.ShapeDtypeStruct((N, D), F32), mesh=mesh,
        scratch_types=[
            pltpu.VMEM((tok_block, D), F32),
            pltpu.VMEM((tok_block, D), F32),
            pltpu.VMEM((tok_block, PEER_SEL), jnp.int32),
            pltpu.VMEM((tok_block, PEER_SEL), F32),
            pltpu.VMEM((NSLOT, PEER_TOPK, D), jnp.int32),
            pltpu.SemaphoreType.DMA((NSLOT,)),
        ],
        compiler_params=pltpu.CompilerParams(needs_layout_passes=False),
        name="peer_expert_sc",
    )
    def k(hn_hbm, idx_hbm, g_hbm, uv_hbm, out_hbm,
          hn_b, o_b, idx_b, g_b, uvbuf, sem):
        wid = lax.axis_index("s") * SC_CORES + lax.axis_index("c")
        tok0 = wid * tpw
        lane = lax.iota(jnp.int32, SC_LANES)

        def gather_desc(s, slot):
            t = s // PEER_HEADS
            h0 = pl.multiple_of((s % PEER_HEADS) * PEER_TOPK, PEER_TOPK)
            ids = idx_b.at[t, pl.ds(h0, PEER_TOPK)]
            return pltpu.make_async_copy(uv_hbm.at[ids], uvbuf.at[slot], sem.at[slot])

        def issue(s, slot):
            gather_desc(s, slot).start()

        def wait(s, slot):
            gather_desc(s, slot).wait()

        def compute(s, slot):
            t = s // PEER_HEADS
            h = s % PEER_HEADS
            h0 = pl.multiple_of(h * PEER_TOPK, PEER_TOPK)

            def ubody(c, accs):
                c0 = pl.multiple_of(c * SC_LANES, SC_LANES)
                x_lo = hn_b[t, pl.ds(c0, SC_LANES)]
                x_hi = hn_b[t, pl.ds(HALF_D + c0, SC_LANES)]
                out = []
                for kk in range(PEER_TOPK):
                    lo, hi = _unpack_words(uvbuf[slot, kk, pl.ds(c0, SC_LANES)])
                    out.append(accs[kk] + (x_lo * lo + x_hi * hi))
                return tuple(out)

            accs = lax.fori_loop(0, NWCHUNK, ubody,
                                 tuple(jnp.zeros((SC_LANES,), F32) for _ in range(PEER_TOPK)))
            last = jnp.full((SC_LANES,), SC_LANES - 1, jnp.int32)
            r = jnp.zeros((SC_LANES,), F32)
            for kk in range(PEER_TOPK):
                tot = jnp.take_along_axis(plsc.cumsum(accs[kk]), last, axis=0)
                r = jnp.where(lane == kk, tot, r)
            gv = g_b[t, pl.ds(h0, PEER_TOPK)]
            z = 0.7978845608028654 * (r + 0.044715 * (r * r * r))
            w = gv * r / (1.0 + jnp.exp(-2.0 * z))
            wk = [jnp.take_along_axis(w, jnp.full((SC_LANES,), kk, jnp.int32), axis=0) for kk in range(PEER_TOPK)]

            def vloop(first_head):
                @plsc.parallel_loop(0, NWCHUNK, unroll=2)
                def _(c):
                    c0 = pl.multiple_of(c * SC_LANES, SC_LANES)
                    los, his = [], []
                    for kk in range(PEER_TOPK):
                        lo, hi = _unpack_words(uvbuf[slot, kk, pl.ds(HALF_D + c0, SC_LANES)])
                        los.append(wk[kk] * lo)
                        his.append(wk[kk] * hi)
                    while len(los) > 1:
                        los = [los[i] + los[i + 1] for i in range(0, len(los), 2)]
                        his = [his[i] + his[i + 1] for i in range(0, len(his), 2)]
                    if first_head:
                        o_b[t, pl.ds(c0, SC_LANES)] = los[0]
                        o_b[t, pl.ds(HALF_D + c0, SC_LANES)] = his[0]
                    else:
                        plsc.addupdate(o_b.at[t, pl.ds(c0, SC_LANES)], los[0])
                        plsc.addupdate(o_b.at[t, pl.ds(HALF_D + c0, SC_LANES)], his[0])

            @pl.when(h == 0)
            def _():
                vloop(True)

            @pl.when(h != 0)
            def _():
                vloop(False)

        @pl.loop(0, nblocks)
        def _(blk):
            tok = pl.multiple_of(tok0 + blk * tok_block, tok_block)
            pltpu.sync_copy(idx_hbm.at[pl.ds(tok, tok_block)], idx_b)
            pltpu.sync_copy(g_hbm.at[pl.ds(tok, tok_block)], g_b)
            pltpu.sync_copy(hn_hbm.at[pl.ds(tok, tok_block)], hn_b)
            for s0 in range(NSLOT - 1):
                issue(s0, s0)

            @pl.loop(0, steps)
            def _(s):
                @pl.when(s + (NSLOT - 1) < steps)
                def _():
                    issue(s + (NSLOT - 1), (s + (NSLOT - 1)) % NSLOT)

                slot = s % NSLOT
                wait(s, slot)
                compute(s, slot)

            pltpu.sync_copy(o_b, out_hbm.at[pl.ds(tok, tok_block)])

    return k(hn, eidx, g, uv_tab)


TC_TOK_BLOCK = 64
TC_ISSUE_UNROLL = 32


def _peer_expert_tc_kernel(idx_hbm, x_ref, hn_ref, g_ref, g2_ref, uv_hbm, out_ref,
                           idx_s, uvbuf, sem, isem, *, first_block):
    tb = x_ref.shape[0]
    row0 = (pl.program_id(0) + first_block) * tb
    cp = pltpu.make_async_copy(idx_hbm.at[pl.ds(row0 * PEER_SEL, tb * PEER_SEL)], idx_s, isem)
    cp.start()
    cp.wait()

    def issue(t, slot):
        def body(jb, _):
            base = t * PEER_SEL + jb * TC_ISSUE_UNROLL
            for jj in range(TC_ISSUE_UNROLL):
                pltpu.make_async_copy(uv_hbm.at[pl.ds(idx_s[base + jj], 1)], uvbuf.at[slot, jb, pl.ds(jj, 1)],
                                      sem.at[slot]).start()
            return 0

        lax.fori_loop(0, PEER_SEL // TC_ISSUE_UNROLL, body, 0)

    def wait(slot):
        pltpu.make_async_copy(uv_hbm.at[pl.ds(0, PEER_SEL)], uvbuf.at[slot].reshape(PEER_SEL, D_MODEL),
                              sem.at[slot]).wait()

    issue(0, 0)
    r2 = lax.broadcasted_iota(jnp.int32, (PEER_SEL, LANES), 0)
    c2 = lax.broadcasted_iota(jnp.int32, (PEER_SEL, LANES), 1)
    diag = r2 == c2

    def tok(t, _):
        slot = t % 2

        @pl.when(t + 1 < tb)
        def _():
            issue(t + 1, 1 - slot)

        wait(slot)
        xr = hn_ref[pl.ds(t, 1), :]
        words = uvbuf[slot].reshape(PEER_SEL, D_MODEL)
        u_lo, u_hi = _unpack_words(words[:, :HALF_D])
        prod = u_lo * xr[:, :HALF_D] + u_hi * xr[:, HALF_D:]
        p = prod[:, 0:LANES]
        for c in range(1, HALF_D // LANES):
            p = p + prod[:, c * LANES:(c + 1) * LANES]
        act = jnp.sum(p, axis=-1, keepdims=True)
        grow = g_ref[pl.ds(t, 1), :]
        gcol = jnp.sum(jnp.where(diag, jnp.broadcast_to(grow, (PEER_SEL, LANES)), 0.0),
                       axis=-1, keepdims=True)
        wgt = gcol * _gelu_tanh(act)
        v_lo, v_hi = _unpack_words(words[:, HALF_D:])
        o = jnp.concatenate([jnp.sum(v_lo * wgt, axis=0, keepdims=True),
                             jnp.sum(v_hi * wgt, axis=0, keepdims=True)], axis=-1)
        out_ref[pl.ds(t, 1), :] = x_ref[pl.ds(t, 1), :] + g2_ref[...] * o
        return 0

    lax.fori_loop(0, tb, tok, 0)


def _peer_expert_tc(x, hn, eidx, g, g2, uv_tab, first_token):
    N, D = x.shape
    tb = TC_TOK_BLOCK
    n_out = N - first_token
    assert first_token % tb == 0 and n_out % tb == 0
    first_block = first_token // tb
    tok = lambda t: (t + first_block, 0)
    return pl.pallas_call(
        functools.partial(_peer_expert_tc_kernel, first_block=first_block),
        grid=(n_out // tb,),
        in_specs=[
            pl.BlockSpec(memory_space=pl.ANY),
            pl.BlockSpec((tb, D), tok),
            pl.BlockSpec((tb, D), tok),
            pl.BlockSpec((tb, PEER_SEL), tok),
            pl.BlockSpec((1, D), lambda t: (0, 0)),
            pl.BlockSpec(memory_space=pl.ANY),
        ],
        out_specs=pl.BlockSpec((tb, D), lambda t: (t, 0)),
        out_shape=jax.ShapeDtypeStruct((n_out, D), F32),
        scratch_shapes=[
            pltpu.SMEM((tb * PEER_SEL,), jnp.int32),
            pltpu.VMEM((2, PEER_SEL // TC_ISSUE_UNROLL, TC_ISSUE_UNROLL, D), jnp.int32),
            pltpu.SemaphoreType.DMA((2,)),
            pltpu.SemaphoreType.DMA,
        ],
        compiler_params=pltpu.CompilerParams(
            dimension_semantics=("arbitrary",), vmem_limit_bytes=VMEM_LIMIT_SMALL,
            disable_bounds_checks=True),
        name="peer_expert_tc",
    )(eidx.reshape(-1), x, hn, g, g2, uv_tab)


TC_SHARE_TOKENS = 1536


def _peer_combine_kernel(*refs, n_head_tiles, has_tail):
    if has_tail:
        x_ref, o_ref, g2_ref, tail_ref, y_ref = refs
    else:
        x_ref, o_ref, g2_ref, y_ref = refs
    i = pl.program_id(0)

    @pl.when(i < n_head_tiles)
    def _():
        y_ref[...] = x_ref[...] + g2_ref[0] * o_ref[...]

    if has_tail:
        @pl.when(i >= n_head_tiles)
        def _():
            y_ref[...] = tail_ref[...]


def _peer_combine(x, o_head, g2, y_tail, tokens_per_seq):
    N, D = x.shape
    nh = o_head.shape[0]
    rows = min(LAYER_TILE, tokens_per_seq)
    assert nh % rows == 0 and N % rows == 0 and tokens_per_seq % rows == 0
    n_head_tiles = nh // rows
    has_tail = y_tail is not None
    assert has_tail == (nh < N)
    in_specs = [
        pl.BlockSpec((rows, D), lambda i: (i, 0)),
        pl.BlockSpec((rows, D), lambda i: (jnp.minimum(i, n_head_tiles - 1), 0)),
        pl.BlockSpec((1, 1, D), lambda i: ((i * rows) // tokens_per_seq, 0, 0)),
    ]
    args = [x, o_head, g2]
    if has_tail:
        in_specs.append(pl.BlockSpec((rows, D), lambda i: (jnp.maximum(i - n_head_tiles, 0), 0)))
        args.append(y_tail)
    return pl.pallas_call(
        functools.partial(_peer_combine_kernel, n_head_tiles=n_head_tiles, has_tail=has_tail),
        grid=(N // rows,),
        in_specs=in_specs,
        out_specs=pl.BlockSpec((rows, D), lambda i: (i, 0)),
        out_shape=jax.ShapeDtypeStruct((N, D), F32),
        compiler_params=pltpu.CompilerParams(
            dimension_semantics=("parallel",), vmem_limit_bytes=VMEM_LIMIT_SMALL),
        name="peer_combine",
    )(*args)


def _peer_layer(x, sh2, sc2, g2, ng, wqt, sk, uv_tab):
    B, T, D = x.shape
    hn, eidx, g = _peer_route(x, sh2, sc2, ng, wqt, sk)
    N = B * T
    x2, hn2, eidx2, g_2 = x.reshape(N, D), hn.reshape(N, D), eidx.reshape(N, PEER_SEL), g.reshape(N, PEER_SEL)
    n_tc = TC_SHARE_TOKENS if (B == 1 and N >= 4 * TC_SHARE_TOKENS) else 0
    o_head = _peer_expert_sc(hn2, eidx2, g_2, uv_tab, n_tokens=N - n_tc)
    y_tail = _peer_expert_tc(x2, hn2, eidx2, g_2, g2.reshape(B, D), uv_tab, first_token=N - n_tc) if n_tc else None
    return _peer_combine(x2, o_head, g2, y_tail, tokens_per_seq=T).reshape(B, T, D)


def _attn_layer_kernel(*refs, tq, has_cache):
    if has_cache:
        (x_ref, sh_ref, sc_ref, g1_ref, ng_ref, wqkv_ref, qg_ref, kg_ref, bias_ref, wo_ref, kc_ref, vc_ref,
         y_ref, ko_ref, vo_ref, kcat, vcat, qn_s, o_s) = refs
    else:
        (x_ref, sh_ref, sc_ref, g1_ref, ng_ref, wqkv_ref, qg_ref, kg_ref, bias_ref, wo_ref,
         y_ref, ko_ref, vo_ref, kcat, vcat, qn_s, o_s) = refs
    t = pl.program_id(1)

    if has_cache:
        kcat[0:LEFT_ROWS, :] = kc_ref[0].astype(BF16)
        vcat[0:LEFT_ROWS, :] = vc_ref[0].astype(BF16)
    else:
        @pl.when(t == 0)
        def _():
            kcat[0:LEFT_ROWS, :] = jnp.zeros((LEFT_ROWS, D_MODEL), BF16)
            vcat[0:LEFT_ROWS, :] = jnp.zeros((LEFT_ROWS, D_MODEL), BF16)

    x = x_ref[0]
    ms = jnp.mean(x * x, axis=-1, keepdims=True)
    hn = x * lax.rsqrt(ms + EPS) * ng_ref[...]
    hn = hn * (1.0 + sc_ref[0]) + sh_ref[0]
    qkv = jnp.dot(hn.astype(BF16), wqkv_ref[...], preferred_element_type=F32)
    for h in range(N_HEADS):
        lo = h * HEAD_DIM
        qh = qkv[:, lo:lo + HEAD_DIM]
        kh = qkv[:, D_MODEL + lo:D_MODEL + lo + HEAD_DIM]
        qh = qh * lax.rsqrt(jnp.mean(qh * qh, axis=-1, keepdims=True) + EPS) * qg_ref[...]
        kh = kh * lax.rsqrt(jnp.mean(kh * kh, axis=-1, keepdims=True) + EPS) * kg_ref[...]
        qn_s[:, lo:lo + HEAD_DIM] = qh.astype(BF16)
        ko_ref[0, :, lo:lo + HEAD_DIM] = kh
        kcat[LEFT_ROWS:LEFT_ROWS + tq, lo:lo + HEAD_DIM] = kh.astype(BF16)
    v = qkv[:, 2 * D_MODEL:]
    vo_ref[0] = v
    vcat[LEFT_ROWS:LEFT_ROWS + tq, :] = v.astype(BF16)

    scale = HEAD_DIM ** -0.5
    if not has_cache:
        kpos = lax.broadcasted_iota(jnp.int32, (tq, LEFT_ROWS + tq), 1) + (t * tq - LEFT_ROWS)
        valid = kpos >= 0
    for h in range(N_HEADS):
        lo = h * HEAD_DIM
        s = lax.dot_general(qn_s[:, lo:lo + HEAD_DIM], kcat[:, lo:lo + HEAD_DIM], (((1,), (1,)), ((), ())),
                            preferred_element_type=F32) * scale
        s = s + bias_ref[h]
        if not has_cache:
            s = jnp.where(valid, s, NEG_INF)
        m = jnp.max(s, axis=-1, keepdims=True)
        e = jnp.exp(s - m)
        p = e / jnp.sum(e, axis=-1, keepdims=True)
        o_s[:, lo:lo + HEAD_DIM] = jnp.dot(p.astype(BF16), vcat[:, lo:lo + HEAD_DIM],
                                           preferred_element_type=F32).astype(BF16)

    out = jnp.dot(o_s[...], wo_ref[...], preferred_element_type=F32)
    y_ref[0] = x + g1_ref[0] * out

    if not has_cache:
        kcat[0:LEFT_ROWS, :] = kcat[tq:tq + LEFT_ROWS, :]
        vcat[0:LEFT_ROWS, :] = vcat[tq:tq + LEFT_ROWS, :]


def _rel_bias_tile(rel_bias, tq, ch):
    i = jnp.arange(tq)[:, None]
    r = jnp.arange(LEFT_ROWS + tq)[None, :]
    start = (i // ch) * ch
    in_band = (r >= start) & (r < start + LEFT_ROWS + ch)
    rel = jnp.clip(i + LEFT_ROWS - r, -REL_CLIP, REL_CLIP) + REL_CLIP
    return jnp.where(in_band[None], rel_bias[:, rel], NEG_INF)


def _attn_layer(x, sh, sc, g1, ng, wqkv, qg, kg, rel_bias, wo, k_cache=None, v_cache=None):
    B, T, D = x.shape
    has_cache = k_cache is not None
    if has_cache:
        tq, ch = T, T
        keep_blocks = 1
    else:
        tq, ch = LAYER_TILE, CHUNK
        keep_blocks = LEFT_ROWS // tq
    nt = T // tq
    assert nt >= keep_blocks
    bias = _rel_bias_tile(rel_bias, tq, ch)
    tok = lambda b, t: (b, t, 0)
    per_b = lambda b, t: (b, 0, 0)
    c2 = lambda b, t: (0, 0)
    c3 = lambda b, t: (0, 0, 0)
    keep = lambda b, t: (b, jnp.maximum(t - (nt - keep_blocks), 0), 0)
    in_specs = [
        pl.BlockSpec((1, tq, D), tok),
        pl.BlockSpec((1, 1, D), per_b), pl.BlockSpec((1, 1, D), per_b), pl.BlockSpec((1, 1, D), per_b),
        pl.BlockSpec((1, D), c2),
        pl.BlockSpec(wqkv.shape, c2),
        pl.BlockSpec((1, HEAD_DIM), c2), pl.BlockSpec((1, HEAD_DIM), c2),
        pl.BlockSpec(bias.shape, c3),
        pl.BlockSpec(wo.shape, c2),
    ]
    args = [x, sh, sc, g1, ng, wqkv, qg, kg, bias, wo]
    if has_cache:
        in_specs += [pl.BlockSpec((1, LEFT_ROWS, D), per_b), pl.BlockSpec((1, LEFT_ROWS, D), per_b)]
        args += [k_cache, v_cache]
    return pl.pallas_call(
        functools.partial(_attn_layer_kernel, tq=tq, has_cache=has_cache),
        grid=(B, nt),
        in_specs=in_specs,
        out_specs=[pl.BlockSpec((1, tq, D), tok), pl.BlockSpec((1, tq, D), keep), pl.BlockSpec((1, tq, D), keep)],
        out_shape=[jax.ShapeDtypeStruct((B, T, D), F32),
                   jax.ShapeDtypeStruct((B, keep_blocks * tq, D), F32),
                   jax.ShapeDtypeStruct((B, keep_blocks * tq, D), F32)],
        scratch_shapes=[
            pltpu.VMEM((LEFT_ROWS + tq, D), BF16),
            pltpu.VMEM((LEFT_ROWS + tq, D), BF16),
            pltpu.VMEM((tq, D), BF16),
            pltpu.VMEM((tq, D), BF16),
        ],
        compiler_params=pltpu.CompilerParams(
            dimension_semantics=("parallel", "arbitrary"), vmem_limit_bytes=VMEM_LIMIT_LAYER),
        name="attn_layer",
    )(*args)


def _gelu_tanh(x):
    return 0.5 * x * (1.0 + jnp.tanh(0.7978845608028654 * (x + 0.044715 * (x * x * x))))


def _lru_layer_kernel(x_ref, sh_ref, sc_ref, g1_ref, ng_ref, win_ref, cw_ref, cb_ref, gw_ref, gb_ref, lam_ref,
                      wout_ref, cs0_ref, h0_ref,
                      y_ref, cs_ref, hl_ref,
                      ext_s, a_s, b_s, hcar_s):
    tt = x_ref.shape[1]
    W = LRU_WIDTH
    t = pl.program_id(1)

    @pl.when(t == 0)
    def _():
        ext_s[0:SUBLANES, :] = cs0_ref[0]
        hcar_s[...] = h0_ref[0]

    x = x_ref[0]
    ms = jnp.mean(x * x, axis=-1, keepdims=True)
    hn = x * lax.rsqrt(ms + EPS) * ng_ref[...]
    hn = hn * (1.0 + sc_ref[0]) + sh_ref[0]
    proj = jnp.dot(hn.astype(BF16), win_ref[...], preferred_element_type=F32)
    gate = _gelu_tanh(proj[:, :W])
    rec = proj[:, W:]
    ext_s[SUBLANES:SUBLANES + tt, :] = rec
    u = cb_ref[...] + cw_ref[3:4, :] * rec
    for k in range(CONV_W - 1):
        u = u + cw_ref[k:k + 1, :] * ext_s[SUBLANES - 3 + k:SUBLANES - 3 + k + tt, :]
    tail = ext_s[tt:tt + SUBLANES, :]
    cs_ref[0] = tail
    ext_s[0:SUBLANES, :] = tail

    sp_in = -lam_ref[...]
    softplus = jnp.maximum(sp_in, 0.0) + jnp.log1p(jnp.exp(-jnp.abs(sp_in)))
    for n in range(LRU_BLOCKS):
        lo = n * LRU_BLOCK_W
        ub = u[:, lo:lo + LRU_BLOCK_W]
        ubb = ub.astype(BF16)
        gr = jnp.dot(ubb, gw_ref[0, n], preferred_element_type=F32) + gb_ref[0:1, lo:lo + LRU_BLOCK_W]
        gi = jnp.dot(ubb, gw_ref[1, n], preferred_element_type=F32) + gb_ref[1:2, lo:lo + LRU_BLOCK_W]
        r = jax.nn.sigmoid(gr)
        i = jax.nn.sigmoid(gi)
        log_a = -RG_C * r * softplus[:, lo:lo + LRU_BLOCK_W]
        a = jnp.exp(log_a)
        a_s[:, lo:lo + LRU_BLOCK_W] = a
        b_s[:, lo:lo + LRU_BLOCK_W] = jnp.sqrt(1.0 - a * a) * i * ub

    def group(gi_, h):
        r0 = pl.multiple_of(gi_ * SUBLANES, SUBLANES)
        a8 = a_s[pl.ds(r0, SUBLANES), :]
        b8 = b_s[pl.ds(r0, SUBLANES), :]
        rows = []
        for j in range(SUBLANES):
            h = a8[j:j + 1, :] * h + b8[j:j + 1, :]
            rows.append(h)
        b_s[pl.ds(r0, SUBLANES), :] = jnp.concatenate(rows, axis=0)
        return h

    h_last = lax.fori_loop(0, tt // SUBLANES, group, hcar_s[...])
    hcar_s[...] = h_last
    hl_ref[0] = h_last
    yv = b_s[...] * gate
    out = jnp.dot(yv.astype(BF16), wout_ref[...], preferred_element_type=F32)
    y_ref[0] = x + g1_ref[0] * out


def _lru_layer(x, sh, sc, g1, ng, w_in, conv_w, conv_b, gate_w, gate_b, lam, w_out, conv_state, h0):
    B, T, D = x.shape
    W = LRU_WIDTH
    tt = min(T, LAYER_TILE)
    cs0 = jnp.pad(conv_state, ((0, 0), (SUBLANES - (CONV_W - 1), 0), (0, 0)))
    tok = lambda b, t: (b, t, 0)
    per_b = lambda b, t: (b, 0, 0)
    c2 = lambda b, t: (0, 0)
    y, cs, hl = pl.pallas_call(
        _lru_layer_kernel,
        grid=(B, T // tt),
        in_specs=[
            pl.BlockSpec((1, tt, D), tok),
            pl.BlockSpec((1, 1, D), per_b), pl.BlockSpec((1, 1, D), per_b), pl.BlockSpec((1, 1, D), per_b),
            pl.BlockSpec((1, D), c2),
            pl.BlockSpec(w_in.shape, c2),
            pl.BlockSpec((CONV_W, W), c2), pl.BlockSpec((1, W), c2),
            pl.BlockSpec(gate_w.shape, lambda b, t: (0, 0, 0, 0)),
            pl.BlockSpec((2, W), c2), pl.BlockSpec((1, W), c2),
            pl.BlockSpec(w_out.shape, c2),
            pl.BlockSpec((1, SUBLANES, W), per_b), pl.BlockSpec((1, 1, W), per_b),
        ],
        out_specs=[pl.BlockSpec((1, tt, D), tok), pl.BlockSpec((1, SUBLANES, W), per_b),
                   pl.BlockSpec((1, 1, W), per_b)],
        out_shape=[jax.ShapeDtypeStruct((B, T, D), F32), jax.ShapeDtypeStruct((B, SUBLANES, W), F32),
                   jax.ShapeDtypeStruct((B, 1, W), F32)],
        scratch_shapes=[
            pltpu.VMEM((tt + SUBLANES, W), F32),
            pltpu.VMEM((tt, W), F32),
            pltpu.VMEM((tt, W), F32),
            pltpu.VMEM((1, W), F32),
        ],
        compiler_params=pltpu.CompilerParams(
            dimension_semantics=("parallel", "arbitrary"), vmem_limit_bytes=VMEM_LIMIT_LAYER),
        name="lru_layer",
    )(x, sh, sc, g1, ng, w_in, conv_w, conv_b, gate_w, gate_b, lam, w_out, cs0, h0[:, None, :])
    return y, cs[:, SUBLANES - (CONV_W - 1):], hl[:, 0]


def _ada_kernel(c_ref, w_ref, b_ref, m_ref):
    c = c_ref[...]
    s = c * jax.nn.sigmoid(c)
    m_ref[0] = jnp.dot(s, w_ref[0], preferred_element_type=F32, precision=lax.Precision.HIGHEST) + b_ref[0]


def _ada_all(c_all, ada_w, ada_b):
    L, D, E = ada_w.shape
    R = c_all.shape[0]
    tn = ADA_TILE
    return pl.pallas_call(
        _ada_kernel,
        grid=(L, E // tn),
        in_specs=[pl.BlockSpec((R, D), lambda l, j: (0, 0)),
                  pl.BlockSpec((1, D, tn), lambda l, j: (l, 0, j)),
                  pl.BlockSpec((1, 1, tn), lambda l, j: (l, 0, j))],
        out_specs=pl.BlockSpec((1, R, tn), lambda l, j: (l, 0, j)),
        out_shape=jax.ShapeDtypeStruct((L, R, E), F32),
        compiler_params=pltpu.CompilerParams(
            dimension_semantics=("parallel", "parallel"), vmem_limit_bytes=VMEM_LIMIT_SMALL),
        name="ada_mod",
    )(c_all, ada_w, ada_b[:, None, :])


class _Chain:
    def __init__(self, x, mods, conv_st, lru_st, k_cache, v_cache, w, prompt):
        self.x, self.mods, self.w, self.prompt = x, mods, w, prompt
        self.conv_st, self.lru_st, self.k_cache, self.v_cache = conv_st, lru_st, k_cache, v_cache
        self.new_conv, self.new_h, self.new_k, self.new_v = [], [], [], []
        self.layer = 0

    def step(self):
        i, w, x = self.layer, self.w, self.x
        B, T, D = x.shape
        sh1, sc1, g1, sh2, sc2, g2 = self.mods[i]
        j = i // 2
        if i % 2 == 0:
            if self.prompt:
                cs0 = jnp.zeros((B, CONV_W - 1, LRU_WIDTH), x.dtype)
                h0 = jnp.zeros((B, LRU_WIDTH), F32)
            else:
                cs0, h0 = self.conv_st[j], self.lru_st[j]
            x, cnew, hnew = _lru_layer(x, sh1, sc1, g1, w['norm_g'][i, 0][None, :], w['lru_w_in'][j],
                                       w['lru_conv_w'][j], w['lru_conv_b'][j][None, :], w['lru_gate_w'][j],
                                       w['lru_gate_b'][j], w['lru_lambda'][j][None, :], w['lru_w_out'][j], cs0, h0)
            self.new_conv.append(cnew)
            self.new_h.append(hnew)
        else:
            if self.prompt:
                kc = vc = None
            else:
                kc = self.k_cache[j].reshape(B, LEFT_ROWS, D)
                vc = self.v_cache[j].reshape(B, LEFT_ROWS, D)
            x, kn, vn = _attn_layer(x, sh1, sc1, g1, w['norm_g'][i, 0][None, :], w['att_w_qkv'][j],
                                    w['att_q_gain'][j][None, :], w['att_k_gain'][j][None, :], w['att_rel_bias'][j],
                                    w['att_w_o'][j], kc, vc)
            self.new_k.append(kn.reshape(B, kn.shape[1], N_HEADS, HEAD_DIM))
            self.new_v.append(vn.reshape(B, vn.shape[1], N_HEADS, HEAD_DIM))
        self.x = _peer_layer(x, sh2, sc2, g2, w['norm_g'][i, 1][None, :], w['peer_wqt'][i], w['peer_sk'][i],
                             w['peer_uv'](i, x))
        self.layer += 1

    def results(self):
        return (self.x, jnp.stack(self.new_conv), jnp.stack(self.new_h), jnp.stack(self.new_k),
                jnp.stack(self.new_v))


def _run_chains(chains, depth):
    for wave in range(depth + len(chains) - 1):
        for ci, chain in enumerate(chains):
            if 0 <= wave - ci < depth:
                chain.step()


def kernel(x_prompt, x_sample, c_prompt, c_sample, state_conv, state_lru_h, cache_k, cache_v, norm_g, ada_w, ada_b, lru_w_in, lru_conv_w, lru_conv_b, lru_gate_w, lru_gate_b, lru_lambda, lru_w_out, att_w_qkv, att_q_gain, att_k_gain, att_rel_bias, att_w_o, peer_w_query, peer_sub_keys, peer_u, peer_v):
    depth = norm_g.shape[0]
    bp, bs = c_prompt.shape[0], c_sample.shape[0]
    rows = -(-(bp + bs) // SUBLANES) * SUBLANES
    c_all = jnp.concatenate([c_prompt, c_sample, jnp.zeros((rows - bp - bs, D_MODEL), F32)], axis=0)
    m_all = _ada_all(c_all, ada_w, ada_b)

    def mods(lo, n):
        return [[m_all[i, lo:lo + n, None, k * D_MODEL:(k + 1) * D_MODEL] for k in range(6)] for i in range(depth)]

    w = {
        'norm_g': norm_g,
        'lru_w_in': lru_w_in.astype(BF16), 'lru_conv_w': lru_conv_w, 'lru_conv_b': lru_conv_b,
        'lru_gate_w': lru_gate_w.astype(BF16), 'lru_gate_b': lru_gate_b, 'lru_lambda': lru_lambda,
        'lru_w_out': lru_w_out.astype(BF16),
        'att_w_qkv': att_w_qkv.astype(BF16), 'att_q_gain': att_q_gain, 'att_k_gain': att_k_gain,
        'att_rel_bias': att_rel_bias, 'att_w_o': att_w_o.astype(BF16),
        'peer_wqt': jnp.swapaxes(peer_w_query, 1, 2).astype(BF16),
        'peer_sk': peer_sub_keys.astype(BF16).reshape(depth, 2 * PEER_HEADS, N_KEYS, PEER_HALF),
    }
    packed = {}

    def peer_uv(i, anchor):
        if i not in packed:
            u_tab, v_tab = peer_u[i], peer_v[i]
            if i > 0:
                u_tab, v_tab, _ = lax.optimization_barrier((u_tab, v_tab, anchor))
            packed[i] = _pack_expert_tables(u_tab, v_tab)
        return packed[i]

    w['peer_uv'] = peer_uv
    prompt_chains = [_Chain(x_prompt[b:b + 1], mods(b, 1), None, None, None, None, w, True) for b in range(bp)]
    sample_chain = _Chain(x_sample, mods(bp, bs), state_conv, state_lru_h, cache_k, cache_v, w, False)
    _run_chains(prompt_chains + [sample_chain], depth)
    y_prompt, p_conv, p_h, p_k, p_v = (
        jnp.concatenate(parts, axis=ax)
        for parts, ax in zip(zip(*(c.results() for c in prompt_chains)), (0, 1, 1, 1, 1)))
    y_sample, s_conv, s_h, s_k, s_v = sample_chain.results()
    return (y_prompt, y_sample, p_conv, p_h, p_k, p_v, s_conv, s_h, s_k, s_v)
```
